```python
import math
import jax, jax.numpy as jnp
from jax import lax
import numpy as np

D_MODEL = 1024
BATCH = 8
SEQ = 4096
DEPTH = 1

PLE_DIM = 256
HEAD_DIM = 64
A_Q_HEADS = 8
A_KV_HEADS = 2
A_GROUP = A_Q_HEADS // A_KV_HEADS
A_WINDOW = 128
B_HEADS = 8
B_PATTERNS = ((128, 1), (512, 4), (2048, 16))
N_HEADS_TOTAL = A_Q_HEADS + B_HEADS
A_Q = A_Q_HEADS * HEAD_DIM
A_KV = A_KV_HEADS * HEAD_DIM
B_W = B_HEADS * HEAD_DIM
D_IN = A_Q + 2 * A_KV + 3 * B_W
D_MIX = A_Q + B_W
D_FF = 2816
NUM_BUCKETS = 32
MAX_DISTANCE = 2048
BLOCK = 128
EPS = 1e-6
NEG_INF = -1e30

kernel_name = "hymba_swa_sink_dilated_macaron_layer"


def rms_norm(x, g):
    xf = x.astype(jnp.float32)
    y = xf * lax.rsqrt(jnp.mean(xf * xf, axis=-1, keepdims=True) + EPS)
    return (y * g.astype(jnp.float32)).astype(x.dtype)


def swiglu(x, w_gu, w_down):
    g, u = jnp.split(x @ w_gu, 2, axis=-1)
    return (jax.nn.silu(g) * u) @ w_down


def t5_bucket(dist):
    max_exact = NUM_BUCKETS // 2
    n = jnp.maximum(dist, 0)
    nf = jnp.maximum(n, 1).astype(jnp.float32)
    large = max_exact + (jnp.log(nf / max_exact) / math.log(MAX_DISTANCE / max_exact)
                         * (NUM_BUCKETS - max_exact)).astype(jnp.int32)
    large = jnp.minimum(large, NUM_BUCKETS - 1)
    return jnp.where(n < max_exact, n, large)


def banded_attention(q, k, v, rel_bias, max_dist, stride):
    n, L, hkv, grp, dh = q.shape
    bq = math.gcd(L, BLOCK)
    nb = L // bq
    nk = bq + max_dist
    pad = ((0, 0), (max_dist, 0), (0, 0), (0, 0))
    k_pad = jnp.pad(k, pad)
    v_pad = jnp.pad(v, pad)
    key_idx = jnp.arange(nb)[:, None] * bq + jnp.arange(nk)[None, :]
    kb = k_pad[:, key_idx]
    vb = v_pad[:, key_idx]
    qb = q.reshape(n, nb, bq, hkv, grp, dh)
    logits = jnp.einsum('nbqhgd,nbkhd->nbhgqk', qb, kb,
                        preferred_element_type=jnp.float32) * (dh ** -0.5)
    rel = jnp.arange(bq)[:, None] + max_dist - jnp.arange(nk)[None, :]
    bias = rel_bias[t5_bucket(rel * stride)].astype(jnp.float32)
    bias = bias.reshape(bq, nk, hkv, grp).transpose(2, 3, 0, 1)
    in_band = (rel >= 0) & (rel <= max_dist)
    key_pos = key_idx - max_dist
    valid = in_band[None] & (key_pos >= 0)[:, None, :]
    logits = jnp.where(valid[None, :, None, None], logits + bias, NEG_INF)
    m = jnp.max(logits, axis=-1)
    pr = jnp.exp(logits - m[..., None])
    s = jnp.sum(pr, axis=-1)
    o = jnp.einsum('nbhgqk,nbkhd->nbqhgd', pr, vb.astype(jnp.float32))
    o = o.reshape(n, L, hkv, grp, dh)
    m = m.transpose(0, 1, 4, 2, 3).reshape(n, L, hkv, grp)
    s = s.transpose(0, 1, 4, 2, 3).reshape(n, L, hkv, grp)
    return o, m, s


def to_classes(t, r):
    b, s = t.shape[:2]
    rest = t.shape[2:]
    t = jnp.moveaxis(t.reshape((b, s // r, r) + rest), 2, 1)
    return t.reshape((b * r, s // r) + rest)


def from_classes(t, b, r):
    n, L = t.shape[:2]
    rest = t.shape[2:]
    t = jnp.moveaxis(t.reshape((b, r, L) + rest), 1, 2)
    return t.reshape((b, L * r) + rest)


def sink_swa_gqa(q_a, k_a, v_a, sinks, rel_bias_a):
    b, s, _ = q_a.shape
    q = q_a.reshape(b, s, A_KV_HEADS, A_GROUP, HEAD_DIM)
    k = k_a.reshape(b, s, A_KV_HEADS, HEAD_DIM)
    v = v_a.reshape(b, s, A_KV_HEADS, HEAD_DIM)
    o, m, den = banded_attention(q, k, v, rel_bias_a, A_WINDOW - 1, 1)
    sink = sinks.reshape(A_KV_HEADS, A_GROUP).astype(jnp.float32)
    m_all = jnp.maximum(m, sink)
    scale = jnp.exp(m - m_all)
    total = den * scale + jnp.exp(sink - m_all)
    o = o * (scale / total)[..., None]
    return o.reshape(b, s, A_Q)


def dilated_mixture(q_b, k_b, v_b, rel_bias_b):
    b, s, _ = q_b.shape
    q = q_b.reshape(b, s, B_HEADS, 1, HEAD_DIM)
    k = k_b.reshape(b, s, B_HEADS, HEAD_DIM)
    v = v_b.reshape(b, s, B_HEADS, HEAD_DIM)
    outs, maxes, dens = [], [], []
    for window, dil in B_PATTERNS:
        o, m, den = banded_attention(to_classes(q, dil), to_classes(k, dil), to_classes(v, dil),
                                     rel_bias_b, window // dil, dil)
        outs.append(from_classes(o, b, dil))
        maxes.append(from_classes(m, b, dil))
        dens.append(from_classes(den, b, dil))
    m_stack = jnp.stack(maxes)
    m_all = jnp.max(m_stack, axis=0)
    w = jnp.exp(m_stack - m_all)
    num = jnp.sum(w[..., None] * jnp.stack(outs), axis=0)
    den_all = jnp.sum(w * jnp.stack(dens), axis=0)
    return (num / den_all[..., None]).reshape(b, s, B_W)


def setup_inputs(seed: int = 0) -> dict:
    key = jax.random.key(seed)
    ks = jax.random.split(key, 24)
    f32 = jnp.float32

    def w(k, shape, fan_in):
        return jax.random.normal(k, shape, f32) * (fan_in ** -0.5)

    def gain(k, shape):
        return 1.0 + 0.05 * jax.random.normal(k, shape, f32)

    def small(k, shape, scale=0.02):
        return scale * jax.random.normal(k, shape, f32)

    D = D_MODEL
    return {
        "x": jax.random.normal(ks[0], (BATCH, SEQ, D), f32),
        "p": jax.random.normal(ks[1], (DEPTH, BATCH, SEQ, PLE_DIM), f32),
        "rel_bias": small(ks[2], (NUM_BUCKETS, N_HEADS_TOTAL), 0.5),
        "ffn1_pre_g": gain(ks[3], (DEPTH, D)),
        "ffn1_w_gu": w(ks[4], (DEPTH, D, 2 * D_FF), D),
        "ffn1_w_down": w(ks[5], (DEPTH, D_FF, D), D_FF),
        "ffn1_post_g": gain(ks[6], (DEPTH, D)),
        "attn_pre_g": gain(ks[7], (DEPTH, D)),
        "w_in": w(ks[8], (DEPTH, D, D_IN), D),
        "b_in": small(ks[9], (DEPTH, D_IN)),
        "sinks": small(ks[10], (DEPTH, A_Q_HEADS), 0.5),
        "w_out": w(ks[11], (DEPTH, D_MIX, D), D_MIX),
        "b_out": small(ks[12], (DEPTH, D)),
        "attn_post_g": gain(ks[13], (DEPTH, D)),
        "ffn2_pre_g": gain(ks[14], (DEPTH, D)),
        "ffn2_w_gu": w(ks[15], (DEPTH, D, 2 * D_FF), D),
        "ffn2_w_down": w(ks[16], (DEPTH, D_FF, D), D_FF),
        "ffn2_post_g": gain(ks[17], (DEPTH, D)),
        "ple_pre_g": gain(ks[18], (DEPTH, D)),
        "w_ple_gate": w(ks[19], (DEPTH, D, D), D),
        "w_ple_proj": w(ks[20], (DEPTH, PLE_DIM, D), PLE_DIM),
        "ple_post_g": gain(ks[21], (DEPTH, D)),
    }


def reference(x, p, rel_bias, ffn1_pre_g, ffn1_w_gu, ffn1_w_down, ffn1_post_g,
              attn_pre_g, w_in, b_in, sinks, w_out, b_out, attn_post_g,
              ffn2_pre_g, ffn2_w_gu, ffn2_w_down, ffn2_post_g,
              ple_pre_g, w_ple_gate, w_ple_proj, ple_post_g):
    h = x
    splits = [A_Q, A_Q + A_KV, A_Q + 2 * A_KV, A_Q + 2 * A_KV + B_W, A_Q + 2 * A_KV + 2 * B_W]
    for i in range(DEPTH):
        f = swiglu(rms_norm(h, ffn1_pre_g[i]), ffn1_w_gu[i], ffn1_w_down[i])
        h = h + 0.5 * rms_norm(f, ffn1_post_g[i])

        z = rms_norm(h, attn_pre_g[i]) @ w_in[i] + b_in[i]
        q_a, k_a, v_a, q_b, k_b, v_b = jnp.split(z, splits, axis=-1)
        out_a = sink_swa_gqa(q_a, k_a, v_a, sinks[i], rel_bias[:, :A_Q_HEADS])
        out_b = dilated_mixture(q_b, k_b, v_b, rel_bias[:, A_Q_HEADS:])
        mix = jnp.concatenate([out_a, out_b], axis=-1).astype(x.dtype)
        att = mix @ w_out[i] + b_out[i]
        h = h + rms_norm(att, attn_post_g[i])

        f = swiglu(rms_norm(h, ffn2_pre_g[i]), ffn2_w_gu[i], ffn2_w_down[i])
        h = h + 0.5 * rms_norm(f, ffn2_post_g[i])

        gate = jax.nn.sigmoid(rms_norm(h, ple_pre_g[i]) @ w_ple_gate[i])
        e = p[i] @ w_ple_proj[i]
        h = h + rms_norm(gate * e, ple_post_g[i])
    return h
```

```python
import functools
import math

import numpy as np
import jax
import jax.numpy as jnp
from jax import lax
from jax.experimental import pallas as pl
from jax.experimental.pallas import tpu as pltpu

HEAD_DIM = 64
A_Q_HEADS = 8
A_KV_HEADS = 2
A_WINDOW = 128
B_HEADS = 8
B_PATTERNS = ((128, 1), (512, 4), (2048, 16))
NUM_BUCKETS = 32
MAX_DISTANCE = 2048
EPS = 1e-6
NEG_INF = -1e30

ATTN_BLOCK = 128
LANE_BLOCK = 256
HEADS_PER_LANE_BLOCK = LANE_BLOCK // HEAD_DIM
FF_CHUNK = 256
TOKEN_TILE = 512
ATTN_ROWS = 512
VMEM_LIMIT_TOKEN_STAGES = 56 * 1024 * 1024
VMEM_LIMIT_ATTENTION = 40 * 1024 * 1024

F32 = jnp.float32
BF16 = jnp.bfloat16


def _rms(x, g):
    return x * lax.rsqrt(jnp.mean(x * x, axis=-1, keepdims=True) + EPS) * g


def _sigmoid(x):
    return 1.0 / (1.0 + jnp.exp(-x))


def _swiglu(xn_ref, wgu_ref, wd_ref, act_ref):
    d_ff = wd_ref.shape[0]
    for c in range(d_ff // FF_CHUNK):
        lo = c * FF_CHUNK
        g = jnp.dot(xn_ref[...], wgu_ref[:, lo:lo + FF_CHUNK], preferred_element_type=F32)
        u = jnp.dot(xn_ref[...], wgu_ref[:, d_ff + lo:d_ff + lo + FF_CHUNK],
                    preferred_element_type=F32)
        act_ref[:, lo:lo + FF_CHUNK] = ((g * _sigmoid(g)) * u).astype(BF16)
    return jnp.dot(act_ref[...], wd_ref[...], preferred_element_type=F32)


def _t5_bucket_np(dist):
    max_exact = NUM_BUCKETS // 2
    n = np.maximum(dist, 0)
    nf = np.maximum(n, 1).astype(np.float32)
    large = max_exact + (np.log(nf / np.float32(max_exact))
                         / np.float32(math.log(MAX_DISTANCE / max_exact))
                         * np.float32(NUM_BUCKETS - max_exact)).astype(np.int32)
    large = np.minimum(large, NUM_BUCKETS - 1)
    return np.where(n < max_exact, n, large).astype(np.int32)


def _bucket_tables():
    q = np.arange(ATTN_BLOCK)[:, None]
    k = np.arange(2 * ATTN_BLOCK)[None, :]
    rel = ATTN_BLOCK + q - k
    tables = []
    for max_dist, stride in ((A_WINDOW - 1, 1),) + tuple((w // d, d) for w, d in B_PATTERNS):
        valid = (rel >= 0) & (rel <= max_dist)
        bkt = _t5_bucket_np(rel * stride)
        later = np.where(valid, bkt, -1)
        first = np.where(valid & (k >= ATTN_BLOCK), bkt, -1)
        tables.append(np.stack([first, later]))
    return np.stack(tables).astype(np.int32)


def _bias_kernel(rb_ref, bkt_ref, out_ref):
    pattern = pl.program_id(0)
    head = pl.program_id(1)
    col = jnp.where(pattern == 0, head, head + A_Q_HEADS)
    bkt = bkt_ref[...]
    acc = jnp.full(bkt.shape, NEG_INF, F32)
    for b in range(NUM_BUCKETS):
        acc = jnp.where(bkt == b, rb_ref[b, col], acc)
    out_ref[...] = acc


def _bias_tables(rel_bias):
    bkt = jnp.asarray(_bucket_tables())
    n_pat = bkt.shape[0]
    out = pl.pallas_call(
        _bias_kernel,
        grid=(n_pat, A_Q_HEADS),
        in_specs=[
            pl.BlockSpec(memory_space=pltpu.SMEM),
            pl.BlockSpec((None, 2, ATTN_BLOCK, 2 * ATTN_BLOCK), lambda p, h: (p, 0, 0, 0)),
        ],
        out_specs=pl.BlockSpec((None, None, 2, ATTN_BLOCK, 2 * ATTN_BLOCK),
                               lambda p, h: (p, h, 0, 0, 0)),
        out_shape=jax.ShapeDtypeStruct((n_pat, A_Q_HEADS, 2, ATTN_BLOCK, 2 * ATTN_BLOCK), F32),
        compiler_params=pltpu.CompilerParams(dimension_semantics=("arbitrary", "arbitrary")),
        name="bias_tables",
    )(rel_bias, bkt)
    return out.reshape(n_pat, A_Q_HEADS * 2, ATTN_BLOCK, 2 * ATTN_BLOCK)


def _stage1_kernel(x_ref, g_pre_ref, wgu_ref, wd_ref, g_post_ref, g_attn_ref, win_ref, bin_ref,
                   h_ref, qb_ref, kb_ref, vb_ref, qa_ref, kva_ref, xn_ref, act_ref):
    xn_ref[...] = _rms(x_ref[...], g_pre_ref[...]).astype(BF16)
    f = _swiglu(xn_ref, wgu_ref, wd_ref, act_ref)
    h = x_ref[...] + 0.5 * _rms(f, g_post_ref[...])
    h_ref[...] = h
    xn_ref[...] = _rms(h, g_attn_ref[...]).astype(BF16)

    def proj(col_block):
        lo = col_block * LANE_BLOCK
        return (jnp.dot(xn_ref[...], win_ref[:, lo:lo + LANE_BLOCK], preferred_element_type=F32)
                + bin_ref[:, lo:lo + LANE_BLOCK])

    q_scale = HEAD_DIM ** -0.5
    for j in range(2):
        qa_ref[j] = (proj(j) * q_scale).astype(BF16)
        qb_ref[j] = (proj(3 + j) * q_scale).astype(BF16)
        kb_ref[j] = proj(5 + j).astype(BF16)
        vb_ref[j] = proj(7 + j).astype(BF16)
    kva_ref[...] = proj(2).astype(BF16)


def _const_spec(shape):
    return pl.BlockSpec(shape, lambda *_: (0,) * len(shape), pipeline_mode=pl.Buffered(1))


def _stage1(x2, g_pre, wgu, wd, g_post, g_attn, w_in, b_in, batch, seq):
    tokens, d_model = x2.shape
    tm = TOKEN_TILE
    tiles_per_seq = seq // tm
    d_ff = wd.shape[0]

    def head_spec(n_blocks):
        return pl.BlockSpec((None, n_blocks, tm, LANE_BLOCK),
                            lambda i: (i // tiles_per_seq, 0, i % tiles_per_seq, 0))

    def head_shape(n_blocks):
        return jax.ShapeDtypeStruct((batch, n_blocks, seq, LANE_BLOCK), BF16)

    row_spec = pl.BlockSpec((tm, d_model), lambda i: (i, 0))
    return pl.pallas_call(
        _stage1_kernel,
        grid=(tokens // tm,),
        in_specs=[
            row_spec,
            _const_spec((1, d_model)),
            _const_spec(wgu.shape),
            _const_spec(wd.shape),
            _const_spec((1, d_model)),
            _const_spec((1, d_model)),
            _const_spec(w_in.shape),
            _const_spec(b_in.shape),
        ],
        out_specs=[
            row_spec,
            head_spec(2), head_spec(2), head_spec(2), head_spec(2),
            pl.BlockSpec((None, tm, LANE_BLOCK),
                         lambda i: (i // tiles_per_seq, i % tiles_per_seq, 0)),
        ],
        out_shape=[
            jax.ShapeDtypeStruct((tokens, d_model), F32),
            head_shape(2), head_shape(2), head_shape(2), head_shape(2),
            jax.ShapeDtypeStruct((batch, seq, LANE_BLOCK), BF16),
        ],
        scratch_shapes=[pltpu.VMEM((tm, d_model), BF16), pltpu.VMEM((tm, d_ff), BF16)],
        compiler_params=pltpu.CompilerParams(dimension_semantics=("arbitrary",),
                                             vmem_limit_bytes=VMEM_LIMIT_TOKEN_STAGES),
        name="ffn1_inproj",
    )(x2, g_pre, wgu, wd, g_post, g_attn, w_in, b_in)


def _lane_is_low(shape):
    return lax.broadcasted_iota(jnp.int32, shape, 1) < HEAD_DIM


def _softmax_pv(q, k_masked, v_masked, bias):
    s = lax.dot_general(q, k_masked, (((1,), (1,)), ((), ())), preferred_element_type=F32) + bias
    m = jnp.max(s, axis=-1, keepdims=True)
    p = jnp.exp(s - m)
    den = jnp.sum(p, axis=-1, keepdims=True)
    o = jnp.dot(p.astype(BF16), v_masked, preferred_element_type=F32)
    return o, m, den


def _attn_b_kernel(q_ref, kp_ref, kc_ref, vp_ref, vc_ref, bias_ref, o_ref, lse_ref):
    first_chunk = pl.program_id(2) == 0
    rows = q_ref.shape[1]
    pair = 2 * HEAD_DIM
    low_kv = _lane_is_low((rows + ATTN_BLOCK, pair))
    low_out = _lane_is_low((ATTN_BLOCK, pair))
    for jb in range(q_ref.shape[0]):
        k_all = jnp.concatenate([kp_ref[jb], kc_ref[jb]], axis=0)
        v_all = jnp.concatenate([vp_ref[jb], vc_ref[jb]], axis=0)
        for grp in range(LANE_BLOCK // pair):
            lanes = slice(grp * pair, (grp + 1) * pair)
            k_grp = k_all[:, lanes]
            v_grp = v_all[:, lanes]
            zero = jnp.zeros_like(k_grp)
            k_par = (jnp.where(low_kv, k_grp, zero), jnp.where(low_kv, zero, k_grp))
            v_par = (jnp.where(low_kv, v_grp, zero), jnp.where(low_kv, zero, v_grp))
            for t in range(rows // ATTN_BLOCK):
                q = q_ref[jb, t * ATTN_BLOCK:(t + 1) * ATTN_BLOCK, lanes]
                keys = slice(t * ATTN_BLOCK, (t + 2) * ATTN_BLOCK)
                outs, lses = [], []
                for par in range(2):
                    head = jb * HEADS_PER_LANE_BLOCK + grp * 2 + par
                    if t == 0:
                        bias = bias_ref[head * 2 + jnp.where(first_chunk, 0, 1)]
                    else:
                        bias = bias_ref[head * 2 + 1]
                    o, m, den = _softmax_pv(q, k_par[par][keys], v_par[par][keys], bias)
                    outs.append(o / den)
                    lses.append(m + jnp.log(den))
                out_lanes = slice(jb * LANE_BLOCK + grp * pair, jb * LANE_BLOCK + (grp + 1) * pair)
                out_rows = slice(t * ATTN_BLOCK, (t + 1) * ATTN_BLOCK)
                o_ref[out_rows, out_lanes] = (outs[0] + outs[1]).astype(BF16)
                lse_ref[out_rows, out_lanes] = jnp.where(low_out, lses[0], lses[1])


def _attn_b(qb, kb, vb, bias, dilation):
    batch, n_blk, seq, _ = qb.shape
    class_len = seq // dilation
    rows = min(ATTN_ROWS, class_len)
    chunks = class_len // rows
    prev_per_chunk = rows // ATTN_BLOCK
    view = (batch, n_blk, class_len, dilation * LANE_BLOCK)
    qb, kb, vb = qb.reshape(view), kb.reshape(view), vb.reshape(view)

    cur_spec = pl.BlockSpec((None, n_blk, rows, LANE_BLOCK), lambda b, c, i: (b, 0, i, c))
    prev_spec = pl.BlockSpec(
        (None, n_blk, ATTN_BLOCK, LANE_BLOCK),
        lambda b, c, i: (b, 0, jnp.maximum(i * prev_per_chunk - 1, 0), c))
    width = n_blk * LANE_BLOCK
    out_spec = pl.BlockSpec((None, rows, width), lambda b, c, i: (b, i, c))
    o, lse = pl.pallas_call(
        _attn_b_kernel,
        grid=(batch, dilation, chunks),
        in_specs=[cur_spec, prev_spec, cur_spec, prev_spec, cur_spec,
                  _const_spec(bias.shape)],
        out_specs=[out_spec, out_spec],
        out_shape=[jax.ShapeDtypeStruct((batch, class_len, dilation * width), BF16),
                   jax.ShapeDtypeStruct((batch, class_len, dilation * width), F32)],
        compiler_params=pltpu.CompilerParams(
            dimension_semantics=("arbitrary", "arbitrary", "arbitrary"),
            vmem_limit_bytes=VMEM_LIMIT_ATTENTION),
        name=f"mixer_b_dil{dilation}",
    )(qb, kb, kb, vb, vb, bias)
    return o.reshape(batch * seq, width), lse.reshape(batch * seq, width)


def _swap_halves(x):
    return jnp.concatenate([x[:, HEAD_DIM:], x[:, :HEAD_DIM]], axis=1)


def _attn_a_kernel(sink_ref, q_ref, kvp_ref, kvc_ref, bias_ref, o_ref):
    first_chunk = pl.program_id(1) == 0
    rows = q_ref.shape[1]
    pair = 2 * HEAD_DIM
    low_kv = _lane_is_low((rows + ATTN_BLOCK, pair))
    kv_all = jnp.concatenate([kvp_ref[...], kvc_ref[...]], axis=0)
    k_nat, v_nat = kv_all[:, :pair], kv_all[:, pair:]
    k_swp, v_swp = _swap_halves(k_nat), _swap_halves(v_nat)
    zero = jnp.zeros_like(k_nat)
    for g in range(A_KV_HEADS):
        k_low_src, k_high_src = (k_nat, k_swp) if g == 0 else (k_swp, k_nat)
        v_low_src, v_high_src = (v_nat, v_swp) if g == 0 else (v_swp, v_nat)
        k_par = (jnp.where(low_kv, k_low_src, zero), jnp.where(low_kv, zero, k_high_src))
        v_par = (jnp.where(low_kv, v_low_src, zero), jnp.where(low_kv, zero, v_high_src))
        for grp in range(LANE_BLOCK // pair):
            lanes = slice(grp * pair, (grp + 1) * pair)
            for t in range(rows // ATTN_BLOCK):
                q = q_ref[g, t * ATTN_BLOCK:(t + 1) * ATTN_BLOCK, lanes]
                keys = slice(t * ATTN_BLOCK, (t + 2) * ATTN_BLOCK)
                outs = []
                for par in range(2):
                    head = g * HEADS_PER_LANE_BLOCK + grp * 2 + par
                    if t == 0:
                        bias = bias_ref[head * 2 + jnp.where(first_chunk, 0, 1)]
                    else:
                        bias = bias_ref[head * 2 + 1]
                    o, m, den = _softmax_pv(q, k_par[par][keys], v_par[par][keys], bias)
                    sink = sink_ref[head]
                    m_all = jnp.maximum(m, sink)
                    scale = jnp.exp(m - m_all)
                    total = den * scale + jnp.exp(sink - m_all)
                    outs.append(o * (scale / total))
                out_lanes = slice(g * LANE_BLOCK + grp * pair, g * LANE_BLOCK + (grp + 1) * pair)
                o_ref[t * ATTN_BLOCK:(t + 1) * ATTN_BLOCK, out_lanes] = (
                    outs[0] + outs[1]).astype(BF16)


def _attn_a(qa, kva, bias, sinks):
    batch, n_blk, seq, _ = qa.shape
    rows = min(ATTN_ROWS, seq)
    prev_per_chunk = rows // ATTN_BLOCK
    width = n_blk * LANE_BLOCK
    out = pl.pallas_call(
        _attn_a_kernel,
        grid=(batch, seq // rows),
        in_specs=[
            pl.BlockSpec(memory_space=pltpu.SMEM),
            pl.BlockSpec((None, n_blk, rows, LANE_BLOCK), lambda b, i: (b, 0, i, 0)),
            pl.BlockSpec((None, ATTN_BLOCK, LANE_BLOCK),
                         lambda b, i: (b, jnp.maximum(i * prev_per_chunk - 1, 0), 0)),
            pl.BlockSpec((None, rows, LANE_BLOCK), lambda b, i: (b, i, 0)),
            _const_spec(bias.shape),
        ],
        out_specs=pl.BlockSpec((None, rows, width), lambda b, i: (b, i, 0)),
        out_shape=jax.ShapeDtypeStruct((batch, seq, width), BF16),
        compiler_params=pltpu.CompilerParams(
            dimension_semantics=("arbitrary", "arbitrary"),
            vmem_limit_bytes=VMEM_LIMIT_ATTENTION),
        name="mixer_a",
    )(sinks, qa, kva, kva, bias)
    return out.reshape(batch * seq, width)


def _stage3_kernel(h_ref, mixa_ref, o1_ref, o2_ref, o3_ref, l1_ref, l2_ref, l3_ref, p_ref,
                   wout_ref, bout_ref, g_attn_post_ref, g_pre_ref, wgu_ref, wd_ref, g_post_ref,
                   g_ple_pre_ref, wgate_ref, wproj_ref, g_ple_post_ref,
                   out_ref, xn_ref, act_ref):
    l1, l2, l3 = l1_ref[...], l2_ref[...], l3_ref[...]
    l_max = jnp.maximum(jnp.maximum(l1, l2), l3)
    w1, w2, w3 = jnp.exp(l1 - l_max), jnp.exp(l2 - l_max), jnp.exp(l3 - l_max)
    num = (w1 * o1_ref[...].astype(F32) + w2 * o2_ref[...].astype(F32)
           + w3 * o3_ref[...].astype(F32))
    mix_b = (num / (w1 + w2 + w3)).astype(BF16)

    d_a = mixa_ref.shape[1]
    att = (jnp.dot(mixa_ref[...], wout_ref[:d_a, :], preferred_element_type=F32)
           + jnp.dot(mix_b, wout_ref[d_a:, :], preferred_element_type=F32)
           + bout_ref[...])
    h = h_ref[...] + _rms(att, g_attn_post_ref[...])
    out_ref[...] = h

    xn_ref[...] = _rms(h, g_pre_ref[...]).astype(BF16)
    f = _swiglu(xn_ref, wgu_ref, wd_ref, act_ref)
    h = out_ref[...] + 0.5 * _rms(f, g_post_ref[...])
    out_ref[...] = h

    xn_ref[...] = _rms(h, g_ple_pre_ref[...]).astype(BF16)
    gate = _sigmoid(jnp.dot(xn_ref[...], wgate_ref[...], preferred_element_type=F32))
    e = jnp.dot(p_ref[...].astype(BF16), wproj_ref[...], preferred_element_type=F32)
    out_ref[...] = out_ref[...] + _rms(gate * e, g_ple_post_ref[...])


def _stage3(h1, mix_a, outs_b, lses_b, p2, w_out, b_out, g_attn_post, g_pre, wgu, wd, g_post,
            g_ple_pre, w_gate, w_proj, g_ple_post):
    tokens, d_model = h1.shape
    tm = TOKEN_TILE
    d_ff = wd.shape[0]

    def rows(width):
        return pl.BlockSpec((tm, width), lambda i: (i, 0))

    half = mix_a.shape[1]
    vec = _const_spec((1, d_model))
    return pl.pallas_call(
        _stage3_kernel,
        grid=(tokens // tm,),
        in_specs=[
            rows(d_model), rows(half),
            rows(half), rows(half), rows(half),
            rows(half), rows(half), rows(half),
            rows(p2.shape[1]),
            _const_spec(w_out.shape), vec, vec, vec,
            _const_spec(wgu.shape), _const_spec(wd.shape), vec, vec,
            _const_spec(w_gate.shape), _const_spec(w_proj.shape), vec,
        ],
        out_specs=rows(d_model),
        out_shape=jax.ShapeDtypeStruct((tokens, d_model), F32),
        scratch_shapes=[pltpu.VMEM((tm, d_model), BF16), pltpu.VMEM((tm, d_ff), BF16)],
        compiler_params=pltpu.CompilerParams(dimension_semantics=("arbitrary",),
                                             vmem_limit_bytes=VMEM_LIMIT_TOKEN_STAGES),
        name="outproj_ffn2_ple",
    )(h1, mix_a, *outs_b, *lses_b, p2, w_out, b_out, g_attn_post, g_pre, wgu, wd, g_post,
      g_ple_pre, w_gate, w_proj, g_ple_post)


def kernel(x, p, rel_bias, ffn1_pre_g, ffn1_w_gu, ffn1_w_down, ffn1_post_g, attn_pre_g, w_in, b_in,
           sinks, w_out, b_out, attn_post_g, ffn2_pre_g, ffn2_w_gu, ffn2_w_down, ffn2_post_g,
           ple_pre_g, w_ple_gate, w_ple_proj, ple_post_g):
    batch, seq, d_model = x.shape
    depth = p.shape[0]
    max_dilation = max(d for _, d in B_PATTERNS)
    assert seq % (max_dilation * ATTN_BLOCK) == 0 and seq % TOKEN_TILE == 0
    assert x.dtype == F32

    bias = _bias_tables(rel_bias)
    h = x.reshape(batch * seq, d_model)
    for i in range(depth):
        row = lambda v: v[i].reshape(1, -1)
        h1, qb, kb, vb, qa, kva = _stage1(
            h, row(ffn1_pre_g), ffn1_w_gu[i].astype(BF16), ffn1_w_down[i].astype(BF16),
            row(ffn1_post_g), row(attn_pre_g), w_in[i].astype(BF16), row(b_in), batch, seq)
        mix_a = _attn_a(qa, kva, bias[0], sinks[i])
        outs_b, lses_b = [], []
        for pat, (_, dilation) in enumerate(B_PATTERNS):
            o, lse = _attn_b(qb, kb, vb, bias[1 + pat], dilation)
            outs_b.append(o)
            lses_b.append(lse)
        h = _stage3(
            h1, mix_a, outs_b, lses_b, p[i].reshape(batch * seq, -1),
            w_out[i].astype(BF16), row(b_out), row(attn_post_g), row(ffn2_pre_g),
            ffn2_w_gu[i].astype(BF16), ffn2_w_down[i].astype(BF16), row(ffn2_post_g),
            row(ple_pre_g), w_ple_gate[i].astype(BF16), w_ple_proj[i].astype(BF16),
            row(ple_post_g))
    return h.reshape(batch, seq, d_model)
```

```python
import math

import numpy as np
import jax
import jax.numpy as jnp
from jax import lax
from jax.experimental import pallas as pl
from jax.experimental.pallas import tpu as pltpu

HEAD_DIM = 64
A_Q_HEADS = 8
A_KV_HEADS = 2
A_WINDOW = 128
B_HEADS = 8
B_PATTERNS = ((128, 1), (512, 4), (2048, 16))
DILATIONS = tuple(d for _, d in B_PATTERNS)
NUM_BUCKETS = 32
MAX_DISTANCE = 2048
EPS = 1e-6
NEG_INF = -1e30

ATTN_BLOCK = 128
LANE_BLOCK = 256
HEADS_PER_LANE_BLOCK = LANE_BLOCK // HEAD_DIM
SLAB = 128
SLABS_PER_LANE_BLOCK = LANE_BLOCK // SLAB
FF_CHUNK = 256
TOKEN_TILE = 512
ATTN_ROWS = 512
VMEM_LIMIT_TOKEN_STAGES = 58 * 1024 * 1024
VMEM_LIMIT_ATTENTION = 40 * 1024 * 1024

F32 = jnp.float32
BF16 = jnp.bfloat16


def _rms(x, g):
    return x * lax.rsqrt(jnp.mean(x * x, axis=-1, keepdims=True) + EPS) * g


def _sigmoid(x):
    return 1.0 / (1.0 + jnp.exp(-x))


def _swiglu(xn_ref, wgu_ref, wd_ref, act_ref):
    d_ff = wd_ref.shape[0]
    for c in range(d_ff // FF_CHUNK):
        lo = c * FF_CHUNK
        g = jnp.dot(xn_ref[...], wgu_ref[:, lo:lo + FF_CHUNK], preferred_element_type=F32)
        u = jnp.dot(xn_ref[...], wgu_ref[:, d_ff + lo:d_ff + lo + FF_CHUNK],
                    preferred_element_type=F32)
        act_ref[:, lo:lo + FF_CHUNK] = ((g * _sigmoid(g)) * u).astype(BF16)
    return jnp.dot(act_ref[...], wd_ref[...], preferred_element_type=F32)


def _const_spec(shape):
    return pl.BlockSpec(shape, lambda *_: (0,) * len(shape), pipeline_mode=pl.Buffered(1))


def _t5_bucket_np(dist):
    max_exact = NUM_BUCKETS // 2
    n = np.maximum(dist, 0)
    nf = np.maximum(n, 1).astype(np.float32)
    large = max_exact + (np.log(nf / np.float32(max_exact))
                         / np.float32(math.log(MAX_DISTANCE / max_exact))
                         * np.float32(NUM_BUCKETS - max_exact)).astype(np.int32)
    large = np.minimum(large, NUM_BUCKETS - 1)
    return np.where(n < max_exact, n, large).astype(np.int32)


def _bucket_tables():
    q = np.arange(ATTN_BLOCK)[:, None]
    k = np.arange(2 * ATTN_BLOCK)[None, :]
    rel = ATTN_BLOCK + q - k
    tables = []
    for max_dist, stride in ((A_WINDOW - 1, 1),) + tuple((w // d, d) for w, d in B_PATTERNS):
        valid = (rel >= 0) & (rel <= max_dist)
        bkt = _t5_bucket_np(rel * stride)
        later = np.where(valid, bkt, -1)
        first = np.where(valid & (k >= ATTN_BLOCK), bkt, -1)
        tables.append(np.stack([first, later]))
    return np.stack(tables).astype(np.int32)


def _bias_kernel(rb_ref, bkt_ref, out_ref):
    pattern = pl.program_id(0)
    head = pl.program_id(1)
    col = jnp.where(pattern == 0, head, head + A_Q_HEADS)
    bkt = bkt_ref[...]
    acc = jnp.full(bkt.shape, NEG_INF, F32)
    for b in range(NUM_BUCKETS):
        acc = jnp.where(bkt == b, rb_ref[b, col], acc)
    out_ref[...] = acc


def _bias_tables(rel_bias):
    bkt = jnp.asarray(_bucket_tables())
    n_pat = bkt.shape[0]
    out = pl.pallas_call(
        _bias_kernel,
        grid=(n_pat, A_Q_HEADS),
        in_specs=[
            pl.BlockSpec(memory_space=pltpu.SMEM),
            pl.BlockSpec((None, 2, ATTN_BLOCK, 2 * ATTN_BLOCK), lambda p, h: (p, 0, 0, 0)),
        ],
        out_specs=pl.BlockSpec((None, None, 2, ATTN_BLOCK, 2 * ATTN_BLOCK),
                               lambda p, h: (p, h, 0, 0, 0)),
        out_shape=jax.ShapeDtypeStruct((n_pat, A_Q_HEADS, 2, ATTN_BLOCK, 2 * ATTN_BLOCK), F32),
        compiler_params=pltpu.CompilerParams(dimension_semantics=("arbitrary", "arbitrary")),
        name="bias_tables",
    )(rel_bias, bkt)
    return out.reshape(n_pat, A_Q_HEADS * 2, ATTN_BLOCK, 2 * ATTN_BLOCK)


def _stage1_kernel(x_ref, g_pre_ref, wgu_ref, wd_ref, g_post_ref, g_attn_ref, win_ref, bin_ref,
                   h_ref, qa_ref, kva_ref, *rest):
    n_dil = len(DILATIONS)
    qb_refs, kb_refs, vb_refs = rest[:n_dil], rest[n_dil:2 * n_dil], rest[2 * n_dil:3 * n_dil]
    xn_ref, act_ref, slab_ref = rest[3 * n_dil:]
    tm = x_ref.shape[0]

    xn_ref[...] = _rms(x_ref[...], g_pre_ref[...]).astype(BF16)
    f = _swiglu(xn_ref, wgu_ref, wd_ref, act_ref)
    h = x_ref[...] + 0.5 * _rms(f, g_post_ref[...])
    h_ref[...] = h
    xn_ref[...] = _rms(h, g_attn_ref[...]).astype(BF16)

    def proj(col_block):
        lo = col_block * LANE_BLOCK
        return (jnp.dot(xn_ref[...], win_ref[:, lo:lo + LANE_BLOCK], preferred_element_type=F32)
                + bin_ref[:, lo:lo + LANE_BLOCK])

    def write_classes(z, refs, jb, slot):
        base = slot * SLABS_PER_LANE_BLOCK
        for s in range(SLABS_PER_LANE_BLOCK):
            slab_ref[base + s] = z[:, s * SLAB:(s + 1) * SLAB]
        for ref, dil in zip(refs, DILATIONS):
            if dil == 1:
                ref[jb, 0] = z.astype(BF16)
                continue
            n = tm // dil
            for c in range(dil):
                piece = jnp.concatenate(
                    [slab_ref[base + s, pl.ds(c, n, stride=dil), :]
                     for s in range(SLABS_PER_LANE_BLOCK)], axis=1)
                ref[jb, c] = piece.astype(BF16)

    q_scale = HEAD_DIM ** -0.5
    for j in range(2):
        qa_ref[j] = (proj(j) * q_scale).astype(BF16)
        write_classes(proj(3 + j) * q_scale, qb_refs, j, 3 * j)
        write_classes(proj(5 + j), kb_refs, j, 3 * j + 1)
        write_classes(proj(7 + j), vb_refs, j, 3 * j + 2)
    kva_ref[...] = proj(2).astype(BF16)


def _stage1(x2, g_pre, wgu, wd, g_post, g_attn, w_in, b_in, batch, seq):
    tokens, d_model = x2.shape
    tm = TOKEN_TILE
    tiles_per_seq = seq // tm
    d_ff = wd.shape[0]
    n_blk = 2
    n_dil = len(DILATIONS)

    def class_spec(dil):
        return pl.BlockSpec((None, n_blk, dil, tm // dil, LANE_BLOCK),
                            lambda i: (i // tiles_per_seq, 0, 0, i % tiles_per_seq, 0))

    def class_shape(dil):
        return jax.ShapeDtypeStruct((batch, n_blk, dil, seq // dil, LANE_BLOCK), BF16)

    row_spec = pl.BlockSpec((tm, d_model), lambda i: (i, 0))
    outs = pl.pallas_call(
        _stage1_kernel,
        grid=(tokens // tm,),
        in_specs=[
            row_spec,
            _const_spec((1, d_model)),
            _const_spec(wgu.shape),
            _const_spec(wd.shape),
            _const_spec((1, d_model)),
            _const_spec((1, d_model)),
            _const_spec(w_in.shape),
            _const_spec(b_in.shape),
        ],
        out_specs=[
            row_spec,
            pl.BlockSpec((None, n_blk, tm, LANE_BLOCK),
                         lambda i: (i // tiles_per_seq, 0, i % tiles_per_seq, 0)),
            pl.BlockSpec((None, tm, LANE_BLOCK),
                         lambda i: (i // tiles_per_seq, i % tiles_per_seq, 0)),
        ] + [class_spec(d) for _ in range(3) for d in DILATIONS],
        out_shape=[
            jax.ShapeDtypeStruct((tokens, d_model), F32),
            jax.ShapeDtypeStruct((batch, n_blk, seq, LANE_BLOCK), BF16),
            jax.ShapeDtypeStruct((batch, seq, LANE_BLOCK), BF16),
        ] + [class_shape(d) for _ in range(3) for d in DILATIONS],
        scratch_shapes=[
            pltpu.VMEM((tm, d_model), BF16),
            pltpu.VMEM((tm, d_ff), BF16),
            pltpu.VMEM((3 * n_blk * SLABS_PER_LANE_BLOCK, tm, SLAB), F32),
        ],
        compiler_params=pltpu.CompilerParams(dimension_semantics=("arbitrary",),
                                             vmem_limit_bytes=VMEM_LIMIT_TOKEN_STAGES),
        name="ffn1_inproj",
    )(x2, g_pre, wgu, wd, g_post, g_attn, w_in, b_in)
    h1, qa, kva = outs[:3]
    qb, kb, vb = outs[3:3 + n_dil], outs[3 + n_dil:3 + 2 * n_dil], outs[3 + 2 * n_dil:]
    return h1, qa, kva, qb, kb, vb


def _lane_is_low(shape):
    return lax.broadcasted_iota(jnp.int32, shape, 1) < HEAD_DIM


def _softmax_pv(q, k_masked, v_masked, bias):
    s = lax.dot_general(q, k_masked, (((1,), (1,)), ((), ())), preferred_element_type=F32) + bias
    m = jnp.max(s, axis=-1, keepdims=True)
    p = jnp.exp(s - m)
    den = jnp.sum(p, axis=-1, keepdims=True)
    o = jnp.dot(p.astype(BF16), v_masked, preferred_element_type=F32)
    return o, m, den


def _attn_b_kernel(q_ref, kp_ref, kc_ref, vp_ref, vc_ref, bias_ref, o_ref, lse_ref):
    first_chunk = pl.program_id(2) == 0
    n_blk, n_cls, rows, _ = q_ref.shape
    pair = 2 * HEAD_DIM
    low_kv = _lane_is_low((rows + ATTN_BLOCK, pair))
    low_out = _lane_is_low((ATTN_BLOCK, pair))
    for cls in range(n_cls):
        for jb in range(n_blk):
            k_all = jnp.concatenate([kp_ref[jb, cls], kc_ref[jb, cls]], axis=0)
            v_all = jnp.concatenate([vp_ref[jb, cls], vc_ref[jb, cls]], axis=0)
            for grp in range(LANE_BLOCK // pair):
                lanes = slice(grp * pair, (grp + 1) * pair)
                k_grp = k_all[:, lanes]
                v_grp = v_all[:, lanes]
                zero = jnp.zeros_like(k_grp)
                k_par = (jnp.where(low_kv, k_grp, zero), jnp.where(low_kv, zero, k_grp))
                v_par = (jnp.where(low_kv, v_grp, zero), jnp.where(low_kv, zero, v_grp))
                for t in range(rows // ATTN_BLOCK):
                    q = q_ref[jb, cls, t * ATTN_BLOCK:(t + 1) * ATTN_BLOCK, lanes]
                    keys = slice(t * ATTN_BLOCK, (t + 2) * ATTN_BLOCK)
                    outs, lses = [], []
                    for par in range(2):
                        head = jb * HEADS_PER_LANE_BLOCK + grp * 2 + par
                        if t == 0:
                            bias = bias_ref[head * 2 + jnp.where(first_chunk, 0, 1)]
                        else:
                            bias = bias_ref[head * 2 + 1]
                        o, m, den = _softmax_pv(q, k_par[par][keys], v_par[par][keys], bias)
                        outs.append(o / den)
                        lses.append(m + jnp.log(den))
                    out_lanes = slice(jb * LANE_BLOCK + grp * pair,
                                      jb * LANE_BLOCK + (grp + 1) * pair)
                    out_rows = slice(t * ATTN_BLOCK, (t + 1) * ATTN_BLOCK)
                    o_ref[cls, out_rows, out_lanes] = (outs[0] + outs[1]).astype(BF16)
                    lse_ref[cls, out_rows, out_lanes] = jnp.where(low_out, lses[0], lses[1])


def _attn_b(qb, kb, vb, bias):
    batch, n_blk, n_cls, class_len, _ = qb.shape
    rows = min(ATTN_ROWS, class_len)
    cls_per_step = min(n_cls, ATTN_ROWS // rows)
    prev_per_chunk = rows // ATTN_BLOCK

    cur_spec = pl.BlockSpec((None, n_blk, cls_per_step, rows, LANE_BLOCK),
                            lambda b, c, i: (b, 0, c, i, 0))
    prev_spec = pl.BlockSpec(
        (None, n_blk, cls_per_step, ATTN_BLOCK, LANE_BLOCK),
        lambda b, c, i: (b, 0, c, jnp.maximum(i * prev_per_chunk - 1, 0), 0))
    width = n_blk * LANE_BLOCK
    out_spec = pl.BlockSpec((None, cls_per_step, rows, width), lambda b, c, i: (b, c, i, 0))
    return pl.pallas_call(
        _attn_b_kernel,
        grid=(batch, n_cls // cls_per_step, class_len // rows),
        in_specs=[cur_spec, prev_spec, cur_spec, prev_spec, cur_spec,
                  _const_spec(bias.shape)],
        out_specs=[out_spec, out_spec],
        out_shape=[jax.ShapeDtypeStruct((batch, n_cls, class_len, width), BF16),
                   jax.ShapeDtypeStruct((batch, n_cls, class_len, width), F32)],
        compiler_params=pltpu.CompilerParams(
            dimension_semantics=("arbitrary", "arbitrary", "arbitrary"),
            vmem_limit_bytes=VMEM_LIMIT_ATTENTION),
        name=f"mixer_b_dil{n_cls}",
    )(qb, kb, kb, vb, vb, bias)


def _swap_halves(x):
    return jnp.concatenate([x[:, HEAD_DIM:], x[:, :HEAD_DIM]], axis=1)


def _attn_a_kernel(sink_ref, q_ref, kvp_ref, kvc_ref, bias_ref, o_ref):
    first_chunk = pl.program_id(1) == 0
    rows = q_ref.shape[1]
    pair = 2 * HEAD_DIM
    low_kv = _lane_is_low((rows + ATTN_BLOCK, pair))
    kv_all = jnp.concatenate([kvp_ref[...], kvc_ref[...]], axis=0)
    k_nat, v_nat = kv_all[:, :pair], kv_all[:, pair:]
    k_swp, v_swp = _swap_halves(k_nat), _swap_halves(v_nat)
    zero = jnp.zeros_like(k_nat)
    for g in range(A_KV_HEADS):
        k_low_src, k_high_src = (k_nat, k_swp) if g == 0 else (k_swp, k_nat)
        v_low_src, v_high_src = (v_nat, v_swp) if g == 0 else (v_swp, v_nat)
        k_par = (jnp.where(low_kv, k_low_src, zero), jnp.where(low_kv, zero, k_high_src))
        v_par = (jnp.where(low_kv, v_low_src, zero), jnp.where(low_kv, zero, v_high_src))
        for grp in range(LANE_BLOCK // pair):
            lanes = slice(grp * pair, (grp + 1) * pair)
            for t in range(rows // ATTN_BLOCK):
                q = q_ref[g, t * ATTN_BLOCK:(t + 1) * ATTN_BLOCK, lanes]
                keys = slice(t * ATTN_BLOCK, (t + 2) * ATTN_BLOCK)
                outs = []
                for par in range(2):
                    head = g * HEADS_PER_LANE_BLOCK + grp * 2 + par
                    if t == 0:
                        bias = bias_ref[head * 2 + jnp.where(first_chunk, 0, 1)]
                    else:
                        bias = bias_ref[head * 2 + 1]
                    o, m, den = _softmax_pv(q, k_par[par][keys], v_par[par][keys], bias)
                    sink = sink_ref[head]
                    m_all = jnp.maximum(m, sink)
                    scale = jnp.exp(m - m_all)
                    total = den * scale + jnp.exp(sink - m_all)
                    outs.append(o * (scale / total))
                out_lanes = slice(g * LANE_BLOCK + grp * pair, g * LANE_BLOCK + (grp + 1) * pair)
                o_ref[t * ATTN_BLOCK:(t + 1) * ATTN_BLOCK, out_lanes] = (
                    outs[0] + outs[1]).astype(BF16)


def _attn_a(qa, kva, bias, sinks):
    batch, n_blk, seq, _ = qa.shape
    rows = min(ATTN_ROWS, seq)
    prev_per_chunk = rows // ATTN_BLOCK
    width = n_blk * LANE_BLOCK
    out = pl.pallas_call(
        _attn_a_kernel,
        grid=(batch, seq // rows),
        in_specs=[
            pl.BlockSpec(memory_space=pltpu.SMEM),
            pl.BlockSpec((None, n_blk, rows, LANE_BLOCK), lambda b, i: (b, 0, i, 0)),
            pl.BlockSpec((None, ATTN_BLOCK, LANE_BLOCK),
                         lambda b, i: (b, jnp.maximum(i * prev_per_chunk - 1, 0), 0)),
            pl.BlockSpec((None, rows, LANE_BLOCK), lambda b, i: (b, i, 0)),
            _const_spec(bias.shape),
        ],
        out_specs=pl.BlockSpec((None, rows, width), lambda b, i: (b, i, 0)),
        out_shape=jax.ShapeDtypeStruct((batch, seq, width), BF16),
        compiler_params=pltpu.CompilerParams(
            dimension_semantics=("arbitrary", "arbitrary"),
            vmem_limit_bytes=VMEM_LIMIT_ATTENTION),
        name="mixer_a",
    )(sinks, qa, kva, kva, bias)
    return out.reshape(batch * seq, width)


def _stage3_kernel(*refs):
    n_dil = len(DILATIONS)
    h_ref, mixa_ref = refs[:2]
    o_refs = refs[2:2 + n_dil]
    l_refs = refs[2 + n_dil:2 + 2 * n_dil]
    (p_ref, wout_ref, bout_ref, g_attn_post_ref, g_pre_ref, wgu_ref, wd_ref, g_post_ref,
     g_ple_pre_ref, wgate_ref, wproj_ref, g_ple_post_ref,
     out_ref, xn_ref, act_ref, mixb_ref, slab_ref) = refs[2 + 2 * n_dil:]
    tm = h_ref.shape[0]
    n_slabs = mixa_ref.shape[1] // SLAB

    for idx, dil in enumerate(DILATIONS):
        if dil == 1:
            continue
        n = tm // dil
        for src_ref, which in ((o_refs[idx], 0), (l_refs[idx], 1)):
            base = (2 * idx + which) * n_slabs
            for c in range(dil):
                blk = src_ref[c].astype(F32)
                for s in range(n_slabs):
                    slab_ref[base + s, pl.ds(c, n, stride=dil), :] = blk[:, s * SLAB:(s + 1) * SLAB]

    for s in range(n_slabs):
        lanes = slice(s * SLAB, (s + 1) * SLAB)
        outs, lses = [], []
        for idx, dil in enumerate(DILATIONS):
            if dil == 1:
                outs.append(o_refs[idx][0, :, lanes].astype(F32))
                lses.append(l_refs[idx][0, :, lanes])
            else:
                outs.append(slab_ref[2 * idx * n_slabs + s])
                lses.append(slab_ref[(2 * idx + 1) * n_slabs + s])
        l_max = lses[0]
        for l in lses[1:]:
            l_max = jnp.maximum(l_max, l)
        weights = [jnp.exp(l - l_max) for l in lses]
        num = weights[0] * outs[0]
        den = weights[0]
        for w, o in zip(weights[1:], outs[1:]):
            num = num + w * o
            den = den + w
        mixb_ref[:, lanes] = (num / den).astype(BF16)

    d_a = mixa_ref.shape[1]
    att = (jnp.dot(mixa_ref[...], wout_ref[:d_a, :], preferred_element_type=F32)
           + jnp.dot(mixb_ref[...], wout_ref[d_a:, :], preferred_element_type=F32)
           + bout_ref[...])
    h = h_ref[...] + _rms(att, g_attn_post_ref[...])
    out_ref[...] = h

    xn_ref[...] = _rms(h, g_pre_ref[...]).astype(BF16)
    f = _swiglu(xn_ref, wgu_ref, wd_ref, act_ref)
    h = out_ref[...] + 0.5 * _rms(f, g_post_ref[...])
    out_ref[...] = h

    xn_ref[...] = _rms(h, g_ple_pre_ref[...]).astype(BF16)
    gate = _sigmoid(jnp.dot(xn_ref[...], wgate_ref[...], preferred_element_type=F32))
    e = jnp.dot(p_ref[...].astype(BF16), wproj_ref[...], preferred_element_type=F32)
    out_ref[...] = out_ref[...] + _rms(gate * e, g_ple_post_ref[...])


def _stage3(h1, mix_a, outs_b, lses_b, p2, w_out, b_out, g_attn_post, g_pre, wgu, wd, g_post,
            g_ple_pre, w_gate, w_proj, g_ple_post, seq):
    tokens, d_model = h1.shape
    tm = TOKEN_TILE
    tiles_per_seq = seq // tm
    d_ff = wd.shape[0]
    half = mix_a.shape[1]
    n_dil = len(DILATIONS)

    def rows(width):
        return pl.BlockSpec((tm, width), lambda i: (i, 0))

    def class_spec(dil):
        return pl.BlockSpec((None, dil, tm // dil, half),
                            lambda i: (i // tiles_per_seq, 0, i % tiles_per_seq, 0))

    vec = _const_spec((1, d_model))
    return pl.pallas_call(
        _stage3_kernel,
        grid=(tokens // tm,),
        in_specs=[rows(d_model), rows(half)]
        + [class_spec(d) for d in DILATIONS] * 2
        + [
            rows(p2.shape[1]),
            _const_spec(w_out.shape), vec, vec, vec,
            _const_spec(wgu.shape), _const_spec(wd.shape), vec, vec,
            _const_spec(w_gate.shape), _const_spec(w_proj.shape), vec,
        ],
        out_specs=rows(d_model),
        out_shape=jax.ShapeDtypeStruct((tokens, d_model), F32),
        scratch_shapes=[
            pltpu.VMEM((tm, d_model), BF16),
            pltpu.VMEM((tm, d_ff), BF16),
            pltpu.VMEM((tm, half), BF16),
            pltpu.VMEM((2 * n_dil * (half // SLAB), tm, SLAB), F32),
        ],
        compiler_params=pltpu.CompilerParams(dimension_semantics=("arbitrary",),
                                             vmem_limit_bytes=VMEM_LIMIT_TOKEN_STAGES),
        name="outproj_ffn2_ple",
    )(h1, mix_a, *outs_b, *lses_b, p2, w_out, b_out, g_attn_post, g_pre, wgu, wd, g_post,
      g_ple_pre, w_gate, w_proj, g_ple_post)


def kernel(x, p, rel_bias, ffn1_pre_g, ffn1_w_gu, ffn1_w_down, ffn1_post_g, attn_pre_g, w_in, b_in,
           sinks, w_out, b_out, attn_post_g, ffn2_pre_g, ffn2_w_gu, ffn2_w_down, ffn2_post_g,
           ple_pre_g, w_ple_gate, w_ple_proj, ple_post_g):
    batch, seq, d_model = x.shape
    depth = p.shape[0]
    assert seq % (max(DILATIONS) * ATTN_BLOCK) == 0 and seq % TOKEN_TILE == 0
    assert x.dtype == F32

    bias = _bias_tables(rel_bias)
    h = x.reshape(batch * seq, d_model)
    for i in range(depth):
        row = lambda v: v[i].reshape(1, -1)
        h1, qa, kva, qb, kb, vb = _stage1(
            h, row(ffn1_pre_g), ffn1_w_gu[i].astype(BF16), ffn1_w_down[i].astype(BF16),
            row(ffn1_post_g), row(attn_pre_g), w_in[i].astype(BF16), row(b_in), batch, seq)
        mix_a = _attn_a(qa, kva, bias[0], sinks[i])
        outs_b, lses_b = [], []
        for pat in range(len(DILATIONS)):
            o, lse = _attn_b(qb[pat], kb[pat], vb[pat], bias[1 + pat])
            outs_b.append(o)
            lses_b.append(lse)
        h = _stage3(
            h1, mix_a, outs_b, lses_b, p[i].reshape(batch * seq, -1),
            w_out[i].astype(BF16), row(b_out), row(attn_post_g), row(ffn2_pre_g),
            ffn2_w_gu[i].astype(BF16), ffn2_w_down[i].astype(BF16), row(ffn2_post_g),
            row(ple_pre_g), w_ple_gate[i].astype(BF16), w_ple_proj[i].astype(BF16),
            row(ple_post_g), seq)
    return h.reshape(batch, seq, d_model)
```

```python
import functools
import math

import numpy as np
import jax
import jax.numpy as jnp
from jax import lax
from jax.experimental import pallas as pl
from jax.experimental.pallas import tpu as pltpu

HEAD_DIM = 64
A_Q_HEADS = 8
A_KV_HEADS = 2
A_WINDOW = 128
B_HEADS = 8
B_PATTERNS = ((128, 1), (512, 4), (2048, 16))
DILATIONS = tuple(d for _, d in B_PATTERNS)
NUM_BUCKETS = 32
MAX_DISTANCE = 2048
EPS = 1e-6
NEG_INF = -1e30

ATTN_BLOCK = 128
LANE_BLOCK = 256
HEADS_PER_LANE_BLOCK = LANE_BLOCK // HEAD_DIM
SLAB = 128
SLABS_PER_LANE_BLOCK = LANE_BLOCK // SLAB
STAGING_SLOTS = 2
FF_CHUNK = 256
TOKEN_TILE = 512
ROW_CHUNK = 16
ATTN_ROWS = 512
VMEM_LIMIT_TOKEN_STAGES = 58 * 1024 * 1024
VMEM_LIMIT_ATTENTION = 40 * 1024 * 1024

F32 = jnp.float32
BF16 = jnp.bfloat16


def _rms(x, g):
    return x * lax.rsqrt(jnp.mean(x * x, axis=-1, keepdims=True) + EPS) * g


def _row_chunks(n_rows):
    return [slice(r, r + ROW_CHUNK) for r in range(0, n_rows, ROW_CHUNK)]


def _sigmoid(x):
    return 1.0 / (1.0 + jnp.exp(-x))


def _swiglu(xn_ref, wgu_ref, wd_ref, act_ref, side_work=None):
    d_ff = wd_ref.shape[0]
    side_work = dict(side_work or {})
    for c in range(d_ff // FF_CHUNK):
        lo = c * FF_CHUNK
        g = jnp.dot(xn_ref[...], wgu_ref[:, lo:lo + FF_CHUNK], preferred_element_type=F32)
        u = jnp.dot(xn_ref[...], wgu_ref[:, d_ff + lo:d_ff + lo + FF_CHUNK],
                    preferred_element_type=F32)
        act_ref[:, lo:lo + FF_CHUNK] = ((g * _sigmoid(g)) * u).astype(BF16)
        if c in side_work:
            side_work.pop(c)()
    assert not side_work
    return jnp.dot(act_ref[...], wd_ref[...], preferred_element_type=F32)


def _const_spec(shape):
    return pl.BlockSpec(shape, lambda *_: (0,) * len(shape), pipeline_mode=pl.Buffered(1))


def _t5_bucket_np(dist):
    max_exact = NUM_BUCKETS // 2
    n = np.maximum(dist, 0)
    nf = np.maximum(n, 1).astype(np.float32)
    large = max_exact + (np.log(nf / np.float32(max_exact))
                         / np.float32(math.log(MAX_DISTANCE / max_exact))
                         * np.float32(NUM_BUCKETS - max_exact)).astype(np.int32)
    large = np.minimum(large, NUM_BUCKETS - 1)
    return np.where(n < max_exact, n, large).astype(np.int32)


def _bucket_tables():
    q = np.arange(ATTN_BLOCK)[:, None]
    k = np.arange(2 * ATTN_BLOCK)[None, :]
    rel = ATTN_BLOCK + q - k
    tables = []
    for max_dist, stride in ((A_WINDOW - 1, 1),) + tuple((w // d, d) for w, d in B_PATTERNS):
        valid = (rel >= 0) & (rel <= max_dist)
        bkt = _t5_bucket_np(rel * stride)
        later = np.where(valid, bkt, -1)
        first = np.where(valid & (k >= ATTN_BLOCK), bkt, -1)
        tables.append(np.stack([first, later]))
    return np.stack(tables).astype(np.int32)


def _bias_kernel(rb_ref, bkt_ref, out_ref):
    pattern = pl.program_id(0)
    head = pl.program_id(1)
    col = jnp.where(pattern == 0, head, head + A_Q_HEADS)
    bkt = bkt_ref[...]
    acc = jnp.full(bkt.shape, NEG_INF, F32)
    for b in range(NUM_BUCKETS):
        acc = jnp.where(bkt == b, rb_ref[b, col], acc)
    out_ref[...] = acc


def _bias_tables(rel_bias):
    bkt = jnp.asarray(_bucket_tables())
    n_pat = bkt.shape[0]
    out = pl.pallas_call(
        _bias_kernel,
        grid=(n_pat, A_Q_HEADS),
        in_specs=[
            pl.BlockSpec(memory_space=pltpu.SMEM),
            pl.BlockSpec((None, 2, ATTN_BLOCK, 2 * ATTN_BLOCK), lambda p, h: (p, 0, 0, 0)),
        ],
        out_specs=pl.BlockSpec((None, None, 2, ATTN_BLOCK, 2 * ATTN_BLOCK),
                               lambda p, h: (p, h, 0, 0, 0)),
        out_shape=jax.ShapeDtypeStruct((n_pat, A_Q_HEADS, 2, ATTN_BLOCK, 2 * ATTN_BLOCK), F32),
        compiler_params=pltpu.CompilerParams(dimension_semantics=("arbitrary", "arbitrary")),
        name="bias_tables",
    )(rel_bias, bkt)
    return out.reshape(n_pat, A_Q_HEADS * 2, ATTN_BLOCK, 2 * ATTN_BLOCK)


def _stage1_kernel(x_ref, xprev_ref, g_pre_ref, wgu_ref, wd_ref, g_post_ref, g_attn_ref, win_ref,
                   bin_ref, h_ref, qa_ref, kva_ref, *rest):
    n_dil = len(DILATIONS)
    qb_refs, kb_refs, vb_refs = rest[:n_dil], rest[n_dil:2 * n_dil], rest[2 * n_dil:3 * n_dil]
    xn_ref, act_ref, f_ref, xn2_ref, slab_ref, slab2_ref = rest[3 * n_dil:]
    tm = x_ref.shape[0]
    step = pl.program_id(0)
    rd_slot = (step + 1) % 2
    wr_slot = step % 2

    @pl.when(step == 0)
    def _():
        f_ref[1] = jnp.zeros(f_ref.shape[1:], F32)

    for rows in _row_chunks(tm):
        xn_ref[rows] = _rms(x_ref[rows], g_pre_ref[...]).astype(BF16)
    for rows in _row_chunks(tm):
        h = xprev_ref[rows] + 0.5 * _rms(f_ref[rd_slot, rows], g_post_ref[...])
        h_ref[rows] = h
        xn2_ref[rows] = _rms(h, g_attn_ref[...]).astype(BF16)

    def proj(col_block):
        lo = col_block * LANE_BLOCK
        return (jnp.dot(xn2_ref[...], win_ref[:, lo:lo + LANE_BLOCK], preferred_element_type=F32)
                + bin_ref[:, lo:lo + LANE_BLOCK])

    def gather_rows(src_ref, base, start, n, stride):
        return [src_ref[base + s, pl.ds(start, n, stride=stride), :]
                for s in range(SLABS_PER_LANE_BLOCK)]

    def write_classes(z, refs, jb, slot):
        ref1, ref4, ref16 = refs
        base = slot * SLABS_PER_LANE_BLOCK
        ref1[jb, 0] = z.astype(BF16)
        for s in range(SLABS_PER_LANE_BLOCK):
            slab_ref[base + s] = z[:, s * SLAB:(s + 1) * SLAB]
        n4 = tm // 4
        for c4 in range(4):
            pieces = gather_rows(slab_ref, base, c4, n4, 4)
            ref4[jb, c4] = jnp.concatenate(pieces, axis=1).astype(BF16)
            for s in range(SLABS_PER_LANE_BLOCK):
                slab2_ref[base + s, c4 * n4:(c4 + 1) * n4, :] = pieces[s]
        for c4 in range(4):
            for j in range(4):
                pieces = gather_rows(slab2_ref, base, c4 * n4 + j, n4 // 4, 4)
                ref16[jb, c4 + 4 * j] = jnp.concatenate(pieces, axis=1).astype(BF16)

    q_scale = HEAD_DIM ** -0.5
    n_slots = slab_ref.shape[0] // SLABS_PER_LANE_BLOCK

    def store_kva():
        kva_ref[...] = proj(2).astype(BF16)

    def store_qa(j):
        qa_ref[j] = (proj(j) * q_scale).astype(BF16)

    def store_classes(col_block, scale, refs, jb, slot):
        z = proj(col_block)
        write_classes(z if scale == 1.0 else z * scale, refs, jb, slot % n_slots)

    projections = [store_kva]
    for j in range(2):
        projections += [
            functools.partial(store_qa, j),
            functools.partial(store_classes, 3 + j, q_scale, qb_refs, j, 3 * j),
            functools.partial(store_classes, 5 + j, 1.0, kb_refs, j, 3 * j + 1),
            functools.partial(store_classes, 7 + j, 1.0, vb_refs, j, 3 * j + 2),
        ]
    side_work = {1 + c: work for c, work in enumerate(projections)}
    f_ref[wr_slot] = _swiglu(xn_ref, wgu_ref, wd_ref, act_ref, side_work)


def _stage1(x2, g_pre, wgu, wd, g_post, g_attn, w_in, b_in, batch, seq):
    tokens, d_model = x2.shape
    tm = TOKEN_TILE
    tiles_per_seq = seq // tm
    d_ff = wd.shape[0]
    n_blk = 2
    n_dil = len(DILATIONS)
    n_tiles = tokens // tm
    assert DILATIONS == (1, 4, 16) and tm % (16 * 16) == 0

    def lagged(i):
        j = jnp.maximum(i - 1, 0)
        return j // tiles_per_seq, j % tiles_per_seq

    def class_spec(dil):
        return pl.BlockSpec((None, n_blk, dil, tm // dil, LANE_BLOCK),
                            lambda i: (lagged(i)[0], 0, 0, lagged(i)[1], 0))

    def class_shape(dil):
        return jax.ShapeDtypeStruct((batch, n_blk, dil, seq // dil, LANE_BLOCK), BF16)

    row_spec = pl.BlockSpec((tm, d_model), lambda i: (jnp.minimum(i, n_tiles - 1), 0))
    lagged_row_spec = pl.BlockSpec((tm, d_model), lambda i: (jnp.maximum(i - 1, 0), 0))
    outs = pl.pallas_call(
        _stage1_kernel,
        grid=(n_tiles + 1,),
        in_specs=[
            row_spec,
            lagged_row_spec,
            _const_spec((1, d_model)),
            _const_spec(wgu.shape),
            _const_spec(wd.shape),
            _const_spec((1, d_model)),
            _const_spec((1, d_model)),
            _const_spec(w_in.shape),
            _const_spec(b_in.shape),
        ],
        out_specs=[
            lagged_row_spec,
            pl.BlockSpec((None, n_blk, tm, LANE_BLOCK),
                         lambda i: (lagged(i)[0], 0, lagged(i)[1], 0)),
            pl.BlockSpec((None, tm, LANE_BLOCK),
                         lambda i: (lagged(i)[0], lagged(i)[1], 0)),
        ] + [class_spec(d) for _ in range(3) for d in DILATIONS],
        out_shape=[
            jax.ShapeDtypeStruct((tokens, d_model), F32),
            jax.ShapeDtypeStruct((batch, n_blk, seq, LANE_BLOCK), BF16),
            jax.ShapeDtypeStruct((batch, seq, LANE_BLOCK), BF16),
        ] + [class_shape(d) for _ in range(3) for d in DILATIONS],
        scratch_shapes=[
            pltpu.VMEM((tm, d_model), BF16),
            pltpu.VMEM((tm, d_ff), BF16),
            pltpu.VMEM((2, tm, d_model), F32),
            pltpu.VMEM((tm, d_model), BF16),
            pltpu.VMEM((STAGING_SLOTS * SLABS_PER_LANE_BLOCK, tm, SLAB), F32),
            pltpu.VMEM((STAGING_SLOTS * SLABS_PER_LANE_BLOCK, tm, SLAB), F32),
        ],
        compiler_params=pltpu.CompilerParams(dimension_semantics=("arbitrary",),
                                             vmem_limit_bytes=VMEM_LIMIT_TOKEN_STAGES),
        name="ffn1_inproj",
    )(x2, x2, g_pre, wgu, wd, g_post, g_attn, w_in, b_in)
    h1, qa, kva = outs[:3]
    qb, kb, vb = outs[3:3 + n_dil], outs[3 + n_dil:3 + 2 * n_dil], outs[3 + 2 * n_dil:]
    return h1, qa, kva, qb, kb, vb


def _lane_is_low(shape):
    return lax.broadcasted_iota(jnp.int32, shape, 1) < HEAD_DIM


def _softmax_pv(q, k_masked, v_masked, bias):
    s = lax.dot_general(q, k_masked, (((1,), (1,)), ((), ())), preferred_element_type=F32) + bias
    m = jnp.max(s, axis=-1, keepdims=True)
    p = jnp.exp(s - m)
    den = jnp.sum(p, axis=-1, keepdims=True)
    o = jnp.dot(p.astype(BF16), v_masked, preferred_element_type=F32)
    return o, m, den


def _attn_b_kernel(q_ref, kp_ref, kc_ref, vp_ref, vc_ref, bias_ref, o_ref, lse_ref):
    first_chunk = pl.program_id(2) == 0
    n_blk, n_cls, rows, _ = q_ref.shape
    pair = 2 * HEAD_DIM
    low_kv = _lane_is_low((rows + ATTN_BLOCK, pair))
    low_out = _lane_is_low((ATTN_BLOCK, pair))
    for cls in range(n_cls):
        for jb in range(n_blk):
            k_all = jnp.concatenate([kp_ref[jb, cls], kc_ref[jb, cls]], axis=0)
            v_all = jnp.concatenate([vp_ref[jb, cls], vc_ref[jb, cls]], axis=0)
            for grp in range(LANE_BLOCK // pair):
                lanes = slice(grp * pair, (grp + 1) * pair)
                k_grp = k_all[:, lanes]
                v_grp = v_all[:, lanes]
                zero = jnp.zeros_like(k_grp)
                k_par = (jnp.where(low_kv, k_grp, zero), jnp.where(low_kv, zero, k_grp))
                v_par = (jnp.where(low_kv, v_grp, zero), jnp.where(low_kv, zero, v_grp))
                for t in range(rows // ATTN_BLOCK):
                    q = q_ref[jb, cls, t * ATTN_BLOCK:(t + 1) * ATTN_BLOCK, lanes]
                    keys = slice(t * ATTN_BLOCK, (t + 2) * ATTN_BLOCK)
                    outs, lses = [], []
                    for par in range(2):
                        head = jb * HEADS_PER_LANE_BLOCK + grp * 2 + par
                        if t == 0:
                            bias = bias_ref[head * 2 + jnp.where(first_chunk, 0, 1)]
                        else:
                            bias = bias_ref[head * 2 + 1]
                        o, m, den = _softmax_pv(q, k_par[par][keys], v_par[par][keys], bias)
                        outs.append(o / den)
                        lses.append(m + jnp.log(den))
                    out_lanes = slice(jb * LANE_BLOCK + grp * pair,
                                      jb * LANE_BLOCK + (grp + 1) * pair)
                    out_rows = slice(t * ATTN_BLOCK, (t + 1) * ATTN_BLOCK)
                    o_ref[cls, out_rows, out_lanes] = (outs[0] + outs[1]).astype(BF16)
                    lse_ref[cls, out_rows, out_lanes] = jnp.where(low_out, lses[0], lses[1])


def _attn_b(qb, kb, vb, bias):
    batch, n_blk, n_cls, class_len, _ = qb.shape
    rows = min(ATTN_ROWS, class_len)
    cls_per_step = min(n_cls, ATTN_ROWS // rows)
    prev_per_chunk = rows // ATTN_BLOCK

    cur_spec = pl.BlockSpec((None, n_blk, cls_per_step, rows, LANE_BLOCK),
                            lambda b, c, i: (b, 0, c, i, 0))
    prev_spec = pl.BlockSpec(
        (None, n_blk, cls_per_step, ATTN_BLOCK, LANE_BLOCK),
        lambda b, c, i: (b, 0, c, jnp.maximum(i * prev_per_chunk - 1, 0), 0))
    width = n_blk * LANE_BLOCK
    out_spec = pl.BlockSpec((None, cls_per_step, rows, width), lambda b, c, i: (b, c, i, 0))
    return pl.pallas_call(
        _attn_b_kernel,
        grid=(batch, n_cls // cls_per_step, class_len // rows),
        in_specs=[cur_spec, prev_spec, cur_spec, prev_spec, cur_spec,
                  _const_spec(bias.shape)],
        out_specs=[out_spec, out_spec],
        out_shape=[jax.ShapeDtypeStruct((batch, n_cls, class_len, width), BF16),
                   jax.ShapeDtypeStruct((batch, n_cls, class_len, width), F32)],
        compiler_params=pltpu.CompilerParams(
            dimension_semantics=("arbitrary", "arbitrary", "arbitrary"),
            vmem_limit_bytes=VMEM_LIMIT_ATTENTION),
        name=f"mixer_b_dil{n_cls}",
    )(qb, kb, kb, vb, vb, bias)


def _swap_halves(x):
    return jnp.concatenate([x[:, HEAD_DIM:], x[:, :HEAD_DIM]], axis=1)


def _attn_a_kernel(sink_ref, q_ref, kvp_ref, kvc_ref, bias_ref, o_ref):
    first_chunk = pl.program_id(1) == 0
    rows = q_ref.shape[1]
    pair = 2 * HEAD_DIM
    low_kv = _lane_is_low((rows + ATTN_BLOCK, pair))
    kv_all = jnp.concatenate([kvp_ref[...], kvc_ref[...]], axis=0)
    k_nat, v_nat = kv_all[:, :pair], kv_all[:, pair:]
    k_swp, v_swp = _swap_halves(k_nat), _swap_halves(v_nat)
    zero = jnp.zeros_like(k_nat)
    for g in range(A_KV_HEADS):
        k_low_src, k_high_src = (k_nat, k_swp) if g == 0 else (k_swp, k_nat)
        v_low_src, v_high_src = (v_nat, v_swp) if g == 0 else (v_swp, v_nat)
        k_par = (jnp.where(low_kv, k_low_src, zero), jnp.where(low_kv, zero, k_high_src))
        v_par = (jnp.where(low_kv, v_low_src, zero), jnp.where(low_kv, zero, v_high_src))
        for grp in range(LANE_BLOCK // pair):
            lanes = slice(grp * pair, (grp + 1) * pair)
            for t in range(rows // ATTN_BLOCK):
                q = q_ref[g, t * ATTN_BLOCK:(t + 1) * ATTN_BLOCK, lanes]
                keys = slice(t * ATTN_BLOCK, (t + 2) * ATTN_BLOCK)
                outs = []
                for par in range(2):
                    head = g * HEADS_PER_LANE_BLOCK + grp * 2 + par
                    if t == 0:
                        bias = bias_ref[head * 2 + jnp.where(first_chunk, 0, 1)]
                    else:
                        bias = bias_ref[head * 2 + 1]
                    o, m, den = _softmax_pv(q, k_par[par][keys], v_par[par][keys], bias)
                    sink = sink_ref[head]
                    m_all = jnp.maximum(m, sink)
                    scale = jnp.exp(m - m_all)
                    total = den * scale + jnp.exp(sink - m_all)
                    outs.append(o * (scale / total))
                out_lanes = slice(g * LANE_BLOCK + grp * pair, g * LANE_BLOCK + (grp + 1) * pair)
                o_ref[t * ATTN_BLOCK:(t + 1) * ATTN_BLOCK, out_lanes] = (
                    outs[0] + outs[1]).astype(BF16)


def _attn_a(qa, kva, bias, sinks):
    batch, n_blk, seq, _ = qa.shape
    rows = min(ATTN_ROWS, seq)
    prev_per_chunk = rows // ATTN_BLOCK
    width = n_blk * LANE_BLOCK
    out = pl.pallas_call(
        _attn_a_kernel,
        grid=(batch, seq // rows),
        in_specs=[
            pl.BlockSpec(memory_space=pltpu.SMEM),
            pl.BlockSpec((None, n_blk, rows, LANE_BLOCK), lambda b, i: (b, 0, i, 0)),
            pl.BlockSpec((None, ATTN_BLOCK, LANE_BLOCK),
                         lambda b, i: (b, jnp.maximum(i * prev_per_chunk - 1, 0), 0)),
            pl.BlockSpec((None, rows, LANE_BLOCK), lambda b, i: (b, i, 0)),
            _const_spec(bias.shape),
        ],
        out_specs=pl.BlockSpec((None, rows, width), lambda b, i: (b, i, 0)),
        out_shape=jax.ShapeDtypeStruct((batch, seq, width), BF16),
        compiler_params=pltpu.CompilerParams(
            dimension_semantics=("arbitrary", "arbitrary"),
            vmem_limit_bytes=VMEM_LIMIT_ATTENTION),
        name="mixer_a",
    )(sinks, qa, kva, kva, bias)
    return out.reshape(batch * seq, width)


def _stage3_kernel(*refs):
    n_dil = len(DILATIONS)
    h_ref, mixa_ref = refs[:2]
    o_refs = refs[2:2 + n_dil]
    l_refs = refs[2 + n_dil:2 + 2 * n_dil]
    (p_ref, wout_ref, bout_ref, g_attn_post_ref, g_pre_ref, wgu_ref, wd_ref, g_post_ref,
     g_ple_pre_ref, wgate_ref, wproj_ref, g_ple_post_ref,
     out_ref, xn_ref, act_ref, mixb_ref, slab_ref) = refs[2 + 2 * n_dil:]
    tm = h_ref.shape[0]
    n_slabs = mixa_ref.shape[1] // SLAB

    for idx, dil in enumerate(DILATIONS):
        if dil == 1:
            continue
        n = tm // dil
        for src_ref, which in ((o_refs[idx], 0), (l_refs[idx], 1)):
            base = (2 * idx + which) * n_slabs
            for c in range(dil):
                blk = src_ref[c].astype(F32)
                for s in range(n_slabs):
                    slab_ref[base + s, pl.ds(c, n, stride=dil), :] = blk[:, s * SLAB:(s + 1) * SLAB]

    for s in range(n_slabs):
        lanes = slice(s * SLAB, (s + 1) * SLAB)
        outs, lses = [], []
        for idx, dil in enumerate(DILATIONS):
            if dil == 1:
                outs.append(o_refs[idx][0, :, lanes].astype(F32))
                lses.append(l_refs[idx][0, :, lanes])
            else:
                outs.append(slab_ref[2 * idx * n_slabs + s])
                lses.append(slab_ref[(2 * idx + 1) * n_slabs + s])
        l_max = lses[0]
        for l in lses[1:]:
            l_max = jnp.maximum(l_max, l)
        weights = [jnp.exp(l - l_max) for l in lses]
        num = weights[0] * outs[0]
        den = weights[0]
        for w, o in zip(weights[1:], outs[1:]):
            num = num + w * o
            den = den + w
        mixb_ref[:, lanes] = (num / den).astype(BF16)

    d_a = mixa_ref.shape[1]
    att = (jnp.dot(mixa_ref[...], wout_ref[:d_a, :], preferred_element_type=F32)
           + jnp.dot(mixb_ref[...], wout_ref[d_a:, :], preferred_element_type=F32)
           + bout_ref[...])
    h = h_ref[...] + _rms(att, g_attn_post_ref[...])
    out_ref[...] = h

    xn_ref[...] = _rms(h, g_pre_ref[...]).astype(BF16)
    f = _swiglu(xn_ref, wgu_ref, wd_ref, act_ref)
    h = out_ref[...] + 0.5 * _rms(f, g_post_ref[...])
    out_ref[...] = h

    xn_ref[...] = _rms(h, g_ple_pre_ref[...]).astype(BF16)
    gate = _sigmoid(jnp.dot(xn_ref[...], wgate_ref[...], preferred_element_type=F32))
    e = jnp.dot(p_ref[...].astype(BF16), wproj_ref[...], preferred_element_type=F32)
    out_ref[...] = out_ref[...] + _rms(gate * e, g_ple_post_ref[...])


def _stage3(h1, mix_a, outs_b, lses_b, p2, w_out, b_out, g_attn_post, g_pre, wgu, wd, g_post,
            g_ple_pre, w_gate, w_proj, g_ple_post, seq):
    tokens, d_model = h1.shape
    tm = TOKEN_TILE
    tiles_per_seq = seq // tm
    d_ff = wd.shape[0]
    half = mix_a.shape[1]
    n_dil = len(DILATIONS)

    def rows(width):
        return pl.BlockSpec((tm, width), lambda i: (i, 0))

    def class_spec(dil):
        return pl.BlockSpec((None, dil, tm // dil, half),
                            lambda i: (i // tiles_per_seq, 0, i % tiles_per_seq, 0))

    vec = _const_spec((1, d_model))
    return pl.pallas_call(
        _stage3_kernel,
        grid=(tokens // tm,),
        in_specs=[rows(d_model), rows(half)]
        + [class_spec(d) for d in DILATIONS] * 2
        + [
            rows(p2.shape[1]),
            _const_spec(w_out.shape), vec, vec, vec,
            _const_spec(wgu.shape), _const_spec(wd.shape), vec, vec,
            _const_spec(w_gate.shape), _const_spec(w_proj.shape), vec,
        ],
        out_specs=rows(d_model),
        out_shape=jax.ShapeDtypeStruct((tokens, d_model), F32),
        scratch_shapes=[
            pltpu.VMEM((tm, d_model), BF16),
            pltpu.VMEM((tm, d_ff), BF16),
            pltpu.VMEM((tm, half), BF16),
            pltpu.VMEM((2 * n_dil * (half // SLAB), tm, SLAB), F32),
        ],
        compiler_params=pltpu.CompilerParams(dimension_semantics=("arbitrary",),
                                             vmem_limit_bytes=VMEM_LIMIT_TOKEN_STAGES),
        name="outproj_ffn2_ple",
    )(h1, mix_a, *outs_b, *lses_b, p2, w_out, b_out, g_attn_post, g_pre, wgu, wd, g_post,
      g_ple_pre, w_gate, w_proj, g_ple_post)


def kernel(x, p, rel_bias, ffn1_pre_g, ffn1_w_gu, ffn1_w_down, ffn1_post_g, attn_pre_g, w_in, b_in,
           sinks, w_out, b_out, attn_post_g, ffn2_pre_g, ffn2_w_gu, ffn2_w_down, ffn2_post_g,
           ple_pre_g, w_ple_gate, w_ple_proj, ple_post_g):
    batch, seq, d_model = x.shape
    depth = p.shape[0]
    assert seq % (max(DILATIONS) * ATTN_BLOCK) == 0 and seq % TOKEN_TILE == 0
    assert x.dtype == F32

    bias = _bias_tables(rel_bias)
    h = x.reshape(batch * seq, d_model)
    for i in range(depth):
        row = lambda v: v[i].reshape(1, -1)
        h1, qa, kva, qb, kb, vb = _stage1(
            h, row(ffn1_pre_g), ffn1_w_gu[i].astype(BF16), ffn1_w_down[i].astype(BF16),
            row(ffn1_post_g), row(attn_pre_g), w_in[i].astype(BF16), row(b_in), batch, seq)
        mix_a = _attn_a(qa, kva, bias[0], sinks[i])
        outs_b, lses_b = [], []
        for pat in range(len(DILATIONS)):
            o, lse = _attn_b(qb[pat], kb[pat], vb[pat], bias[1 + pat])
            outs_b.append(o)
            lses_b.append(lse)
        h = _stage3(
            h1, mix_a, outs_b, lses_b, p[i].reshape(batch * seq, -1),
            w_out[i].astype(BF16), row(b_out), row(attn_post_g), row(ffn2_pre_g),
            ffn2_w_gu[i].astype(BF16), ffn2_w_down[i].astype(BF16), row(ffn2_post_g),
            row(ple_pre_g), w_ple_gate[i].astype(BF16), w_ple_proj[i].astype(BF16),
            row(ple_post_g), seq)
    return h.reshape(batch, seq, d_model)
```

```python
import functools
import math

import numpy as np
import jax
import jax.numpy as jnp
from jax import lax
from jax.experimental import pallas as pl
from jax.experimental.pallas import tpu as pltpu

HEAD_DIM = 64
A_Q_HEADS = 8
A_KV_HEADS = 2
A_WINDOW = 128
B_HEADS = 8
B_PATTERNS = ((128, 1), (512, 4), (2048, 16))
DILATIONS = tuple(d for _, d in B_PATTERNS)
NUM_BUCKETS = 32
MAX_DISTANCE = 2048
EPS = 1e-6
NEG_INF = -1e30

ATTN_BLOCK = 128
LANE_BLOCK = 256
HEADS_PER_LANE_BLOCK = LANE_BLOCK // HEAD_DIM
SLAB = 128
SLABS_PER_LANE_BLOCK = LANE_BLOCK // SLAB
STAGING_SLOTS = 2
FF_CHUNK = 256
TOKEN_TILE = 512
ROW_CHUNK = 16
ATTN_ROWS = 512
VMEM_LIMIT_TOKEN_STAGES = 58 * 1024 * 1024
VMEM_LIMIT_ATTENTION = 40 * 1024 * 1024

F32 = jnp.float32
BF16 = jnp.bfloat16


def _rms(x, g):
    return x * lax.rsqrt(jnp.mean(x * x, axis=-1, keepdims=True) + EPS) * g


def _row_chunks(n_rows):
    return [slice(r, r + ROW_CHUNK) for r in range(0, n_rows, ROW_CHUNK)]


def _sigmoid(x):
    return 1.0 / (1.0 + jnp.exp(-x))


def _swiglu(xn_ref, wgu_ref, wd_ref, act_ref, side_work=None):
    d_ff = wd_ref.shape[0]
    side_work = dict(side_work or {})
    for c in range(d_ff // FF_CHUNK):
        lo = c * FF_CHUNK
        g = jnp.dot(xn_ref[...], wgu_ref[:, lo:lo + FF_CHUNK], preferred_element_type=F32)
        u = jnp.dot(xn_ref[...], wgu_ref[:, d_ff + lo:d_ff + lo + FF_CHUNK],
                    preferred_element_type=F32)
        act_ref[:, lo:lo + FF_CHUNK] = ((g * _sigmoid(g)) * u).astype(BF16)
        if c in side_work:
            side_work.pop(c)()
    assert not side_work
    return jnp.dot(act_ref[...], wd_ref[...], preferred_element_type=F32)


def _const_spec(shape):
    return pl.BlockSpec(shape, lambda *_: (0,) * len(shape), pipeline_mode=pl.Buffered(1))


def _t5_bucket_np(dist):
    max_exact = NUM_BUCKETS // 2
    n = np.maximum(dist, 0)
    nf = np.maximum(n, 1).astype(np.float32)
    large = max_exact + (np.log(nf / np.float32(max_exact))
                         / np.float32(math.log(MAX_DISTANCE / max_exact))
                         * np.float32(NUM_BUCKETS - max_exact)).astype(np.int32)
    large = np.minimum(large, NUM_BUCKETS - 1)
    return np.where(n < max_exact, n, large).astype(np.int32)


def _bucket_tables():
    q = np.arange(ATTN_BLOCK)[:, None]
    k = np.arange(2 * ATTN_BLOCK)[None, :]
    rel = ATTN_BLOCK + q - k
    tables = []
    for max_dist, stride in ((A_WINDOW - 1, 1),) + tuple((w // d, d) for w, d in B_PATTERNS):
        valid = (rel >= 0) & (rel <= max_dist)
        bkt = _t5_bucket_np(rel * stride)
        later = np.where(valid, bkt, -1)
        first = np.where(valid & (k >= ATTN_BLOCK), bkt, -1)
        tables.append(np.stack([first, later]))
    return np.stack(tables).astype(np.int32)


def _bias_kernel(rb_ref, bkt_ref, out_ref):
    pattern = pl.program_id(0)
    head = pl.program_id(1)
    col = jnp.where(pattern == 0, head, head + A_Q_HEADS)
    bkt = bkt_ref[...]
    acc = jnp.full(bkt.shape, NEG_INF, F32)
    for b in range(NUM_BUCKETS):
        acc = jnp.where(bkt == b, rb_ref[b, col], acc)
    out_ref[...] = acc


def _bias_tables(rel_bias):
    bkt = jnp.asarray(_bucket_tables())
    n_pat = bkt.shape[0]
    out = pl.pallas_call(
        _bias_kernel,
        grid=(n_pat, A_Q_HEADS),
        in_specs=[
            pl.BlockSpec(memory_space=pltpu.SMEM),
            pl.BlockSpec((None, 2, ATTN_BLOCK, 2 * ATTN_BLOCK), lambda p, h: (p, 0, 0, 0)),
        ],
        out_specs=pl.BlockSpec((None, None, 2, ATTN_BLOCK, 2 * ATTN_BLOCK),
                               lambda p, h: (p, h, 0, 0, 0)),
        out_shape=jax.ShapeDtypeStruct((n_pat, A_Q_HEADS, 2, ATTN_BLOCK, 2 * ATTN_BLOCK), F32),
        compiler_params=pltpu.CompilerParams(dimension_semantics=("arbitrary", "arbitrary")),
        name="bias_tables",
    )(rel_bias, bkt)
    return out.reshape(n_pat, A_Q_HEADS * 2, ATTN_BLOCK, 2 * ATTN_BLOCK)


def _stage1_kernel(x_ref, xprev_ref, g_pre_ref, wgu_ref, wd_ref, g_post_ref, g_attn_ref, win_ref,
                   bin_ref, h_ref, qa_ref, kva_ref, *rest):
    n_dil = len(DILATIONS)
    qb_refs, kb_refs, vb_refs = rest[:n_dil], rest[n_dil:2 * n_dil], rest[2 * n_dil:3 * n_dil]
    xn_ref, act_ref, f_ref, xn2_ref, slab_ref, slab2_ref = rest[3 * n_dil:]
    tm = x_ref.shape[0]
    step = pl.program_id(0)
    rd_slot = (step + 1) % 2
    wr_slot = step % 2

    @pl.when(step == 0)
    def _():
        f_ref[1] = jnp.zeros(f_ref.shape[1:], F32)

    for rows in _row_chunks(tm):
        xn_ref[rows] = _rms(x_ref[rows], g_pre_ref[...]).astype(BF16)
    for rows in _row_chunks(tm):
        h = xprev_ref[rows] + 0.5 * _rms(f_ref[rd_slot, rows], g_post_ref[...])
        h_ref[rows] = h
        xn2_ref[rows] = _rms(h, g_attn_ref[...]).astype(BF16)

    def proj(col_block):
        lo = col_block * LANE_BLOCK
        return (jnp.dot(xn2_ref[...], win_ref[:, lo:lo + LANE_BLOCK], preferred_element_type=F32)
                + bin_ref[:, lo:lo + LANE_BLOCK])

    def gather_rows(src_ref, base, start, n, stride):
        return [src_ref[base + s, pl.ds(start, n, stride=stride), :]
                for s in range(SLABS_PER_LANE_BLOCK)]

    def write_classes(z, refs, jb, slot):
        ref1, ref4, ref16 = refs
        base = slot * SLABS_PER_LANE_BLOCK
        ref1[jb, 0] = z.astype(BF16)
        for s in range(SLABS_PER_LANE_BLOCK):
            slab_ref[base + s] = z[:, s * SLAB:(s + 1) * SLAB]
        n4 = tm // 4
        for c4 in range(4):
            pieces = gather_rows(slab_ref, base, c4, n4, 4)
            ref4[jb, c4] = jnp.concatenate(pieces, axis=1).astype(BF16)
            for s in range(SLABS_PER_LANE_BLOCK):
                slab2_ref[base + s, c4 * n4:(c4 + 1) * n4, :] = pieces[s]
        for c4 in range(4):
            for j in range(4):
                pieces = gather_rows(slab2_ref, base, c4 * n4 + j, n4 // 4, 4)
                ref16[jb, c4 + 4 * j] = jnp.concatenate(pieces, axis=1).astype(BF16)

    q_scale = HEAD_DIM ** -0.5
    n_slots = slab_ref.shape[0] // SLABS_PER_LANE_BLOCK

    def store_kva():
        kva_ref[...] = proj(2).astype(BF16)

    def store_qa(j):
        qa_ref[j] = (proj(j) * q_scale).astype(BF16)

    def store_classes(col_block, scale, refs, jb, slot):
        z = proj(col_block)
        write_classes(z if scale == 1.0 else z * scale, refs, jb, slot % n_slots)

    projections = [store_kva]
    for j in range(2):
        projections += [
            functools.partial(store_qa, j),
            functools.partial(store_classes, 3 + j, q_scale, qb_refs, j, 3 * j),
            functools.partial(store_classes, 5 + j, 1.0, kb_refs, j, 3 * j + 1),
            functools.partial(store_classes, 7 + j, 1.0, vb_refs, j, 3 * j + 2),
        ]
    side_work = {1 + c: work for c, work in enumerate(projections)}
    f_ref[wr_slot] = _swiglu(xn_ref, wgu_ref, wd_ref, act_ref, side_work)


def _stage1(x2, g_pre, wgu, wd, g_post, g_attn, w_in, b_in, batch, seq):
    tokens, d_model = x2.shape
    tm = TOKEN_TILE
    tiles_per_seq = seq // tm
    d_ff = wd.shape[0]
    n_blk = 2
    n_dil = len(DILATIONS)
    n_tiles = tokens // tm
    assert DILATIONS == (1, 4, 16) and tm % (16 * 16) == 0

    def lagged(i):
        j = jnp.maximum(i - 1, 0)
        return j // tiles_per_seq, j % tiles_per_seq

    def class_spec(dil):
        return pl.BlockSpec((None, n_blk, dil, tm // dil, LANE_BLOCK),
                            lambda i: (lagged(i)[0], 0, 0, lagged(i)[1], 0))

    def class_shape(dil):
        return jax.ShapeDtypeStruct((batch, n_blk, dil, seq // dil, LANE_BLOCK), BF16)

    row_spec = pl.BlockSpec((tm, d_model), lambda i: (jnp.minimum(i, n_tiles - 1), 0))
    lagged_row_spec = pl.BlockSpec((tm, d_model), lambda i: (jnp.maximum(i - 1, 0), 0))
    outs = pl.pallas_call(
        _stage1_kernel,
        grid=(n_tiles + 1,),
        in_specs=[
            row_spec,
            lagged_row_spec,
            _const_spec((1, d_model)),
            _const_spec(wgu.shape),
            _const_spec(wd.shape),
            _const_spec((1, d_model)),
            _const_spec((1, d_model)),
            _const_spec(w_in.shape),
            _const_spec(b_in.shape),
        ],
        out_specs=[
            lagged_row_spec,
            pl.BlockSpec((None, n_blk, tm, LANE_BLOCK),
                         lambda i: (lagged(i)[0], 0, lagged(i)[1], 0)),
            pl.BlockSpec((None, tm, LANE_BLOCK),
                         lambda i: (lagged(i)[0], lagged(i)[1], 0)),
        ] + [class_spec(d) for _ in range(3) for d in DILATIONS],
        out_shape=[
            jax.ShapeDtypeStruct((tokens, d_model), F32),
            jax.ShapeDtypeStruct((batch, n_blk, seq, LANE_BLOCK), BF16),
            jax.ShapeDtypeStruct((batch, seq, LANE_BLOCK), BF16),
        ] + [class_shape(d) for _ in range(3) for d in DILATIONS],
        scratch_shapes=[
            pltpu.VMEM((tm, d_model), BF16),
            pltpu.VMEM((tm, d_ff), BF16),
            pltpu.VMEM((2, tm, d_model), F32),
            pltpu.VMEM((tm, d_model), BF16),
            pltpu.VMEM((STAGING_SLOTS * SLABS_PER_LANE_BLOCK, tm, SLAB), F32),
            pltpu.VMEM((STAGING_SLOTS * SLABS_PER_LANE_BLOCK, tm, SLAB), F32),
        ],
        compiler_params=pltpu.CompilerParams(dimension_semantics=("arbitrary",),
                                             vmem_limit_bytes=VMEM_LIMIT_TOKEN_STAGES),
        name="ffn1_inproj",
    )(x2, x2, g_pre, wgu, wd, g_post, g_attn, w_in, b_in)
    h1, qa, kva = outs[:3]
    qb, kb, vb = outs[3:3 + n_dil], outs[3 + n_dil:3 + 2 * n_dil], outs[3 + 2 * n_dil:]
    return h1, qa, kva, qb, kb, vb


def _lane_is_low(shape):
    return lax.broadcasted_iota(jnp.int32, shape, 1) < HEAD_DIM


def _masked_kv(k_grp, v_grp):
    low = _lane_is_low(k_grp.shape)
    zero = jnp.zeros_like(k_grp)
    lane = lax.broadcasted_iota(jnp.int32, k_grp.shape, 1)
    ones_low = jnp.clip(HEAD_DIM - lane, 0, 1).astype(F32).astype(BF16)
    ones_high = jnp.clip(lane - (HEAD_DIM - 1), 0, 1).astype(F32).astype(BF16)
    k_par = (jnp.where(low, k_grp, zero), jnp.where(low, zero, k_grp))
    v_par = (jnp.concatenate([jnp.where(low, v_grp, zero), ones_low], axis=1),
             jnp.concatenate([jnp.where(low, zero, v_grp), ones_high], axis=1))
    return k_par, v_par


def _pair_attention(q, k_par, v_par, biases):
    pair = 2 * HEAD_DIM
    acc = None
    maxes = []
    for par in range(2):
        s = lax.dot_general(q, k_par[par], (((1,), (1,)), ((), ())),
                            preferred_element_type=F32) + biases[par]
        m = jnp.max(s, axis=-1, keepdims=True)
        p = jnp.exp(s - m).astype(BF16)
        pv = jnp.dot(p, v_par[par], preferred_element_type=F32)
        acc = pv if acc is None else acc + pv
        maxes.append(m)
    m_pair = jnp.where(_lane_is_low((q.shape[0], pair)), maxes[0], maxes[1])
    return acc[:, :pair], acc[:, pair:], m_pair


def _attn_b_kernel(q_ref, kp_ref, kc_ref, vp_ref, vc_ref, bias_ref, o_ref, lse_ref):
    first_chunk = pl.program_id(2) == 0
    n_blk, n_cls, rows, _ = q_ref.shape
    pair = 2 * HEAD_DIM
    for cls in range(n_cls):
        for jb in range(n_blk):
            k_all = jnp.concatenate([kp_ref[jb, cls], kc_ref[jb, cls]], axis=0)
            v_all = jnp.concatenate([vp_ref[jb, cls], vc_ref[jb, cls]], axis=0)
            for grp in range(LANE_BLOCK // pair):
                lanes = slice(grp * pair, (grp + 1) * pair)
                k_par, v_par = _masked_kv(k_all[:, lanes], v_all[:, lanes])
                for t in range(rows // ATTN_BLOCK):
                    q = q_ref[jb, cls, t * ATTN_BLOCK:(t + 1) * ATTN_BLOCK, lanes]
                    keys = slice(t * ATTN_BLOCK, (t + 2) * ATTN_BLOCK)
                    variant = jnp.where(first_chunk, 0, 1) if t == 0 else 1
                    head = jb * HEADS_PER_LANE_BLOCK + grp * 2
                    biases = [bias_ref[(head + par) * 2 + variant] for par in range(2)]
                    o, den, m = _pair_attention(q, [k[keys] for k in k_par],
                                                [v[keys] for v in v_par], biases)
                    out_lanes = slice(jb * LANE_BLOCK + grp * pair,
                                      jb * LANE_BLOCK + (grp + 1) * pair)
                    out_rows = slice(t * ATTN_BLOCK, (t + 1) * ATTN_BLOCK)
                    o_ref[cls, out_rows, out_lanes] = (o / den).astype(BF16)
                    lse_ref[cls, out_rows, out_lanes] = m + jnp.log(den)


def _attn_b(qb, kb, vb, bias):
    batch, n_blk, n_cls, class_len, _ = qb.shape
    rows = min(ATTN_ROWS, class_len)
    cls_per_step = min(n_cls, ATTN_ROWS // rows)
    prev_per_chunk = rows // ATTN_BLOCK

    cur_spec = pl.BlockSpec((None, n_blk, cls_per_step, rows, LANE_BLOCK),
                            lambda b, c, i: (b, 0, c, i, 0))
    prev_spec = pl.BlockSpec(
        (None, n_blk, cls_per_step, ATTN_BLOCK, LANE_BLOCK),
        lambda b, c, i: (b, 0, c, jnp.maximum(i * prev_per_chunk - 1, 0), 0))
    width = n_blk * LANE_BLOCK
    out_spec = pl.BlockSpec((None, cls_per_step, rows, width), lambda b, c, i: (b, c, i, 0))
    return pl.pallas_call(
        _attn_b_kernel,
        grid=(batch, n_cls // cls_per_step, class_len // rows),
        in_specs=[cur_spec, prev_spec, cur_spec, prev_spec, cur_spec,
                  _const_spec(bias.shape)],
        out_specs=[out_spec, out_spec],
        out_shape=[jax.ShapeDtypeStruct((batch, n_cls, class_len, width), BF16),
                   jax.ShapeDtypeStruct((batch, n_cls, class_len, width), F32)],
        compiler_params=pltpu.CompilerParams(
            dimension_semantics=("arbitrary", "arbitrary", "arbitrary"),
            vmem_limit_bytes=VMEM_LIMIT_ATTENTION),
        name=f"mixer_b_dil{n_cls}",
    )(qb, kb, kb, vb, vb, bias)


def _swap_halves(x):
    return jnp.concatenate([x[:, HEAD_DIM:], x[:, :HEAD_DIM]], axis=1)


def _attn_a_kernel(sink_ref, q_ref, kvp_ref, kvc_ref, bias_ref, o_ref):
    first_chunk = pl.program_id(1) == 0
    rows = q_ref.shape[1]
    pair = 2 * HEAD_DIM
    low_out = _lane_is_low((ATTN_BLOCK, pair))
    kv_all = jnp.concatenate([kvp_ref[...], kvc_ref[...]], axis=0)
    k_nat, v_nat = kv_all[:, :pair], kv_all[:, pair:]

    def both_halves(x, g):
        half = x[:, g * HEAD_DIM:(g + 1) * HEAD_DIM]
        return jnp.concatenate([half, half], axis=1)

    for g in range(A_KV_HEADS):
        k_par, v_par = _masked_kv(both_halves(k_nat, g), both_halves(v_nat, g))
        for grp in range(LANE_BLOCK // pair):
            lanes = slice(grp * pair, (grp + 1) * pair)
            head = g * HEADS_PER_LANE_BLOCK + grp * 2
            sink = jnp.where(low_out, sink_ref[head], sink_ref[head + 1])
            for t in range(rows // ATTN_BLOCK):
                q = q_ref[g, t * ATTN_BLOCK:(t + 1) * ATTN_BLOCK, lanes]
                keys = slice(t * ATTN_BLOCK, (t + 2) * ATTN_BLOCK)
                variant = jnp.where(first_chunk, 0, 1) if t == 0 else 1
                biases = [bias_ref[(head + par) * 2 + variant] for par in range(2)]
                o, den, m = _pair_attention(q, [k[keys] for k in k_par],
                                            [v[keys] for v in v_par], biases)
                m_all = jnp.maximum(m, sink)
                scale = jnp.exp(m - m_all)
                total = den * scale + jnp.exp(sink - m_all)
                out_lanes = slice(g * LANE_BLOCK + grp * pair, g * LANE_BLOCK + (grp + 1) * pair)
                o_ref[t * ATTN_BLOCK:(t + 1) * ATTN_BLOCK, out_lanes] = (
                    o * (scale / total)).astype(BF16)


def _attn_a(qa, kva, bias, sinks):
    batch, n_blk, seq, _ = qa.shape
    rows = min(ATTN_ROWS, seq)
    prev_per_chunk = rows // ATTN_BLOCK
    width = n_blk * LANE_BLOCK
    out = pl.pallas_call(
        _attn_a_kernel,
        grid=(batch, seq // rows),
        in_specs=[
            pl.BlockSpec(memory_space=pltpu.SMEM),
            pl.BlockSpec((None, n_blk, rows, LANE_BLOCK), lambda b, i: (b, 0, i, 0)),
            pl.BlockSpec((None, ATTN_BLOCK, LANE_BLOCK),
                         lambda b, i: (b, jnp.maximum(i * prev_per_chunk - 1, 0), 0)),
            pl.BlockSpec((None, rows, LANE_BLOCK), lambda b, i: (b, i, 0)),
            _const_spec(bias.shape),
        ],
        out_specs=pl.BlockSpec((None, rows, width), lambda b, i: (b, i, 0)),
        out_shape=jax.ShapeDtypeStruct((batch, seq, width), BF16),
        compiler_params=pltpu.CompilerParams(
            dimension_semantics=("arbitrary", "arbitrary"),
            vmem_limit_bytes=VMEM_LIMIT_ATTENTION),
        name="mixer_a",
    )(sinks, qa, kva, kva, bias)
    return out.reshape(batch * seq, width)


def _stage3_kernel(*refs):
    n_dil = len(DILATIONS)
    h_ref, mixa_ref = refs[:2]
    o_refs = refs[2:2 + n_dil]
    l_refs = refs[2 + n_dil:2 + 2 * n_dil]
    (p_ref, wout_ref, bout_ref, g_attn_post_ref, g_pre_ref, wgu_ref, wd_ref, g_post_ref,
     g_ple_pre_ref, wgate_ref, wproj_ref, g_ple_post_ref,
     out_ref, xn_ref, act_ref, mixb_ref, slab_ref) = refs[2 + 2 * n_dil:]
    tm = h_ref.shape[0]
    n_slabs = mixa_ref.shape[1] // SLAB

    for idx, dil in enumerate(DILATIONS):
        if dil == 1:
            continue
        n = tm // dil
        for src_ref, which in ((o_refs[idx], 0), (l_refs[idx], 1)):
            base = (2 * idx + which) * n_slabs
            for c in range(dil):
                blk = src_ref[c].astype(F32)
                for s in range(n_slabs):
                    slab_ref[base + s, pl.ds(c, n, stride=dil), :] = blk[:, s * SLAB:(s + 1) * SLAB]

    for s in range(n_slabs):
        lanes = slice(s * SLAB, (s + 1) * SLAB)
        outs, lses = [], []
        for idx, dil in enumerate(DILATIONS):
            if dil == 1:
                outs.append(o_refs[idx][0, :, lanes].astype(F32))
                lses.append(l_refs[idx][0, :, lanes])
            else:
                outs.append(slab_ref[2 * idx * n_slabs + s])
                lses.append(slab_ref[(2 * idx + 1) * n_slabs + s])
        l_max = lses[0]
        for l in lses[1:]:
            l_max = jnp.maximum(l_max, l)
        weights = [jnp.exp(l - l_max) for l in lses]
        num = weights[0] * outs[0]
        den = weights[0]
        for w, o in zip(weights[1:], outs[1:]):
            num = num + w * o
            den = den + w
        mixb_ref[:, lanes] = (num / den).astype(BF16)

    d_a = mixa_ref.shape[1]
    att = (jnp.dot(mixa_ref[...], wout_ref[:d_a, :], preferred_element_type=F32)
           + jnp.dot(mixb_ref[...], wout_ref[d_a:, :], preferred_element_type=F32)
           + bout_ref[...])
    h = h_ref[...] + _rms(att, g_attn_post_ref[...])
    out_ref[...] = h

    xn_ref[...] = _rms(h, g_pre_ref[...]).astype(BF16)
    f = _swiglu(xn_ref, wgu_ref, wd_ref, act_ref)
    h = out_ref[...] + 0.5 * _rms(f, g_post_ref[...])
    out_ref[...] = h

    xn_ref[...] = _rms(h, g_ple_pre_ref[...]).astype(BF16)
    gate = _sigmoid(jnp.dot(xn_ref[...], wgate_ref[...], preferred_element_type=F32))
    e = jnp.dot(p_ref[...].astype(BF16), wproj_ref[...], preferred_element_type=F32)
    out_ref[...] = out_ref[...] + _rms(gate * e, g_ple_post_ref[...])


def _stage3(h1, mix_a, outs_b, lses_b, p2, w_out, b_out, g_attn_post, g_pre, wgu, wd, g_post,
            g_ple_pre, w_gate, w_proj, g_ple_post, seq):
    tokens, d_model = h1.shape
    tm = TOKEN_TILE
    tiles_per_seq = seq // tm
    d_ff = wd.shape[0]
    half = mix_a.shape[1]
    n_dil = len(DILATIONS)

    def rows(width):
        return pl.BlockSpec((tm, width), lambda i: (i, 0))

    def class_spec(dil):
        return pl.BlockSpec((None, dil, tm // dil, half),
                            lambda i: (i // tiles_per_seq, 0, i % tiles_per_seq, 0))

    vec = _const_spec((1, d_model))
    return pl.pallas_call(
        _stage3_kernel,
        grid=(tokens // tm,),
        in_specs=[rows(d_model), rows(half)]
        + [class_spec(d) for d in DILATIONS] * 2
        + [
            rows(p2.shape[1]),
            _const_spec(w_out.shape), vec, vec, vec,
            _const_spec(wgu.shape), _const_spec(wd.shape), vec, vec,
            _const_spec(w_gate.shape), _const_spec(w_proj.shape), vec,
        ],
        out_specs=rows(d_model),
        out_shape=jax.ShapeDtypeStruct((tokens, d_model), F32),
        scratch_shapes=[
            pltpu.VMEM((tm, d_model), BF16),
            pltpu.VMEM((tm, d_ff), BF16),
            pltpu.VMEM((tm, half), BF16),
            pltpu.VMEM((2 * n_dil * (half // SLAB), tm, SLAB), F32),
        ],
        compiler_params=pltpu.CompilerParams(dimension_semantics=("arbitrary",),
                                             vmem_limit_bytes=VMEM_LIMIT_TOKEN_STAGES),
        name="outproj_ffn2_ple",
    )(h1, mix_a, *outs_b, *lses_b, p2, w_out, b_out, g_attn_post, g_pre, wgu, wd, g_post,
      g_ple_pre, w_gate, w_proj, g_ple_post)


def kernel(x, p, rel_bias, ffn1_pre_g, ffn1_w_gu, ffn1_w_down, ffn1_post_g, attn_pre_g, w_in, b_in,
           sinks, w_out, b_out, attn_post_g, ffn2_pre_g, ffn2_w_gu, ffn2_w_down, ffn2_post_g,
           ple_pre_g, w_ple_gate, w_ple_proj, ple_post_g):
    batch, seq, d_model = x.shape
    depth = p.shape[0]
    assert seq % (max(DILATIONS) * ATTN_BLOCK) == 0 and seq % TOKEN_TILE == 0
    assert x.dtype == F32

    bias = _bias_tables(rel_bias)
    h = x.reshape(batch * seq, d_model)
    for i in range(depth):
        row = lambda v: v[i].reshape(1, -1)
        h1, qa, kva, qb, kb, vb = _stage1(
            h, row(ffn1_pre_g), ffn1_w_gu[i].astype(BF16), ffn1_w_down[i].astype(BF16),
            row(ffn1_post_g), row(attn_pre_g), w_in[i].astype(BF16), row(b_in), batch, seq)
        mix_a = _attn_a(qa, kva, bias[0], sinks[i])
        outs_b, lses_b = [], []
        for pat in range(len(DILATIONS)):
            o, lse = _attn_b(qb[pat], kb[pat], vb[pat], bias[1 + pat])
            outs_b.append(o)
            lses_b.append(lse)
        h = _stage3(
            h1, mix_a, outs_b, lses_b, p[i].reshape(batch * seq, -1),
            w_out[i].astype(BF16), row(b_out), row(attn_post_g), row(ffn2_pre_g),
            ffn2_w_gu[i].astype(BF16), ffn2_w_down[i].astype(BF16), row(ffn2_post_g),
            row(ple_pre_g), w_ple_gate[i].astype(BF16), w_ple_proj[i].astype(BF16),
            row(ple_post_g), seq)
    return h.reshape(batch, seq, d_model)
```

```python
import functools
import math

import numpy as np
import jax
import jax.numpy as jnp
from jax import lax
from jax.experimental import pallas as pl
from jax.experimental.pallas import tpu as pltpu

HEAD_DIM = 64
A_Q_HEADS = 8
A_KV_HEADS = 2
A_WINDOW = 128
B_HEADS = 8
B_PATTERNS = ((128, 1), (512, 4), (2048, 16))
DILATIONS = tuple(d for _, d in B_PATTERNS)
NUM_BUCKETS = 32
MAX_DISTANCE = 2048
EPS = 1e-6
NEG_INF = -1e30

ATTN_BLOCK = 128
LANE_BLOCK = 256
HEADS_PER_LANE_BLOCK = LANE_BLOCK // HEAD_DIM
SLAB = 128
SLABS_PER_LANE_BLOCK = LANE_BLOCK // SLAB
STAGING_SLOTS = 2
FF_CHUNK = 256
TOKEN_TILE = 512
ROW_CHUNK = 16
ATTN_ROWS = 1024
VMEM_LIMIT_TOKEN_STAGES = 58 * 1024 * 1024
VMEM_LIMIT_ATTENTION = 40 * 1024 * 1024

F32 = jnp.float32
BF16 = jnp.bfloat16


def _rms(x, g):
    return x * lax.rsqrt(jnp.mean(x * x, axis=-1, keepdims=True) + EPS) * g


def _row_chunks(n_rows):
    return [slice(r, r + ROW_CHUNK) for r in range(0, n_rows, ROW_CHUNK)]


def _sigmoid(x):
    return 1.0 / (1.0 + jnp.exp(-x))


def _swiglu(xn_ref, wgu_ref, wd_ref, act_ref, side_work=None):
    d_ff = wd_ref.shape[0]
    side_work = dict(side_work or {})
    for c in range(d_ff // FF_CHUNK):
        lo = c * FF_CHUNK
        g = jnp.dot(xn_ref[...], wgu_ref[:, lo:lo + FF_CHUNK], preferred_element_type=F32)
        u = jnp.dot(xn_ref[...], wgu_ref[:, d_ff + lo:d_ff + lo + FF_CHUNK],
                    preferred_element_type=F32)
        act_ref[:, lo:lo + FF_CHUNK] = ((g * _sigmoid(g)) * u).astype(BF16)
        if c in side_work:
            side_work.pop(c)()
    assert not side_work
    return jnp.dot(act_ref[...], wd_ref[...], preferred_element_type=F32)


def _const_spec(shape):
    return pl.BlockSpec(shape, lambda *_: (0,) * len(shape), pipeline_mode=pl.Buffered(1))


def _t5_bucket_np(dist):
    max_exact = NUM_BUCKETS // 2
    n = np.maximum(dist, 0)
    nf = np.maximum(n, 1).astype(np.float32)
    large = max_exact + (np.log(nf / np.float32(max_exact))
                         / np.float32(math.log(MAX_DISTANCE / max_exact))
                         * np.float32(NUM_BUCKETS - max_exact)).astype(np.int32)
    large = np.minimum(large, NUM_BUCKETS - 1)
    return np.where(n < max_exact, n, large).astype(np.int32)


def _bucket_tables():
    q = np.arange(ATTN_BLOCK)[:, None]
    k = np.arange(2 * ATTN_BLOCK)[None, :]
    rel = ATTN_BLOCK + q - k
    tables = []
    for max_dist, stride in ((A_WINDOW - 1, 1),) + tuple((w // d, d) for w, d in B_PATTERNS):
        valid = (rel >= 0) & (rel <= max_dist)
        bkt = _t5_bucket_np(rel * stride)
        later = np.where(valid, bkt, -1)
        first = np.where(valid & (k >= ATTN_BLOCK), bkt, -1)
        tables.append(np.stack([first, later]))
    return np.stack(tables).astype(np.int32)


def _bias_kernel(rb_ref, bkt_ref, out_ref):
    pattern = pl.program_id(0)
    head = pl.program_id(1)
    col = jnp.where(pattern == 0, head, head + A_Q_HEADS)
    bkt = bkt_ref[...]
    acc = jnp.full(bkt.shape, NEG_INF, F32)
    for b in range(NUM_BUCKETS):
        acc = jnp.where(bkt == b, rb_ref[b, col], acc)
    out_ref[...] = acc


def _bias_tables(rel_bias):
    bkt = jnp.asarray(_bucket_tables())
    n_pat = bkt.shape[0]
    out = pl.pallas_call(
        _bias_kernel,
        grid=(n_pat, A_Q_HEADS),
        in_specs=[
            pl.BlockSpec(memory_space=pltpu.SMEM),
            pl.BlockSpec((None, 2, ATTN_BLOCK, 2 * ATTN_BLOCK), lambda p, h: (p, 0, 0, 0)),
        ],
        out_specs=pl.BlockSpec((None, None, 2, ATTN_BLOCK, 2 * ATTN_BLOCK),
                               lambda p, h: (p, h, 0, 0, 0)),
        out_shape=jax.ShapeDtypeStruct((n_pat, A_Q_HEADS, 2, ATTN_BLOCK, 2 * ATTN_BLOCK), F32),
        compiler_params=pltpu.CompilerParams(dimension_semantics=("arbitrary", "arbitrary")),
        name="bias_tables",
    )(rel_bias, bkt)
    return out.reshape(n_pat, A_Q_HEADS * 2, ATTN_BLOCK, 2 * ATTN_BLOCK)


def _stage1_kernel(x_ref, xprev_ref, g_pre_ref, wgu_ref, wd_ref, g_post_ref, g_attn_ref, win_ref,
                   bin_ref, h_ref, qa_ref, kva_ref, *rest):
    n_dil = len(DILATIONS)
    qb_refs, kb_refs, vb_refs = rest[:n_dil], rest[n_dil:2 * n_dil], rest[2 * n_dil:3 * n_dil]
    xn_ref, act_ref, f_ref, xn2_ref, slab_ref, slab2_ref = rest[3 * n_dil:]
    tm = x_ref.shape[0]
    step = pl.program_id(0)
    rd_slot = (step + 1) % 2
    wr_slot = step % 2

    @pl.when(step == 0)
    def _():
        f_ref[1] = jnp.zeros(f_ref.shape[1:], F32)

    for rows in _row_chunks(tm):
        xn_ref[rows] = _rms(x_ref[rows], g_pre_ref[...]).astype(BF16)
    for rows in _row_chunks(tm):
        h = xprev_ref[rows] + 0.5 * _rms(f_ref[rd_slot, rows], g_post_ref[...])
        h_ref[rows] = h
        xn2_ref[rows] = _rms(h, g_attn_ref[...]).astype(BF16)

    def proj(col_block):
        lo = col_block * LANE_BLOCK
        return (jnp.dot(xn2_ref[...], win_ref[:, lo:lo + LANE_BLOCK], preferred_element_type=F32)
                + bin_ref[:, lo:lo + LANE_BLOCK])

    def gather_rows(src_ref, base, start, n, stride):
        return [src_ref[base + s, pl.ds(start, n, stride=stride), :]
                for s in range(SLABS_PER_LANE_BLOCK)]

    def write_classes(z, refs, jb, slot):
        ref1, ref4, ref16 = refs
        base = slot * SLABS_PER_LANE_BLOCK
        ref1[jb, 0] = z.astype(BF16)
        for s in range(SLABS_PER_LANE_BLOCK):
            slab_ref[base + s] = z[:, s * SLAB:(s + 1) * SLAB]
        n4 = tm // 4
        for c4 in range(4):
            pieces = gather_rows(slab_ref, base, c4, n4, 4)
            ref4[jb, c4] = jnp.concatenate(pieces, axis=1).astype(BF16)
            for s in range(SLABS_PER_LANE_BLOCK):
                slab2_ref[base + s, c4 * n4:(c4 + 1) * n4, :] = pieces[s]
        for c4 in range(4):
            for j in range(4):
                pieces = gather_rows(slab2_ref, base, c4 * n4 + j, n4 // 4, 4)
                ref16[jb, c4 + 4 * j] = jnp.concatenate(pieces, axis=1).astype(BF16)

    q_scale = HEAD_DIM ** -0.5
    n_slots = slab_ref.shape[0] // SLABS_PER_LANE_BLOCK

    def store_kva():
        kva_ref[...] = proj(2).astype(BF16)

    def store_qa(j):
        qa_ref[j] = (proj(j) * q_scale).astype(BF16)

    def store_classes(col_block, scale, refs, jb, slot):
        z = proj(col_block)
        write_classes(z if scale == 1.0 else z * scale, refs, jb, slot % n_slots)

    projections = [store_kva]
    for j in range(2):
        projections += [
            functools.partial(store_qa, j),
            functools.partial(store_classes, 3 + j, q_scale, qb_refs, j, 3 * j),
            functools.partial(store_classes, 5 + j, 1.0, kb_refs, j, 3 * j + 1),
            functools.partial(store_classes, 7 + j, 1.0, vb_refs, j, 3 * j + 2),
        ]
    side_work = {1 + c: work for c, work in enumerate(projections)}
    f_ref[wr_slot] = _swiglu(xn_ref, wgu_ref, wd_ref, act_ref, side_work)


def _stage1(x2, g_pre, wgu, wd, g_post, g_attn, w_in, b_in, batch, seq):
    tokens, d_model = x2.shape
    tm = TOKEN_TILE
    tiles_per_seq = seq // tm
    d_ff = wd.shape[0]
    n_blk = 2
    n_dil = len(DILATIONS)
    n_tiles = tokens // tm
    assert DILATIONS == (1, 4, 16) and tm % (16 * 16) == 0

    def lagged(i):
        j = jnp.maximum(i - 1, 0)
        return j // tiles_per_seq, j % tiles_per_seq

    def class_spec(dil):
        return pl.BlockSpec((None, n_blk, dil, tm // dil, LANE_BLOCK),
                            lambda i: (lagged(i)[0], 0, 0, lagged(i)[1], 0))

    def class_shape(dil):
        return jax.ShapeDtypeStruct((batch, n_blk, dil, seq // dil, LANE_BLOCK), BF16)

    row_spec = pl.BlockSpec((tm, d_model), lambda i: (jnp.minimum(i, n_tiles - 1), 0))
    lagged_row_spec = pl.BlockSpec((tm, d_model), lambda i: (jnp.maximum(i - 1, 0), 0))
    outs = pl.pallas_call(
        _stage1_kernel,
        grid=(n_tiles + 1,),
        in_specs=[
            row_spec,
            lagged_row_spec,
            _const_spec((1, d_model)),
            _const_spec(wgu.shape),
            _const_spec(wd.shape),
            _const_spec((1, d_model)),
            _const_spec((1, d_model)),
            _const_spec(w_in.shape),
            _const_spec(b_in.shape),
        ],
        out_specs=[
            lagged_row_spec,
            pl.BlockSpec((None, n_blk, tm, LANE_BLOCK),
                         lambda i: (lagged(i)[0], 0, lagged(i)[1], 0)),
            pl.BlockSpec((None, tm, LANE_BLOCK),
                         lambda i: (lagged(i)[0], lagged(i)[1], 0)),
        ] + [class_spec(d) for _ in range(3) for d in DILATIONS],
        out_shape=[
            jax.ShapeDtypeStruct((tokens, d_model), F32),
            jax.ShapeDtypeStruct((batch, n_blk, seq, LANE_BLOCK), BF16),
            jax.ShapeDtypeStruct((batch, seq, LANE_BLOCK), BF16),
        ] + [class_shape(d) for _ in range(3) for d in DILATIONS],
        scratch_shapes=[
            pltpu.VMEM((tm, d_model), BF16),
            pltpu.VMEM((tm, d_ff), BF16),
            pltpu.VMEM((2, tm, d_model), F32),
            pltpu.VMEM((tm, d_model), BF16),
            pltpu.VMEM((STAGING_SLOTS * SLABS_PER_LANE_BLOCK, tm, SLAB), F32),
            pltpu.VMEM((STAGING_SLOTS * SLABS_PER_LANE_BLOCK, tm, SLAB), F32),
        ],
        compiler_params=pltpu.CompilerParams(dimension_semantics=("arbitrary",),
                                             vmem_limit_bytes=VMEM_LIMIT_TOKEN_STAGES),
        name="ffn1_inproj",
    )(x2, x2, g_pre, wgu, wd, g_post, g_attn, w_in, b_in)
    h1, qa, kva = outs[:3]
    qb, kb, vb = outs[3:3 + n_dil], outs[3 + n_dil:3 + 2 * n_dil], outs[3 + 2 * n_dil:]
    return h1, qa, kva, qb, kb, vb


def _lane_is_low(shape):
    return lax.broadcasted_iota(jnp.int32, shape, 1) < HEAD_DIM


def _masked_kv(k_grp, v_grp):
    low = _lane_is_low(k_grp.shape)
    zero = jnp.zeros_like(k_grp)
    lane = lax.broadcasted_iota(jnp.int32, k_grp.shape, 1)
    ones_low = jnp.clip(HEAD_DIM - lane, 0, 1).astype(F32).astype(BF16)
    ones_high = jnp.clip(lane - (HEAD_DIM - 1), 0, 1).astype(F32).astype(BF16)
    k_par = (jnp.where(low, k_grp, zero), jnp.where(low, zero, k_grp))
    v_par = (jnp.concatenate([jnp.where(low, v_grp, zero), ones_low], axis=1),
             jnp.concatenate([jnp.where(low, zero, v_grp), ones_high], axis=1))
    return k_par, v_par


def _pair_attention(q, k_par, v_par, biases):
    pair = 2 * HEAD_DIM
    acc = None
    maxes = []
    for par in range(2):
        s = lax.dot_general(q, k_par[par], (((1,), (1,)), ((), ())),
                            preferred_element_type=F32) + biases[par]
        m = jnp.max(s, axis=-1, keepdims=True)
        p = jnp.exp(s - m).astype(BF16)
        pv = jnp.dot(p, v_par[par], preferred_element_type=F32)
        acc = pv if acc is None else acc + pv
        maxes.append(m)
    m_pair = jnp.where(_lane_is_low((q.shape[0], pair)), maxes[0], maxes[1])
    return acc[:, :pair], acc[:, pair:], m_pair


def _attn_b_kernel(q_ref, kp_ref, kc_ref, vp_ref, vc_ref, bias_ref, o_ref, lse_ref):
    first_chunk = pl.program_id(2) == 0
    n_blk, n_cls, rows, _ = q_ref.shape
    pair = 2 * HEAD_DIM
    for cls in range(n_cls):
        for jb in range(n_blk):
            k_all = jnp.concatenate([kp_ref[jb, cls], kc_ref[jb, cls]], axis=0)
            v_all = jnp.concatenate([vp_ref[jb, cls], vc_ref[jb, cls]], axis=0)
            for grp in range(LANE_BLOCK // pair):
                lanes = slice(grp * pair, (grp + 1) * pair)
                k_par, v_par = _masked_kv(k_all[:, lanes], v_all[:, lanes])
                for t in range(rows // ATTN_BLOCK):
                    q = q_ref[jb, cls, t * ATTN_BLOCK:(t + 1) * ATTN_BLOCK, lanes]
                    keys = slice(t * ATTN_BLOCK, (t + 2) * ATTN_BLOCK)
                    variant = jnp.where(first_chunk, 0, 1) if t == 0 else 1
                    head = jb * HEADS_PER_LANE_BLOCK + grp * 2
                    biases = [bias_ref[(head + par) * 2 + variant] for par in range(2)]
                    o, den, m = _pair_attention(q, [k[keys] for k in k_par],
                                                [v[keys] for v in v_par], biases)
                    out_lanes = slice(jb * LANE_BLOCK + grp * pair,
                                      jb * LANE_BLOCK + (grp + 1) * pair)
                    out_rows = slice(t * ATTN_BLOCK, (t + 1) * ATTN_BLOCK)
                    o_ref[cls, out_rows, out_lanes] = (o / den).astype(BF16)
                    lse_ref[cls, out_rows, out_lanes] = m + jnp.log(den)


def _attn_b(qb, kb, vb, bias):
    batch, n_blk, n_cls, class_len, _ = qb.shape
    rows = min(ATTN_ROWS, class_len)
    cls_per_step = min(n_cls, ATTN_ROWS // rows)
    prev_per_chunk = rows // ATTN_BLOCK

    cur_spec = pl.BlockSpec((None, n_blk, cls_per_step, rows, LANE_BLOCK),
                            lambda b, c, i: (b, 0, c, i, 0))
    prev_spec = pl.BlockSpec(
        (None, n_blk, cls_per_step, ATTN_BLOCK, LANE_BLOCK),
        lambda b, c, i: (b, 0, c, jnp.maximum(i * prev_per_chunk - 1, 0), 0))
    width = n_blk * LANE_BLOCK
    out_spec = pl.BlockSpec((None, cls_per_step, rows, width), lambda b, c, i: (b, c, i, 0))
    return pl.pallas_call(
        _attn_b_kernel,
        grid=(batch, n_cls // cls_per_step, class_len // rows),
        in_specs=[cur_spec, prev_spec, cur_spec, prev_spec, cur_spec,
                  _const_spec(bias.shape)],
        out_specs=[out_spec, out_spec],
        out_shape=[jax.ShapeDtypeStruct((batch, n_cls, class_len, width), BF16),
                   jax.ShapeDtypeStruct((batch, n_cls, class_len, width), F32)],
        compiler_params=pltpu.CompilerParams(
            dimension_semantics=("arbitrary", "arbitrary", "arbitrary"),
            vmem_limit_bytes=VMEM_LIMIT_ATTENTION),
        name=f"mixer_b_dil{n_cls}",
    )(qb, kb, kb, vb, vb, bias)


def _swap_halves(x):
    return jnp.concatenate([x[:, HEAD_DIM:], x[:, :HEAD_DIM]], axis=1)


def _attn_a_kernel(sink_ref, q_ref, kvp_ref, kvc_ref, bias_ref, o_ref):
    first_chunk = pl.program_id(1) == 0
    rows = q_ref.shape[1]
    pair = 2 * HEAD_DIM
    low_out = _lane_is_low((ATTN_BLOCK, pair))
    kv_all = jnp.concatenate([kvp_ref[...], kvc_ref[...]], axis=0)
    k_nat, v_nat = kv_all[:, :pair], kv_all[:, pair:]

    def both_halves(x, g):
        half = x[:, g * HEAD_DIM:(g + 1) * HEAD_DIM]
        return jnp.concatenate([half, half], axis=1)

    for g in range(A_KV_HEADS):
        k_par, v_par = _masked_kv(both_halves(k_nat, g), both_halves(v_nat, g))
        for grp in range(LANE_BLOCK // pair):
            lanes = slice(grp * pair, (grp + 1) * pair)
            head = g * HEADS_PER_LANE_BLOCK + grp * 2
            sink = jnp.where(low_out, sink_ref[head], sink_ref[head + 1])
            for t in range(rows // ATTN_BLOCK):
                q = q_ref[g, t * ATTN_BLOCK:(t + 1) * ATTN_BLOCK, lanes]
                keys = slice(t * ATTN_BLOCK, (t + 2) * ATTN_BLOCK)
                variant = jnp.where(first_chunk, 0, 1) if t == 0 else 1
                biases = [bias_ref[(head + par) * 2 + variant] for par in range(2)]
                o, den, m = _pair_attention(q, [k[keys] for k in k_par],
                                            [v[keys] for v in v_par], biases)
                m_all = jnp.maximum(m, sink)
                scale = jnp.exp(m - m_all)
                total = den * scale + jnp.exp(sink - m_all)
                out_lanes = slice(g * LANE_BLOCK + grp * pair, g * LANE_BLOCK + (grp + 1) * pair)
                o_ref[t * ATTN_BLOCK:(t + 1) * ATTN_BLOCK, out_lanes] = (
                    o * (scale / total)).astype(BF16)


def _attn_a(qa, kva, bias, sinks):
    batch, n_blk, seq, _ = qa.shape
    rows = min(ATTN_ROWS, seq)
    prev_per_chunk = rows // ATTN_BLOCK
    width = n_blk * LANE_BLOCK
    out = pl.pallas_call(
        _attn_a_kernel,
        grid=(batch, seq // rows),
        in_specs=[
            pl.BlockSpec(memory_space=pltpu.SMEM),
            pl.BlockSpec((None, n_blk, rows, LANE_BLOCK), lambda b, i: (b, 0, i, 0)),
            pl.BlockSpec((None, ATTN_BLOCK, LANE_BLOCK),
                         lambda b, i: (b, jnp.maximum(i * prev_per_chunk - 1, 0), 0)),
            pl.BlockSpec((None, rows, LANE_BLOCK), lambda b, i: (b, i, 0)),
            _const_spec(bias.shape),
        ],
        out_specs=pl.BlockSpec((None, rows, width), lambda b, i: (b, i, 0)),
        out_shape=jax.ShapeDtypeStruct((batch, seq, width), BF16),
        compiler_params=pltpu.CompilerParams(
            dimension_semantics=("arbitrary", "arbitrary"),
            vmem_limit_bytes=VMEM_LIMIT_ATTENTION),
        name="mixer_a",
    )(sinks, qa, kva, kva, bias)
    return out.reshape(batch * seq, width)


def _stage3_kernel(*refs):
    n_dil = len(DILATIONS)
    h_ref, mixa_ref = refs[:2]
    o_refs = refs[2:2 + n_dil]
    l_refs = refs[2 + n_dil:2 + 2 * n_dil]
    (p_ref, wout_ref, bout_ref, g_attn_post_ref, g_pre_ref, wgu_ref, wd_ref, g_post_ref,
     g_ple_pre_ref, wgate_ref, wproj_ref, g_ple_post_ref,
     out_ref, xn_ref, act_ref, mixb_ref, slab_ref) = refs[2 + 2 * n_dil:]
    tm = h_ref.shape[0]
    n_slabs = mixa_ref.shape[1] // SLAB

    def interleave(s):
        for k, dil in enumerate(d for d in DILATIONS if d != 1):
            idx = DILATIONS.index(dil)
            n = tm // dil
            sub = min(n, 2 * ROW_CHUNK)
            for src_ref, which in ((o_refs[idx], 0), (l_refs[idx], 1)):
                dst = (2 * k + which) * n_slabs + s
                for c in range(dil):
                    for r0 in range(0, n, sub):
                        blk = src_ref[c, r0:r0 + sub, s * SLAB:(s + 1) * SLAB].astype(F32)
                        slab_ref[dst, pl.ds(c + r0 * dil, sub, stride=dil), :] = blk

    def merge(s):
        lanes = slice(s * SLAB, (s + 1) * SLAB)
        for r0 in range(0, tm, 2 * ROW_CHUNK):
            rows = slice(r0, r0 + 2 * ROW_CHUNK)
            outs, lses = [], []
            k = 0
            for idx, dil in enumerate(DILATIONS):
                if dil == 1:
                    outs.append(o_refs[idx][0, rows, lanes].astype(F32))
                    lses.append(l_refs[idx][0, rows, lanes])
                else:
                    outs.append(slab_ref[2 * k * n_slabs + s, rows, :])
                    lses.append(slab_ref[(2 * k + 1) * n_slabs + s, rows, :])
                    k += 1
            l_max = lses[0]
            for l in lses[1:]:
                l_max = jnp.maximum(l_max, l)
            weights = [jnp.exp(l - l_max) for l in lses]
            num = weights[0] * outs[0]
            den = weights[0]
            for w, o in zip(weights[1:], outs[1:]):
                num = num + w * o
                den = den + w
            mixb_ref[rows, lanes] = (num / den).astype(BF16)

    d_a = mixa_ref.shape[1]
    att = jnp.dot(mixa_ref[...], wout_ref[:d_a, :], preferred_element_type=F32) + bout_ref[...]
    slabs_per_part = 2
    for part in range(n_slabs // slabs_per_part):
        for s in range(part * slabs_per_part, (part + 1) * slabs_per_part):
            interleave(s)
            merge(s)
        lo = part * slabs_per_part * SLAB
        hi = lo + slabs_per_part * SLAB
        att = att + jnp.dot(mixb_ref[:, lo:hi], wout_ref[d_a + lo:d_a + hi, :],
                            preferred_element_type=F32)
    h = h_ref[...] + _rms(att, g_attn_post_ref[...])
    out_ref[...] = h

    xn_ref[...] = _rms(h, g_pre_ref[...]).astype(BF16)
    f = _swiglu(xn_ref, wgu_ref, wd_ref, act_ref)
    h = out_ref[...] + 0.5 * _rms(f, g_post_ref[...])
    out_ref[...] = h

    xn_ref[...] = _rms(h, g_ple_pre_ref[...]).astype(BF16)
    gate = _sigmoid(jnp.dot(xn_ref[...], wgate_ref[...], preferred_element_type=F32))
    e = jnp.dot(p_ref[...].astype(BF16), wproj_ref[...], preferred_element_type=F32)
    out_ref[...] = out_ref[...] + _rms(gate * e, g_ple_post_ref[...])


def _stage3(h1, mix_a, outs_b, lses_b, p2, w_out, b_out, g_attn_post, g_pre, wgu, wd, g_post,
            g_ple_pre, w_gate, w_proj, g_ple_post, seq):
    tokens, d_model = h1.shape
    tm = TOKEN_TILE
    tiles_per_seq = seq // tm
    d_ff = wd.shape[0]
    half = mix_a.shape[1]
    n_dil = len(DILATIONS)

    def rows(width):
        return pl.BlockSpec((tm, width), lambda i: (i, 0))

    def class_spec(dil):
        return pl.BlockSpec((None, dil, tm // dil, half),
                            lambda i: (i // tiles_per_seq, 0, i % tiles_per_seq, 0))

    vec = _const_spec((1, d_model))
    return pl.pallas_call(
        _stage3_kernel,
        grid=(tokens // tm,),
        in_specs=[rows(d_model), rows(half)]
        + [class_spec(d) for d in DILATIONS] * 2
        + [
            rows(p2.shape[1]),
            _const_spec(w_out.shape), vec, vec, vec,
            _const_spec(wgu.shape), _const_spec(wd.shape), vec, vec,
            _const_spec(w_gate.shape), _const_spec(w_proj.shape), vec,
        ],
        out_specs=rows(d_model),
        out_shape=jax.ShapeDtypeStruct((tokens, d_model), F32),
        scratch_shapes=[
            pltpu.VMEM((tm, d_model), BF16),
            pltpu.VMEM((tm, d_ff), BF16),
            pltpu.VMEM((tm, half), BF16),
            pltpu.VMEM((2 * (n_dil - 1) * (half // SLAB), tm, SLAB), F32),
        ],
        compiler_params=pltpu.CompilerParams(dimension_semantics=("arbitrary",),
                                             vmem_limit_bytes=VMEM_LIMIT_TOKEN_STAGES),
        name="outproj_ffn2_ple",
    )(h1, mix_a, *outs_b, *lses_b, p2, w_out, b_out, g_attn_post, g_pre, wgu, wd, g_post,
      g_ple_pre, w_gate, w_proj, g_ple_post)


def kernel(x, p, rel_bias, ffn1_pre_g, ffn1_w_gu, ffn1_w_down, ffn1_post_g, attn_pre_g, w_in, b_in,
           sinks, w_out, b_out, attn_post_g, ffn2_pre_g, ffn2_w_gu, ffn2_w_down, ffn2_post_g,
           ple_pre_g, w_ple_gate, w_ple_proj, ple_post_g):
    batch, seq, d_model = x.shape
    depth = p.shape[0]
    assert seq % (max(DILATIONS) * ATTN_BLOCK) == 0 and seq % TOKEN_TILE == 0
    assert x.dtype == F32

    bias = _bias_tables(rel_bias)
    h = x.reshape(batch * seq, d_model)
    for i in range(depth):
        row = lambda v: v[i].reshape(1, -1)
        h1, qa, kva, qb, kb, vb = _stage1(
            h, row(ffn1_pre_g), ffn1_w_gu[i].astype(BF16), ffn1_w_down[i].astype(BF16),
            row(ffn1_post_g), row(attn_pre_g), w_in[i].astype(BF16), row(b_in), batch, seq)
        mix_a = _attn_a(qa, kva, bias[0], sinks[i])
        outs_b, lses_b = [], []
        for pat in range(len(DILATIONS)):
            o, lse = _attn_b(qb[pat], kb[pat], vb[pat], bias[1 + pat])
            outs_b.append(o)
            lses_b.append(lse)
        h = _stage3(
            h1, mix_a, outs_b, lses_b, p[i].reshape(batch * seq, -1),
            w_out[i].astype(BF16), row(b_out), row(attn_post_g), row(ffn2_pre_g),
            ffn2_w_gu[i].astype(BF16), ffn2_w_down[i].astype(BF16), row(ffn2_post_g),
            row(ple_pre_g), w_ple_gate[i].astype(BF16), w_ple_proj[i].astype(BF16),
            row(ple_post_g), seq)
    return h.reshape(batch, seq, d_model)
```

```python
import functools
import math

import numpy as np
import jax
import jax.numpy as jnp
from jax import lax
from jax.experimental import pallas as pl
from jax.experimental.pallas import tpu as pltpu

HEAD_DIM = 64
A_Q_HEADS = 8
A_KV_HEADS = 2
A_WINDOW = 128
B_HEADS = 8
B_PATTERNS = ((128, 1), (512, 4), (2048, 16))
DILATIONS = tuple(d for _, d in B_PATTERNS)
NUM_BUCKETS = 32
MAX_DISTANCE = 2048
EPS = 1e-6
NEG_INF = -1e30

ATTN_BLOCK = 128
LANE_BLOCK = 256
HEADS_PER_LANE_BLOCK = LANE_BLOCK // HEAD_DIM
SLAB = 128
SLABS_PER_LANE_BLOCK = LANE_BLOCK // SLAB
STAGING_SLOTS = 2
FF_CHUNK = 256
TOKEN_TILE = 512
ROW_CHUNK = 16
WEIGHT_STAGE_ROWS = 256
ATTN_ROWS = 1024
VMEM_LIMIT_TOKEN_STAGES = 58 * 1024 * 1024
VMEM_LIMIT_ATTENTION = 40 * 1024 * 1024

F32 = jnp.float32
BF16 = jnp.bfloat16


def _rms(x, g):
    return x * lax.rsqrt(jnp.mean(x * x, axis=-1, keepdims=True) + EPS) * g


def _row_chunks(n_rows):
    return [slice(r, r + ROW_CHUNK) for r in range(0, n_rows, ROW_CHUNK)]


def _sigmoid(x):
    return 1.0 / (1.0 + jnp.exp(-x))


def _swiglu(xn_ref, wgu_ref, wd_ref, act_ref, side_work=None):
    d_ff = wd_ref.shape[0]
    side_work = dict(side_work or {})
    for c in range(d_ff // FF_CHUNK):
        lo = c * FF_CHUNK
        g = jnp.dot(xn_ref[...], wgu_ref[:, lo:lo + FF_CHUNK], preferred_element_type=F32)
        u = jnp.dot(xn_ref[...], wgu_ref[:, d_ff + lo:d_ff + lo + FF_CHUNK],
                    preferred_element_type=F32)
        act_ref[:, lo:lo + FF_CHUNK] = ((g * _sigmoid(g)) * u).astype(BF16)
        if c in side_work:
            side_work.pop(c)()
    assert not side_work
    return jnp.dot(act_ref[...], wd_ref[...], preferred_element_type=F32)


def _load_as_bf16(src_hbm, dst_ref, stage_ref, sem_ref):
    s_rows, s_cols = stage_ref.shape[1:]
    rows, cols = src_hbm.shape
    pieces = [(r, min(s_rows, rows - r), c, min(s_cols, cols - c))
              for r in range(0, rows, s_rows) for c in range(0, cols, s_cols)]

    def copy(i):
        r, nr, c, nc = pieces[i]
        return pltpu.make_async_copy(src_hbm.at[pl.ds(r, nr), pl.ds(c, nc)],
                                     stage_ref.at[i % 2, pl.ds(0, nr), pl.ds(0, nc)],
                                     sem_ref.at[i % 2])

    copy(0).start()
    for i, (r, nr, c, nc) in enumerate(pieces):
        if i + 1 < len(pieces):
            copy(i + 1).start()
        copy(i).wait()
        dst_ref[r:r + nr, c:c + nc] = stage_ref[i % 2, :nr, :nc].astype(BF16)


def _const_spec(shape):
    return pl.BlockSpec(shape, lambda *_: (0,) * len(shape), pipeline_mode=pl.Buffered(1))


def _t5_bucket_np(dist):
    max_exact = NUM_BUCKETS // 2
    n = np.maximum(dist, 0)
    nf = np.maximum(n, 1).astype(np.float32)
    large = max_exact + (np.log(nf / np.float32(max_exact))
                         / np.float32(math.log(MAX_DISTANCE / max_exact))
                         * np.float32(NUM_BUCKETS - max_exact)).astype(np.int32)
    large = np.minimum(large, NUM_BUCKETS - 1)
    return np.where(n < max_exact, n, large).astype(np.int32)


def _bucket_tables():
    q = np.arange(ATTN_BLOCK)[:, None]
    k = np.arange(2 * ATTN_BLOCK)[None, :]
    rel = ATTN_BLOCK + q - k
    tables = []
    for max_dist, stride in ((A_WINDOW - 1, 1),) + tuple((w // d, d) for w, d in B_PATTERNS):
        valid = (rel >= 0) & (rel <= max_dist)
        bkt = _t5_bucket_np(rel * stride)
        later = np.where(valid, bkt, -1)
        first = np.where(valid & (k >= ATTN_BLOCK), bkt, -1)
        tables.append(np.stack([first, later]))
    return np.stack(tables).astype(np.int32)


def _bias_kernel(buckets_present, rb_ref, bkt_ref, out_ref):
    head = pl.program_id(0)
    first_block_keys = lax.broadcasted_iota(jnp.int32, bkt_ref.shape[2:], 1) >= ATTN_BLOCK
    for pattern, present in enumerate(buckets_present):
        col = head if pattern == 0 else head + A_Q_HEADS
        bkt = bkt_ref[pattern, 1]
        later = jnp.full(bkt.shape, NEG_INF, F32)
        for b in present:
            later = jnp.where(bkt == b, rb_ref[b, col], later)
        out_ref[pattern, 1] = later
        out_ref[pattern, 0] = jnp.where(first_block_keys, later, NEG_INF)


def _bias_tables(rel_bias):
    bkt_np = _bucket_tables()
    assert (bkt_np[:, 0] == np.where(np.arange(2 * ATTN_BLOCK) >= ATTN_BLOCK,
                                     bkt_np[:, 1], -1)).all()
    buckets_present = tuple(tuple(int(b) for b in np.unique(t[1]) if b >= 0) for t in bkt_np)
    n_pat = bkt_np.shape[0]
    table = (2, ATTN_BLOCK, 2 * ATTN_BLOCK)
    out = pl.pallas_call(
        functools.partial(_bias_kernel, buckets_present),
        grid=(A_Q_HEADS,),
        in_specs=[
            pl.BlockSpec(memory_space=pltpu.SMEM),
            pl.BlockSpec((n_pat,) + table, lambda h: (0, 0, 0, 0)),
        ],
        out_specs=pl.BlockSpec((n_pat, None) + table, lambda h: (0, h, 0, 0, 0)),
        out_shape=jax.ShapeDtypeStruct((n_pat, A_Q_HEADS) + table, F32),
        compiler_params=pltpu.CompilerParams(dimension_semantics=("arbitrary",)),
        name="bias_tables",
    )(rel_bias, jnp.asarray(bkt_np))
    return out.reshape(n_pat, A_Q_HEADS * 2, ATTN_BLOCK, 2 * ATTN_BLOCK)


def _stage1_kernel(x_ref, xprev_ref, g_pre_ref, wgu_hbm, wd_hbm, g_post_ref, g_attn_ref, win_hbm,
                   bin_ref, h_ref, qa_ref, kva_ref, *rest):
    n_dil = len(DILATIONS)
    qb_refs, kb_refs, vb_refs = rest[:n_dil], rest[n_dil:2 * n_dil], rest[2 * n_dil:3 * n_dil]
    (xn_ref, act_ref, f_ref, xn2_ref, slab_ref, slab2_ref,
     wgu_ref, wd_ref, win_ref, sem_ref) = rest[3 * n_dil:]
    tm = x_ref.shape[0]
    step = pl.program_id(0)
    rd_slot = (step + 1) % 2
    wr_slot = step % 2

    @pl.when(step == 0)
    def _():
        _load_as_bf16(wgu_hbm, wgu_ref, f_ref, sem_ref)
        _load_as_bf16(wd_hbm, wd_ref, f_ref, sem_ref)
        _load_as_bf16(win_hbm, win_ref, f_ref, sem_ref)
        f_ref[1] = jnp.zeros(f_ref.shape[1:], F32)

    for rows in _row_chunks(tm):
        xn_ref[rows] = _rms(x_ref[rows], g_pre_ref[...]).astype(BF16)
    for rows in _row_chunks(tm):
        h = xprev_ref[rows] + 0.5 * _rms(f_ref[rd_slot, rows], g_post_ref[...])
        h_ref[rows] = h
        xn2_ref[rows] = _rms(h, g_attn_ref[...]).astype(BF16)

    def proj(col_block):
        lo = col_block * LANE_BLOCK
        return (jnp.dot(xn2_ref[...], win_ref[:, lo:lo + LANE_BLOCK], preferred_element_type=F32)
                + bin_ref[:, lo:lo + LANE_BLOCK])

    def gather_rows(src_ref, base, start, n, stride):
        return [src_ref[base + s, pl.ds(start, n, stride=stride), :]
                for s in range(SLABS_PER_LANE_BLOCK)]

    def write_classes(z, refs, jb, slot):
        ref1, ref4, ref16 = refs
        base = slot * SLABS_PER_LANE_BLOCK
        ref1[jb, 0] = z.astype(BF16)
        for s in range(SLABS_PER_LANE_BLOCK):
            slab_ref[base + s] = z[:, s * SLAB:(s + 1) * SLAB]
        n4 = tm // 4
        for c4 in range(4):
            pieces = gather_rows(slab_ref, base, c4, n4, 4)
            ref4[jb, c4] = jnp.concatenate(pieces, axis=1).astype(BF16)
            for s in range(SLABS_PER_LANE_BLOCK):
                slab2_ref[base + s, c4 * n4:(c4 + 1) * n4, :] = pieces[s]
        for c4 in range(4):
            for j in range(4):
                pieces = gather_rows(slab2_ref, base, c4 * n4 + j, n4 // 4, 4)
                ref16[jb, c4 + 4 * j] = jnp.concatenate(pieces, axis=1).astype(BF16)

    q_scale = HEAD_DIM ** -0.5
    n_slots = slab_ref.shape[0] // SLABS_PER_LANE_BLOCK

    def store_kva():
        kva_ref[...] = proj(2).astype(BF16)

    def store_qa(j):
        qa_ref[j] = (proj(j) * q_scale).astype(BF16)

    def store_classes(col_block, scale, refs, jb, slot):
        z = proj(col_block)
        write_classes(z if scale == 1.0 else z * scale, refs, jb, slot % n_slots)

    projections = [store_kva]
    for j in range(2):
        projections += [
            functools.partial(store_qa, j),
            functools.partial(store_classes, 3 + j, q_scale, qb_refs, j, 3 * j),
            functools.partial(store_classes, 5 + j, 1.0, kb_refs, j, 3 * j + 1),
            functools.partial(store_classes, 7 + j, 1.0, vb_refs, j, 3 * j + 2),
        ]
    side_work = {1 + c: work for c, work in enumerate(projections)}
    f_ref[wr_slot] = _swiglu(xn_ref, wgu_ref, wd_ref, act_ref, side_work)


def _stage1(x2, g_pre, wgu, wd, g_post, g_attn, w_in, b_in, batch, seq):
    tokens, d_model = x2.shape
    tm = TOKEN_TILE
    tiles_per_seq = seq // tm
    d_ff = wd.shape[0]
    n_blk = 2
    n_dil = len(DILATIONS)
    n_tiles = tokens // tm
    assert DILATIONS == (1, 4, 16) and tm % (16 * 16) == 0

    def lagged(i):
        j = jnp.maximum(i - 1, 0)
        return j // tiles_per_seq, j % tiles_per_seq

    def class_spec(dil):
        return pl.BlockSpec((None, n_blk, dil, tm // dil, LANE_BLOCK),
                            lambda i: (lagged(i)[0], 0, 0, lagged(i)[1], 0))

    def class_shape(dil):
        return jax.ShapeDtypeStruct((batch, n_blk, dil, seq // dil, LANE_BLOCK), BF16)

    row_spec = pl.BlockSpec((tm, d_model), lambda i: (jnp.minimum(i, n_tiles - 1), 0))
    lagged_row_spec = pl.BlockSpec((tm, d_model), lambda i: (jnp.maximum(i - 1, 0), 0))
    hbm_spec = pl.BlockSpec(memory_space=pl.ANY)
    outs = pl.pallas_call(
        _stage1_kernel,
        grid=(n_tiles + 1,),
        in_specs=[
            row_spec,
            lagged_row_spec,
            _const_spec((1, d_model)),
            hbm_spec,
            hbm_spec,
            _const_spec((1, d_model)),
            _const_spec((1, d_model)),
            hbm_spec,
            _const_spec(b_in.shape),
        ],
        out_specs=[
            lagged_row_spec,
            pl.BlockSpec((None, n_blk, tm, LANE_BLOCK),
                         lambda i: (lagged(i)[0], 0, lagged(i)[1], 0)),
            pl.BlockSpec((None, tm, LANE_BLOCK),
                         lambda i: (lagged(i)[0], lagged(i)[1], 0)),
        ] + [class_spec(d) for _ in range(3) for d in DILATIONS],
        out_shape=[
            jax.ShapeDtypeStruct((tokens, d_model), F32),
            jax.ShapeDtypeStruct((batch, n_blk, seq, LANE_BLOCK), BF16),
            jax.ShapeDtypeStruct((batch, seq, LANE_BLOCK), BF16),
        ] + [class_shape(d) for _ in range(3) for d in DILATIONS],
        scratch_shapes=[
            pltpu.VMEM((tm, d_model), BF16),
            pltpu.VMEM((tm, d_ff), BF16),
            pltpu.VMEM((2, tm, d_model), F32),
            pltpu.VMEM((tm, d_model), BF16),
            pltpu.VMEM((STAGING_SLOTS * SLABS_PER_LANE_BLOCK, tm, SLAB), F32),
            pltpu.VMEM((STAGING_SLOTS * SLABS_PER_LANE_BLOCK, tm, SLAB), F32),
            pltpu.VMEM(wgu.shape, BF16),
            pltpu.VMEM(wd.shape, BF16),
            pltpu.VMEM(w_in.shape, BF16),
            pltpu.SemaphoreType.DMA((2,)),
        ],
        compiler_params=pltpu.CompilerParams(dimension_semantics=("arbitrary",),
                                             vmem_limit_bytes=VMEM_LIMIT_TOKEN_STAGES),
        name="ffn1_inproj",
    )(x2, x2, g_pre, wgu, wd, g_post, g_attn, w_in, b_in)
    h1, qa, kva = outs[:3]
    qb, kb, vb = outs[3:3 + n_dil], outs[3 + n_dil:3 + 2 * n_dil], outs[3 + 2 * n_dil:]
    return h1, qa, kva, qb, kb, vb


def _lane_is_low(shape):
    return lax.broadcasted_iota(jnp.int32, shape, 1) < HEAD_DIM


def _masked_kv(k_grp, v_grp):
    low = _lane_is_low(k_grp.shape)
    zero = jnp.zeros_like(k_grp)
    lane = lax.broadcasted_iota(jnp.int32, k_grp.shape, 1)
    ones_low = jnp.clip(HEAD_DIM - lane, 0, 1).astype(F32).astype(BF16)
    ones_high = jnp.clip(lane - (HEAD_DIM - 1), 0, 1).astype(F32).astype(BF16)
    k_par = (jnp.where(low, k_grp, zero), jnp.where(low, zero, k_grp))
    v_par = (jnp.concatenate([jnp.where(low, v_grp, zero), ones_low], axis=1),
             jnp.concatenate([jnp.where(low, zero, v_grp), ones_high], axis=1))
    return k_par, v_par


def _pair_attention(q, k_par, v_par, biases):
    pair = 2 * HEAD_DIM
    acc = None
    maxes = []
    for par in range(2):
        s = lax.dot_general(q, k_par[par], (((1,), (1,)), ((), ())),
                            preferred_element_type=F32) + biases[par]
        m = jnp.max(s, axis=-1, keepdims=True)
        p = jnp.exp(s - m).astype(BF16)
        pv = jnp.dot(p, v_par[par], preferred_element_type=F32)
        acc = pv if acc is None else acc + pv
        maxes.append(m)
    m_pair = jnp.where(_lane_is_low((q.shape[0], pair)), maxes[0], maxes[1])
    return acc[:, :pair], acc[:, pair:], m_pair


def _attn_b_kernel(q_ref, kp_ref, kc_ref, vp_ref, vc_ref, bias_ref, o_ref, lse_ref):
    first_chunk = pl.program_id(2) == 0
    n_blk, n_cls, rows, _ = q_ref.shape
    pair = 2 * HEAD_DIM
    for cls in range(n_cls):
        for jb in range(n_blk):
            k_all = jnp.concatenate([kp_ref[jb, cls], kc_ref[jb, cls]], axis=0)
            v_all = jnp.concatenate([vp_ref[jb, cls], vc_ref[jb, cls]], axis=0)
            for grp in range(LANE_BLOCK // pair):
                lanes = slice(grp * pair, (grp + 1) * pair)
                k_par, v_par = _masked_kv(k_all[:, lanes], v_all[:, lanes])
                for t in range(rows // ATTN_BLOCK):
                    q = q_ref[jb, cls, t * ATTN_BLOCK:(t + 1) * ATTN_BLOCK, lanes]
                    keys = slice(t * ATTN_BLOCK, (t + 2) * ATTN_BLOCK)
                    variant = jnp.where(first_chunk, 0, 1) if t == 0 else 1
                    head = jb * HEADS_PER_LANE_BLOCK + grp * 2
                    biases = [bias_ref[(head + par) * 2 + variant] for par in range(2)]
                    o, den, m = _pair_attention(q, [k[keys] for k in k_par],
                                                [v[keys] for v in v_par], biases)
                    out_lanes = slice(jb * LANE_BLOCK + grp * pair,
                                      jb * LANE_BLOCK + (grp + 1) * pair)
                    out_rows = slice(t * ATTN_BLOCK, (t + 1) * ATTN_BLOCK)
                    o_ref[cls, out_rows, out_lanes] = (o / den).astype(BF16)
                    lse_ref[cls, out_rows, out_lanes] = m + jnp.log(den)


def _attn_b(qb, kb, vb, bias):
    batch, n_blk, n_cls, class_len, _ = qb.shape
    rows = min(ATTN_ROWS, class_len)
    cls_per_step = min(n_cls, ATTN_ROWS // rows)
    prev_per_chunk = rows // ATTN_BLOCK

    cur_spec = pl.BlockSpec((None, n_blk, cls_per_step, rows, LANE_BLOCK),
                            lambda b, c, i: (b, 0, c, i, 0))
    prev_spec = pl.BlockSpec(
        (None, n_blk, cls_per_step, ATTN_BLOCK, LANE_BLOCK),
        lambda b, c, i: (b, 0, c, jnp.maximum(i * prev_per_chunk - 1, 0), 0))
    width = n_blk * LANE_BLOCK
    out_spec = pl.BlockSpec((None, cls_per_step, rows, width), lambda b, c, i: (b, c, i, 0))
    return pl.pallas_call(
        _attn_b_kernel,
        grid=(batch, n_cls // cls_per_step, class_len // rows),
        in_specs=[cur_spec, prev_spec, cur_spec, prev_spec, cur_spec,
                  _const_spec(bias.shape)],
        out_specs=[out_spec, out_spec],
        out_shape=[jax.ShapeDtypeStruct((batch, n_cls, class_len, width), BF16),
                   jax.ShapeDtypeStruct((batch, n_cls, class_len, width), F32)],
        compiler_params=pltpu.CompilerParams(
            dimension_semantics=("arbitrary", "arbitrary", "arbitrary"),
            vmem_limit_bytes=VMEM_LIMIT_ATTENTION),
        name=f"mixer_b_dil{n_cls}",
    )(qb, kb, kb, vb, vb, bias)


def _swap_halves(x):
    return jnp.concatenate([x[:, HEAD_DIM:], x[:, :HEAD_DIM]], axis=1)


def _attn_a_kernel(sink_ref, q_ref, kvp_ref, kvc_ref, bias_ref, o_ref):
    first_chunk = pl.program_id(1) == 0
    rows = q_ref.shape[1]
    pair = 2 * HEAD_DIM
    low_out = _lane_is_low((ATTN_BLOCK, pair))
    kv_all = jnp.concatenate([kvp_ref[...], kvc_ref[...]], axis=0)
    k_nat, v_nat = kv_all[:, :pair], kv_all[:, pair:]

    def both_halves(x, g):
        half = x[:, g * HEAD_DIM:(g + 1) * HEAD_DIM]
        return jnp.concatenate([half, half], axis=1)

    for g in range(A_KV_HEADS):
        k_par, v_par = _masked_kv(both_halves(k_nat, g), both_halves(v_nat, g))
        for grp in range(LANE_BLOCK // pair):
            lanes = slice(grp * pair, (grp + 1) * pair)
            head = g * HEADS_PER_LANE_BLOCK + grp * 2
            sink = jnp.where(low_out, sink_ref[head], sink_ref[head + 1])
            for t in range(rows // ATTN_BLOCK):
                q = q_ref[g, t * ATTN_BLOCK:(t + 1) * ATTN_BLOCK, lanes]
                keys = slice(t * ATTN_BLOCK, (t + 2) * ATTN_BLOCK)
                variant = jnp.where(first_chunk, 0, 1) if t == 0 else 1
                biases = [bias_ref[(head + par) * 2 + variant] for par in range(2)]
                o, den, m = _pair_attention(q, [k[keys] for k in k_par],
                                            [v[keys] for v in v_par], biases)
                total = den + jnp.exp(sink - m)
                out_lanes = slice(g * LANE_BLOCK + grp * pair, g * LANE_BLOCK + (grp + 1) * pair)
                o_ref[t * ATTN_BLOCK:(t + 1) * ATTN_BLOCK, out_lanes] = (o / total).astype(BF16)


def _attn_a(qa, kva, bias, sinks):
    batch, n_blk, seq, _ = qa.shape
    rows = min(ATTN_ROWS, seq)
    prev_per_chunk = rows // ATTN_BLOCK
    width = n_blk * LANE_BLOCK
    out = pl.pallas_call(
        _attn_a_kernel,
        grid=(batch, seq // rows),
        in_specs=[
            pl.BlockSpec(memory_space=pltpu.SMEM),
            pl.BlockSpec((None, n_blk, rows, LANE_BLOCK), lambda b, i: (b, 0, i, 0)),
            pl.BlockSpec((None, ATTN_BLOCK, LANE_BLOCK),
                         lambda b, i: (b, jnp.maximum(i * prev_per_chunk - 1, 0), 0)),
            pl.BlockSpec((None, rows, LANE_BLOCK), lambda b, i: (b, i, 0)),
            _const_spec(bias.shape),
        ],
        out_specs=pl.BlockSpec((None, rows, width), lambda b, i: (b, i, 0)),
        out_shape=jax.ShapeDtypeStruct((batch, seq, width), BF16),
        compiler_params=pltpu.CompilerParams(
            dimension_semantics=("arbitrary", "arbitrary"),
            vmem_limit_bytes=VMEM_LIMIT_ATTENTION),
        name="mixer_a",
    )(sinks, qa, kva, kva, bias)
    return out.reshape(batch * seq, width)


def _stage3_kernel(*refs):
    n_dil = len(DILATIONS)
    h_ref, mixa_ref = refs[:2]
    o_refs = refs[2:2 + n_dil]
    l_refs = refs[2 + n_dil:2 + 2 * n_dil]
    (p_ref, wout_hbm, bout_ref, g_attn_post_ref, g_pre_ref, wgu_hbm, wd_hbm, g_post_ref,
     g_ple_pre_ref, wgate_hbm, wproj_hbm, g_ple_post_ref,
     out_ref, xn_ref, act_ref, mixb_ref, slab_ref,
     wout_ref, wgu_ref, wd_ref, wgate_ref, wproj_ref, stage_ref, sem_ref) = refs[2 + 2 * n_dil:]
    tm = h_ref.shape[0]
    n_slabs = mixa_ref.shape[1] // SLAB

    @pl.when(pl.program_id(0) == 0)
    def _():
        for src, dst in ((wout_hbm, wout_ref), (wgu_hbm, wgu_ref), (wd_hbm, wd_ref),
                         (wgate_hbm, wgate_ref), (wproj_hbm, wproj_ref)):
            _load_as_bf16(src, dst, stage_ref, sem_ref)

    def interleave(s):
        for k, dil in enumerate(d for d in DILATIONS if d != 1):
            idx = DILATIONS.index(dil)
            n = tm // dil
            sub = min(n, 2 * ROW_CHUNK)
            for src_ref, which in ((o_refs[idx], 0), (l_refs[idx], 1)):
                dst = (2 * k + which) * n_slabs + s
                for c in range(dil):
                    for r0 in range(0, n, sub):
                        blk = src_ref[c, r0:r0 + sub, s * SLAB:(s + 1) * SLAB].astype(F32)
                        slab_ref[dst, pl.ds(c + r0 * dil, sub, stride=dil), :] = blk

    def merge(s):
        lanes = slice(s * SLAB, (s + 1) * SLAB)
        for r0 in range(0, tm, 2 * ROW_CHUNK):
            rows = slice(r0, r0 + 2 * ROW_CHUNK)
            outs, lses = [], []
            k = 0
            for idx, dil in enumerate(DILATIONS):
                if dil == 1:
                    outs.append(o_refs[idx][0, rows, lanes].astype(F32))
                    lses.append(l_refs[idx][0, rows, lanes])
                else:
                    outs.append(slab_ref[2 * k * n_slabs + s, rows, :])
                    lses.append(slab_ref[(2 * k + 1) * n_slabs + s, rows, :])
                    k += 1
            l_max = lses[0]
            for l in lses[1:]:
                l_max = jnp.maximum(l_max, l)
            weights = [jnp.exp(l - l_max) for l in lses]
            num = weights[0] * outs[0]
            den = weights[0]
            for w, o in zip(weights[1:], outs[1:]):
                num = num + w * o
                den = den + w
            mixb_ref[rows, lanes] = (num / den).astype(BF16)

    d_a = mixa_ref.shape[1]
    att = jnp.dot(mixa_ref[...], wout_ref[:d_a, :], preferred_element_type=F32) + bout_ref[...]
    slabs_per_part = 2
    for part in range(n_slabs // slabs_per_part):
        for s in range(part * slabs_per_part, (part + 1) * slabs_per_part):
            interleave(s)
            merge(s)
        lo = part * slabs_per_part * SLAB
        hi = lo + slabs_per_part * SLAB
        att = att + jnp.dot(mixb_ref[:, lo:hi], wout_ref[d_a + lo:d_a + hi, :],
                            preferred_element_type=F32)
    h = h_ref[...] + _rms(att, g_attn_post_ref[...])
    out_ref[...] = h

    xn_ref[...] = _rms(h, g_pre_ref[...]).astype(BF16)
    f = _swiglu(xn_ref, wgu_ref, wd_ref, act_ref)
    h = out_ref[...] + 0.5 * _rms(f, g_post_ref[...])
    out_ref[...] = h

    xn_ref[...] = _rms(h, g_ple_pre_ref[...]).astype(BF16)
    gate = _sigmoid(jnp.dot(xn_ref[...], wgate_ref[...], preferred_element_type=F32))
    e = jnp.dot(p_ref[...].astype(BF16), wproj_ref[...], preferred_element_type=F32)
    out_ref[...] = out_ref[...] + _rms(gate * e, g_ple_post_ref[...])


def _stage3(h1, mix_a, outs_b, lses_b, p2, w_out, b_out, g_attn_post, g_pre, wgu, wd, g_post,
            g_ple_pre, w_gate, w_proj, g_ple_post, seq):
    tokens, d_model = h1.shape
    tm = TOKEN_TILE
    tiles_per_seq = seq // tm
    d_ff = wd.shape[0]
    half = mix_a.shape[1]
    n_dil = len(DILATIONS)

    def rows(width):
        return pl.BlockSpec((tm, width), lambda i: (i, 0))

    def class_spec(dil):
        return pl.BlockSpec((None, dil, tm // dil, half),
                            lambda i: (i // tiles_per_seq, 0, i % tiles_per_seq, 0))

    vec = _const_spec((1, d_model))
    hbm = pl.BlockSpec(memory_space=pl.ANY)
    return pl.pallas_call(
        _stage3_kernel,
        grid=(tokens // tm,),
        in_specs=[rows(d_model), rows(half)]
        + [class_spec(d) for d in DILATIONS] * 2
        + [
            rows(p2.shape[1]),
            hbm, vec, vec, vec,
            hbm, hbm, vec, vec,
            hbm, hbm, vec,
        ],
        out_specs=rows(d_model),
        out_shape=jax.ShapeDtypeStruct((tokens, d_model), F32),
        scratch_shapes=[
            pltpu.VMEM((tm, d_model), BF16),
            pltpu.VMEM((tm, d_ff), BF16),
            pltpu.VMEM((tm, half), BF16),
            pltpu.VMEM((2 * (n_dil - 1) * (half // SLAB), tm, SLAB), F32),
            pltpu.VMEM(w_out.shape, BF16),
            pltpu.VMEM(wgu.shape, BF16),
            pltpu.VMEM(wd.shape, BF16),
            pltpu.VMEM(w_gate.shape, BF16),
            pltpu.VMEM(w_proj.shape, BF16),
            pltpu.VMEM((2, WEIGHT_STAGE_ROWS, d_model), F32),
            pltpu.SemaphoreType.DMA((2,)),
        ],
        compiler_params=pltpu.CompilerParams(dimension_semantics=("arbitrary",),
                                             vmem_limit_bytes=VMEM_LIMIT_TOKEN_STAGES),
        name="outproj_ffn2_ple",
    )(h1, mix_a, *outs_b, *lses_b, p2, w_out, b_out, g_attn_post, g_pre, wgu, wd, g_post,
      g_ple_pre, w_gate, w_proj, g_ple_post)


def kernel(x, p, rel_bias, ffn1_pre_g, ffn1_w_gu, ffn1_w_down, ffn1_post_g, attn_pre_g, w_in, b_in,
           sinks, w_out, b_out, attn_post_g, ffn2_pre_g, ffn2_w_gu, ffn2_w_down, ffn2_post_g,
           ple_pre_g, w_ple_gate, w_ple_proj, ple_post_g):
    batch, seq, d_model = x.shape
    depth = p.shape[0]
    assert seq % (max(DILATIONS) * ATTN_BLOCK) == 0 and seq % TOKEN_TILE == 0
    assert x.dtype == F32

    bias = _bias_tables(rel_bias)
    h = x.reshape(batch * seq, d_model)
    for i in range(depth):
        row = lambda v: v[i].reshape(1, -1)
        h1, qa, kva, qb, kb, vb = _stage1(
            h, row(ffn1_pre_g), ffn1_w_gu[i], ffn1_w_down[i],
            row(ffn1_post_g), row(attn_pre_g), w_in[i], row(b_in), batch, seq)
        mix_a = _attn_a(qa, kva, bias[0], sinks[i])
        outs_b, lses_b = [], []
        for pat in range(len(DILATIONS)):
            o, lse = _attn_b(qb[pat], kb[pat], vb[pat], bias[1 + pat])
            outs_b.append(o)
            lses_b.append(lse)
        h = _stage3(
            h1, mix_a, outs_b, lses_b, p[i].reshape(batch * seq, -1),
            w_out[i], row(b_out), row(attn_post_g), row(ffn2_pre_g),
            ffn2_w_gu[i], ffn2_w_down[i], row(ffn2_post_g),
            row(ple_pre_g), w_ple_gate[i], w_ple_proj[i],
            row(ple_post_g), seq)
    return h.reshape(batch, seq, d_model)
```

```python
import functools
import math

import numpy as np
import jax
import jax.numpy as jnp
from jax import lax
from jax.experimental import pallas as pl
from jax.experimental.pallas import tpu as pltpu

HEAD_DIM = 64
A_Q_HEADS = 8
A_KV_HEADS = 2
A_WINDOW = 128
B_HEADS = 8
B_PATTERNS = ((128, 1), (512, 4), (2048, 16))
DILATIONS = tuple(d for _, d in B_PATTERNS)
NUM_BUCKETS = 32
MAX_DISTANCE = 2048
EPS = 1e-6
NEG_INF = -1e30

ATTN_BLOCK = 128
LANE_BLOCK = 256
HEADS_PER_LANE_BLOCK = LANE_BLOCK // HEAD_DIM
SLAB = 128
SLABS_PER_LANE_BLOCK = LANE_BLOCK // SLAB
STAGING_SLOTS = 2
FF_CHUNK = 256
TOKEN_TILE = 512
ROW_CHUNK = 16
WEIGHT_STAGE_ROWS = 256
ATTN_ROWS = 1024
VMEM_LIMIT_TOKEN_STAGES = 58 * 1024 * 1024
VMEM_LIMIT_ATTENTION = 40 * 1024 * 1024

F32 = jnp.float32
BF16 = jnp.bfloat16


def _rms(x, g):
    return x * lax.rsqrt(jnp.mean(x * x, axis=-1, keepdims=True) + EPS) * g


def _row_chunks(n_rows):
    return [slice(r, r + ROW_CHUNK) for r in range(0, n_rows, ROW_CHUNK)]


def _sigmoid(x):
    return 1.0 / (1.0 + jnp.exp(-x))


def _swiglu(xn_ref, wgu_ref, wd_ref, act_ref, side_work=None):
    d_ff = wd_ref.shape[0]
    side_work = dict(side_work or {})
    for c in range(d_ff // FF_CHUNK):
        lo = c * FF_CHUNK
        g = jnp.dot(xn_ref[...], wgu_ref[:, lo:lo + FF_CHUNK], preferred_element_type=F32)
        u = jnp.dot(xn_ref[...], wgu_ref[:, d_ff + lo:d_ff + lo + FF_CHUNK],
                    preferred_element_type=F32)
        act_ref[:, lo:lo + FF_CHUNK] = ((g * _sigmoid(g)) * u).astype(BF16)
        if c in side_work:
            side_work.pop(c)()
    assert not side_work
    return jnp.dot(act_ref[...], wd_ref[...], preferred_element_type=F32)


def _load_as_bf16(src_hbm, dst_ref, stage_ref, sem_ref):
    s_rows, s_cols = stage_ref.shape[1:]
    rows, cols = src_hbm.shape
    pieces = [(r, min(s_rows, rows - r), c, min(s_cols, cols - c))
              for r in range(0, rows, s_rows) for c in range(0, cols, s_cols)]

    def copy(i):
        r, nr, c, nc = pieces[i]
        return pltpu.make_async_copy(src_hbm.at[pl.ds(r, nr), pl.ds(c, nc)],
                                     stage_ref.at[i % 2, pl.ds(0, nr), pl.ds(0, nc)],
                                     sem_ref.at[i % 2])

    copy(0).start()
    for i, (r, nr, c, nc) in enumerate(pieces):
        if i + 1 < len(pieces):
            copy(i + 1).start()
        copy(i).wait()
        dst_ref[r:r + nr, c:c + nc] = stage_ref[i % 2, :nr, :nc].astype(BF16)


def _const_spec(shape):
    return pl.BlockSpec(shape, lambda *_: (0,) * len(shape), pipeline_mode=pl.Buffered(1))


def _t5_bucket_np(dist):
    max_exact = NUM_BUCKETS // 2
    n = np.maximum(dist, 0)
    nf = np.maximum(n, 1).astype(np.float32)
    large = max_exact + (np.log(nf / np.float32(max_exact))
                         / np.float32(math.log(MAX_DISTANCE / max_exact))
                         * np.float32(NUM_BUCKETS - max_exact)).astype(np.int32)
    large = np.minimum(large, NUM_BUCKETS - 1)
    return np.where(n < max_exact, n, large).astype(np.int32)


def _bucket_tables():
    q = np.arange(ATTN_BLOCK)[:, None]
    k = np.arange(2 * ATTN_BLOCK)[None, :]
    rel = ATTN_BLOCK + q - k
    tables = []
    for max_dist, stride in ((A_WINDOW - 1, 1),) + tuple((w // d, d) for w, d in B_PATTERNS):
        valid = (rel >= 0) & (rel <= max_dist)
        bkt = _t5_bucket_np(rel * stride)
        later = np.where(valid, bkt, -1)
        first = np.where(valid & (k >= ATTN_BLOCK), bkt, -1)
        tables.append(np.stack([first, later]))
    return np.stack(tables).astype(np.int32)


def _bias_kernel(buckets_present, rb_ref, bkt_ref, out_ref):
    head = pl.program_id(0)
    first_block_keys = lax.broadcasted_iota(jnp.int32, bkt_ref.shape[2:], 1) >= ATTN_BLOCK
    for pattern, present in enumerate(buckets_present):
        col = head if pattern == 0 else head + A_Q_HEADS
        bkt = bkt_ref[pattern, 1]
        later = jnp.full(bkt.shape, NEG_INF, F32)
        for b in present:
            later = jnp.where(bkt == b, rb_ref[b, col], later)
        out_ref[pattern, 1] = later
        out_ref[pattern, 0] = jnp.where(first_block_keys, later, NEG_INF)


def _bias_tables(rel_bias):
    bkt_np = _bucket_tables()
    assert (bkt_np[:, 0] == np.where(np.arange(2 * ATTN_BLOCK) >= ATTN_BLOCK,
                                     bkt_np[:, 1], -1)).all()
    buckets_present = tuple(tuple(int(b) for b in np.unique(t[1]) if b >= 0) for t in bkt_np)
    n_pat = bkt_np.shape[0]
    table = (2, ATTN_BLOCK, 2 * ATTN_BLOCK)
    out = pl.pallas_call(
        functools.partial(_bias_kernel, buckets_present),
        grid=(A_Q_HEADS,),
        in_specs=[
            pl.BlockSpec(memory_space=pltpu.SMEM),
            pl.BlockSpec((n_pat,) + table, lambda h: (0, 0, 0, 0)),
        ],
        out_specs=pl.BlockSpec((n_pat, None) + table, lambda h: (0, h, 0, 0, 0)),
        out_shape=jax.ShapeDtypeStruct((n_pat, A_Q_HEADS) + table, F32),
        compiler_params=pltpu.CompilerParams(dimension_semantics=("arbitrary",)),
        name="bias_tables",
    )(rel_bias, jnp.asarray(bkt_np))
    return out.reshape(n_pat, A_Q_HEADS * 2, ATTN_BLOCK, 2 * ATTN_BLOCK)


def _stage1_kernel(x_ref, xprev_ref, g_pre_ref, wgu_hbm, wd_hbm, g_post_ref, g_attn_ref, win_hbm,
                   bin_ref, h_ref, qa_ref, kva_ref, *rest):
    n_dil = len(DILATIONS)
    qb_refs, kb_refs, vb_refs = rest[:n_dil], rest[n_dil:2 * n_dil], rest[2 * n_dil:3 * n_dil]
    (xn_ref, act_ref, f_ref, xn2_ref, slab_ref, slab2_ref,
     wgu_ref, wd_ref, win_ref, sem_ref) = rest[3 * n_dil:]
    tm = x_ref.shape[0]
    step = pl.program_id(0)
    rd_slot = (step + 1) % 2
    wr_slot = step % 2

    @pl.when(step == 0)
    def _():
        _load_as_bf16(wgu_hbm, wgu_ref, f_ref, sem_ref)
        _load_as_bf16(wd_hbm, wd_ref, f_ref, sem_ref)
        _load_as_bf16(win_hbm, win_ref, f_ref, sem_ref)
        f_ref[1] = jnp.zeros(f_ref.shape[1:], F32)

    for rows in _row_chunks(tm):
        xn_ref[rows] = _rms(x_ref[rows], g_pre_ref[...]).astype(BF16)
    for rows in _row_chunks(tm):
        h = xprev_ref[rows] + 0.5 * _rms(f_ref[rd_slot, rows], g_post_ref[...])
        h_ref[rows] = h
        xn2_ref[rows] = _rms(h, g_attn_ref[...]).astype(BF16)

    def proj(col_block):
        lo = col_block * LANE_BLOCK
        return (jnp.dot(xn2_ref[...], win_ref[:, lo:lo + LANE_BLOCK], preferred_element_type=F32)
                + bin_ref[:, lo:lo + LANE_BLOCK])

    def gather_rows(src_ref, base, start, n, stride):
        return [src_ref[base + s, pl.ds(start, n, stride=stride), :]
                for s in range(SLABS_PER_LANE_BLOCK)]

    def write_classes(z, refs, jb, slot):
        ref1, ref4, ref16 = refs
        base = slot * SLABS_PER_LANE_BLOCK
        ref1[jb, 0] = z.astype(BF16)
        for s in range(SLABS_PER_LANE_BLOCK):
            slab_ref[base + s] = z[:, s * SLAB:(s + 1) * SLAB]
        n4 = tm // 4
        for c4 in range(4):
            pieces = gather_rows(slab_ref, base, c4, n4, 4)
            ref4[jb, c4] = jnp.concatenate(pieces, axis=1).astype(BF16)
            for s in range(SLABS_PER_LANE_BLOCK):
                slab2_ref[base + s, c4 * n4:(c4 + 1) * n4, :] = pieces[s]
        for c4 in range(4):
            for j in range(4):
                pieces = gather_rows(slab2_ref, base, c4 * n4 + j, n4 // 4, 4)
                ref16[jb, c4 + 4 * j] = jnp.concatenate(pieces, axis=1).astype(BF16)

    q_scale = HEAD_DIM ** -0.5
    n_slots = slab_ref.shape[0] // SLABS_PER_LANE_BLOCK

    def store_kva():
        kva_ref[...] = proj(2).astype(BF16)

    def store_qa(j):
        qa_ref[j] = (proj(j) * q_scale).astype(BF16)

    def store_classes(col_block, scale, refs, jb, slot):
        z = proj(col_block)
        write_classes(z if scale == 1.0 else z * scale, refs, jb, slot % n_slots)

    projections = [store_kva]
    for j in range(2):
        projections += [
            functools.partial(store_qa, j),
            functools.partial(store_classes, 3 + j, q_scale, qb_refs, j, 3 * j),
            functools.partial(store_classes, 5 + j, 1.0, kb_refs, j, 3 * j + 1),
            functools.partial(store_classes, 7 + j, 1.0, vb_refs, j, 3 * j + 2),
        ]
    side_work = {1 + c: work for c, work in enumerate(projections)}
    f_ref[wr_slot] = _swiglu(xn_ref, wgu_ref, wd_ref, act_ref, side_work)


def _stage1(x2, g_pre, wgu, wd, g_post, g_attn, w_in, b_in, batch, seq):
    tokens, d_model = x2.shape
    tm = TOKEN_TILE
    tiles_per_seq = seq // tm
    d_ff = wd.shape[0]
    n_blk = 2
    n_dil = len(DILATIONS)
    n_tiles = tokens // tm
    assert DILATIONS == (1, 4, 16) and tm % (16 * 16) == 0

    def lagged(i):
        j = jnp.maximum(i - 1, 0)
        return j // tiles_per_seq, j % tiles_per_seq

    def class_spec(dil):
        return pl.BlockSpec((None, n_blk, dil, tm // dil, LANE_BLOCK),
                            lambda i: (lagged(i)[0], 0, 0, lagged(i)[1], 0))

    def class_shape(dil):
        return jax.ShapeDtypeStruct((batch, n_blk, dil, seq // dil, LANE_BLOCK), BF16)

    row_spec = pl.BlockSpec((tm, d_model), lambda i: (jnp.minimum(i, n_tiles - 1), 0))
    lagged_row_spec = pl.BlockSpec((tm, d_model), lambda i: (jnp.maximum(i - 1, 0), 0))
    hbm_spec = pl.BlockSpec(memory_space=pl.ANY)
    outs = pl.pallas_call(
        _stage1_kernel,
        grid=(n_tiles + 1,),
        in_specs=[
            row_spec,
            lagged_row_spec,
            _const_spec((1, d_model)),
            hbm_spec,
            hbm_spec,
            _const_spec((1, d_model)),
            _const_spec((1, d_model)),
            hbm_spec,
            _const_spec(b_in.shape),
        ],
        out_specs=[
            lagged_row_spec,
            pl.BlockSpec((None, n_blk, tm, LANE_BLOCK),
                         lambda i: (lagged(i)[0], 0, lagged(i)[1], 0)),
            pl.BlockSpec((None, tm, LANE_BLOCK),
                         lambda i: (lagged(i)[0], lagged(i)[1], 0)),
        ] + [class_spec(d) for _ in range(3) for d in DILATIONS],
        out_shape=[
            jax.ShapeDtypeStruct((tokens, d_model), F32),
            jax.ShapeDtypeStruct((batch, n_blk, seq, LANE_BLOCK), BF16),
            jax.ShapeDtypeStruct((batch, seq, LANE_BLOCK), BF16),
        ] + [class_shape(d) for _ in range(3) for d in DILATIONS],
        scratch_shapes=[
            pltpu.VMEM((tm, d_model), BF16),
            pltpu.VMEM((tm, d_ff), BF16),
            pltpu.VMEM((2, tm, d_model), F32),
            pltpu.VMEM((tm, d_model), BF16),
            pltpu.VMEM((STAGING_SLOTS * SLABS_PER_LANE_BLOCK, tm, SLAB), F32),
            pltpu.VMEM((STAGING_SLOTS * SLABS_PER_LANE_BLOCK, tm, SLAB), F32),
            pltpu.VMEM(wgu.shape, BF16),
            pltpu.VMEM(wd.shape, BF16),
            pltpu.VMEM(w_in.shape, BF16),
            pltpu.SemaphoreType.DMA((2,)),
        ],
        compiler_params=pltpu.CompilerParams(dimension_semantics=("arbitrary",),
                                             vmem_limit_bytes=VMEM_LIMIT_TOKEN_STAGES),
        name="ffn1_inproj",
    )(x2, x2, g_pre, wgu, wd, g_post, g_attn, w_in, b_in)
    h1, qa, kva = outs[:3]
    qb, kb, vb = outs[3:3 + n_dil], outs[3 + n_dil:3 + 2 * n_dil], outs[3 + 2 * n_dil:]
    return h1, qa, kva, qb, kb, vb


def _lane_is_low(shape):
    return lax.broadcasted_iota(jnp.int32, shape, 1) < HEAD_DIM


def _masked_kv(k_grp, v_grp):
    low = _lane_is_low(k_grp.shape)
    zero = jnp.zeros_like(k_grp)
    lane = lax.broadcasted_iota(jnp.int32, k_grp.shape, 1)
    ones_low = jnp.clip(HEAD_DIM - lane, 0, 1).astype(F32).astype(BF16)
    ones_high = jnp.clip(lane - (HEAD_DIM - 1), 0, 1).astype(F32).astype(BF16)
    k_par = (jnp.where(low, k_grp, zero), jnp.where(low, zero, k_grp))
    v_par = (jnp.concatenate([jnp.where(low, v_grp, zero), ones_low], axis=1),
             jnp.concatenate([jnp.where(low, zero, v_grp), ones_high], axis=1))
    return k_par, v_par


def _pair_attention(q, k_par, v_par, biases):
    pair = 2 * HEAD_DIM
    acc = None
    maxes = []
    for par in range(2):
        s = lax.dot_general(q, k_par[par], (((1,), (1,)), ((), ())),
                            preferred_element_type=F32) + biases[par]
        m = jnp.max(s, axis=-1, keepdims=True)
        p = jnp.exp(s - m).astype(BF16)
        pv = jnp.dot(p, v_par[par], preferred_element_type=F32)
        acc = pv if acc is None else acc + pv
        maxes.append(m)
    m_pair = jnp.where(_lane_is_low((q.shape[0], pair)), maxes[0], maxes[1])
    return acc[:, :pair], acc[:, pair:], m_pair


def _attn_b_kernel(q_ref, kp_ref, kc_ref, vp_ref, vc_ref, bias_ref, o_ref, lse_ref):
    first_chunk = pl.program_id(2) == 0
    n_blk, n_cls, rows, _ = q_ref.shape
    pair = 2 * HEAD_DIM
    for cls in range(n_cls):
        for jb in range(n_blk):
            k_all = jnp.concatenate([kp_ref[jb, cls], kc_ref[jb, cls]], axis=0)
            v_all = jnp.concatenate([vp_ref[jb, cls], vc_ref[jb, cls]], axis=0)
            for grp in range(LANE_BLOCK // pair):
                lanes = slice(grp * pair, (grp + 1) * pair)
                k_par, v_par = _masked_kv(k_all[:, lanes], v_all[:, lanes])
                for t in range(rows // ATTN_BLOCK):
                    q = q_ref[jb, cls, t * ATTN_BLOCK:(t + 1) * ATTN_BLOCK, lanes]
                    keys = slice(t * ATTN_BLOCK, (t + 2) * ATTN_BLOCK)
                    variant = jnp.where(first_chunk, 0, 1) if t == 0 else 1
                    head = jb * HEADS_PER_LANE_BLOCK + grp * 2
                    biases = [bias_ref[(head + par) * 2 + variant] for par in range(2)]
                    o, den, m = _pair_attention(q, [k[keys] for k in k_par],
                                                [v[keys] for v in v_par], biases)
                    out_lanes = slice(jb * LANE_BLOCK + grp * pair,
                                      jb * LANE_BLOCK + (grp + 1) * pair)
                    out_rows = slice(t * ATTN_BLOCK, (t + 1) * ATTN_BLOCK)
                    o_ref[cls, out_rows, out_lanes] = (o / den).astype(BF16)
                    lse_ref[cls, out_rows, out_lanes] = m + jnp.log(den)


def _attn_b(qb, kb, vb, bias):
    batch, n_blk, n_cls, class_len, _ = qb.shape
    rows = min(ATTN_ROWS, class_len)
    cls_per_step = min(n_cls, ATTN_ROWS // rows)
    prev_per_chunk = rows // ATTN_BLOCK

    cur_spec = pl.BlockSpec((None, n_blk, cls_per_step, rows, LANE_BLOCK),
                            lambda b, c, i: (b, 0, c, i, 0))
    prev_spec = pl.BlockSpec(
        (None, n_blk, cls_per_step, ATTN_BLOCK, LANE_BLOCK),
        lambda b, c, i: (b, 0, c, jnp.maximum(i * prev_per_chunk - 1, 0), 0))
    width = n_blk * LANE_BLOCK
    out_spec = pl.BlockSpec((None, cls_per_step, rows, width), lambda b, c, i: (b, c, i, 0))
    return pl.pallas_call(
        _attn_b_kernel,
        grid=(batch, n_cls // cls_per_step, class_len // rows),
        in_specs=[cur_spec, prev_spec, cur_spec, prev_spec, cur_spec,
                  _const_spec(bias.shape)],
        out_specs=[out_spec, out_spec],
        out_shape=[jax.ShapeDtypeStruct((batch, n_cls, class_len, width), BF16),
                   jax.ShapeDtypeStruct((batch, n_cls, class_len, width), F32)],
        compiler_params=pltpu.CompilerParams(
            dimension_semantics=("arbitrary", "arbitrary", "arbitrary"),
            vmem_limit_bytes=VMEM_LIMIT_ATTENTION),
        name=f"mixer_b_dil{n_cls}",
    )(qb, kb, kb, vb, vb, bias)


def _swap_halves(x):
    return jnp.concatenate([x[:, HEAD_DIM:], x[:, :HEAD_DIM]], axis=1)


def _attn_a_kernel(sink_ref, q_ref, kvp_ref, kvc_ref, bias_ref, o_ref):
    first_chunk = pl.program_id(1) == 0
    rows = q_ref.shape[1]
    pair = 2 * HEAD_DIM
    low_out = _lane_is_low((ATTN_BLOCK, pair))
    kv_all = jnp.concatenate([kvp_ref[...], kvc_ref[...]], axis=0)
    k_nat, v_nat = kv_all[:, :pair], kv_all[:, pair:]

    def both_halves(x, g):
        half = x[:, g * HEAD_DIM:(g + 1) * HEAD_DIM]
        return jnp.concatenate([half, half], axis=1)

    for g in range(A_KV_HEADS):
        k_par, v_par = _masked_kv(both_halves(k_nat, g), both_halves(v_nat, g))
        for grp in range(LANE_BLOCK // pair):
            lanes = slice(grp * pair, (grp + 1) * pair)
            head = g * HEADS_PER_LANE_BLOCK + grp * 2
            sink = jnp.where(low_out, sink_ref[head], sink_ref[head + 1])
            for t in range(rows // ATTN_BLOCK):
                q = q_ref[g, t * ATTN_BLOCK:(t + 1) * ATTN_BLOCK, lanes]
                keys = slice(t * ATTN_BLOCK, (t + 2) * ATTN_BLOCK)
                variant = jnp.where(first_chunk, 0, 1) if t == 0 else 1
                biases = [bias_ref[(head + par) * 2 + variant] for par in range(2)]
                o, den, m = _pair_attention(q, [k[keys] for k in k_par],
                                            [v[keys] for v in v_par], biases)
                total = den + jnp.exp(sink - m)
                out_lanes = slice(g * LANE_BLOCK + grp * pair, g * LANE_BLOCK + (grp + 1) * pair)
                o_ref[t * ATTN_BLOCK:(t + 1) * ATTN_BLOCK, out_lanes] = (o / total).astype(BF16)


def _attn_a(qa, kva, bias, sinks):
    batch, n_blk, seq, _ = qa.shape
    rows = min(ATTN_ROWS, seq)
    prev_per_chunk = rows // ATTN_BLOCK
    width = n_blk * LANE_BLOCK
    out = pl.pallas_call(
        _attn_a_kernel,
        grid=(batch, seq // rows),
        in_specs=[
            pl.BlockSpec(memory_space=pltpu.SMEM),
            pl.BlockSpec((None, n_blk, rows, LANE_BLOCK), lambda b, i: (b, 0, i, 0)),
            pl.BlockSpec((None, ATTN_BLOCK, LANE_BLOCK),
                         lambda b, i: (b, jnp.maximum(i * prev_per_chunk - 1, 0), 0)),
            pl.BlockSpec((None, rows, LANE_BLOCK), lambda b, i: (b, i, 0)),
            _const_spec(bias.shape),
        ],
        out_specs=pl.BlockSpec((None, rows, width), lambda b, i: (b, i, 0)),
        out_shape=jax.ShapeDtypeStruct((batch, seq, width), BF16),
        compiler_params=pltpu.CompilerParams(
            dimension_semantics=("arbitrary", "arbitrary"),
            vmem_limit_bytes=VMEM_LIMIT_ATTENTION),
        name="mixer_a",
    )(sinks, qa, kva, kva, bias)
    return out.reshape(batch * seq, width)


def _stage3_kernel(*refs):
    n_dil = len(DILATIONS)
    h_ref, mixa_ref = refs[:2]
    o_refs = refs[2:2 + n_dil]
    l_refs = refs[2 + n_dil:2 + 2 * n_dil]
    (p_ref, wout_hbm, bout_ref, g_attn_post_ref, g_pre_ref, wgu_hbm, wd_hbm, g_post_ref,
     g_ple_pre_ref, wgate_hbm, wproj_hbm, g_ple_post_ref,
     out_ref, xn_ref, act_ref, mixb_ref, slab_ref, att_ref, f_ref,
     wout_ref, wgu_ref, wd_ref, wgate_ref, wproj_ref, stage_ref, sem_ref) = refs[2 + 2 * n_dil:]
    tm = h_ref.shape[0]
    n_slabs = mixa_ref.shape[1] // SLAB

    @pl.when(pl.program_id(0) == 0)
    def _():
        for src, dst in ((wout_hbm, wout_ref), (wgu_hbm, wgu_ref), (wd_hbm, wd_ref),
                         (wgate_hbm, wgate_ref), (wproj_hbm, wproj_ref)):
            _load_as_bf16(src, dst, stage_ref, sem_ref)

    d_a = mixa_ref.shape[1]
    half = tm // 2
    quarter = half // 4

    def sub_chunks(r0, n):
        return [slice(r, r + ROW_CHUNK) for r in range(r0, r0 + n, ROW_CHUNK)]

    def interleave_merge(s, r0):
        lanes = slice(s * SLAB, (s + 1) * SLAB)
        for k, dil in enumerate(d for d in DILATIONS if d != 1):
            idx = DILATIONS.index(dil)
            n = half // dil
            sub = min(n, 2 * ROW_CHUNK)
            for src_ref, which in ((o_refs[idx], 0), (l_refs[idx], 1)):
                dst = (2 * k + which) * n_slabs + s
                for c in range(dil):
                    for u0 in range(r0 // dil, r0 // dil + n, sub):
                        blk = src_ref[c, u0:u0 + sub, lanes].astype(F32)
                        slab_ref[dst, pl.ds(c + u0 * dil, sub, stride=dil), :] = blk
        for q0 in range(r0, r0 + half, 2 * ROW_CHUNK):
            rows = slice(q0, q0 + 2 * ROW_CHUNK)
            outs, lses = [], []
            k = 0
            for idx, dil in enumerate(DILATIONS):
                if dil == 1:
                    outs.append(o_refs[idx][0, rows, lanes].astype(F32))
                    lses.append(l_refs[idx][0, rows, lanes])
                else:
                    outs.append(slab_ref[2 * k * n_slabs + s, rows, :])
                    lses.append(slab_ref[(2 * k + 1) * n_slabs + s, rows, :])
                    k += 1
            l_max = lses[0]
            for l in lses[1:]:
                l_max = jnp.maximum(l_max, l)
            weights = [jnp.exp(l - l_max) for l in lses]
            num = weights[0] * outs[0]
            den = weights[0]
            for w, o in zip(weights[1:], outs[1:]):
                num = num + w * o
                den = den + w
            mixb_ref[rows, lanes] = (num / den).astype(BF16)

    def out_projection(part, r0):
        rows = slice(r0, r0 + half)
        if part == 0:
            att_ref[rows] = (jnp.dot(mixa_ref[rows], wout_ref[:d_a, :], preferred_element_type=F32)
                             + bout_ref[...])
        else:
            lo = (part - 1) * 2 * SLAB
            hi = lo + 2 * SLAB
            att_ref[rows] = att_ref[rows] + jnp.dot(
                mixb_ref[rows, lo:hi], wout_ref[d_a + lo:d_a + hi, :], preferred_element_type=F32)

    def attention_residual(r0, n):
        for rows in sub_chunks(r0, n):
            h2 = h_ref[rows] + _rms(att_ref[rows], g_attn_post_ref[...])
            out_ref[rows] = h2
            xn_ref[rows] = _rms(h2, g_pre_ref[...]).astype(BF16)

    def head_pieces(r0):
        return [functools.partial(interleave_merge, 0, r0),
                functools.partial(interleave_merge, 1, r0),
                functools.partial(out_projection, 0, r0),
                functools.partial(out_projection, 1, r0),
                functools.partial(interleave_merge, 2, r0),
                functools.partial(interleave_merge, 3, r0),
                functools.partial(out_projection, 2, r0)] + [
                    functools.partial(attention_residual, r0 + j * quarter, quarter)
                    for j in range(4)]

    def ffn_residual(r0, n):
        for rows in sub_chunks(r0, n):
            h3 = out_ref[rows] + 0.5 * _rms(f_ref[rows], g_post_ref[...])
            out_ref[rows] = h3
            xn_ref[rows] = _rms(h3, g_ple_pre_ref[...]).astype(BF16)

    def gate(r0):
        rows = slice(r0, r0 + half)
        att_ref[rows] = _sigmoid(
            jnp.dot(xn_ref[rows], wgate_ref[...], preferred_element_type=F32))

    def embedding(r0):
        rows = slice(r0, r0 + half)
        att_ref[rows] = att_ref[rows] * jnp.dot(
            p_ref[rows].astype(BF16), wproj_ref[...], preferred_element_type=F32)

    def embedding_residual(r0, n):
        for rows in sub_chunks(r0, n):
            out_ref[rows] = out_ref[rows] + _rms(att_ref[rows], g_ple_post_ref[...])

    def tail_pieces(r0):
        return ([functools.partial(ffn_residual, r0 + j * quarter, quarter) for j in range(4)]
                + [functools.partial(gate, r0), functools.partial(embedding, r0)]
                + [functools.partial(embedding_residual, r0 + j * quarter, quarter)
                   for j in range(4)])

    def ffn(r0, pieces):
        rows = slice(r0, r0 + half)
        f_ref[rows] = _swiglu(xn_ref.at[rows], wgu_ref, wd_ref, act_ref, dict(enumerate(pieces)))

    for piece in head_pieces(0):
        piece()
    ffn(0, head_pieces(half))
    ffn(half, tail_pieces(0))
    for piece in tail_pieces(half):
        piece()


def _stage3(h1, mix_a, outs_b, lses_b, p2, w_out, b_out, g_attn_post, g_pre, wgu, wd, g_post,
            g_ple_pre, w_gate, w_proj, g_ple_post, seq):
    tokens, d_model = h1.shape
    tm = TOKEN_TILE
    tiles_per_seq = seq // tm
    d_ff = wd.shape[0]
    half = mix_a.shape[1]
    n_dil = len(DILATIONS)

    def rows(width):
        return pl.BlockSpec((tm, width), lambda i: (i, 0))

    def class_spec(dil):
        return pl.BlockSpec((None, dil, tm // dil, half),
                            lambda i: (i // tiles_per_seq, 0, i % tiles_per_seq, 0))

    vec = _const_spec((1, d_model))
    hbm = pl.BlockSpec(memory_space=pl.ANY)
    return pl.pallas_call(
        _stage3_kernel,
        grid=(tokens // tm,),
        in_specs=[rows(d_model), rows(half)]
        + [class_spec(d) for d in DILATIONS] * 2
        + [
            rows(p2.shape[1]),
            hbm, vec, vec, vec,
            hbm, hbm, vec, vec,
            hbm, hbm, vec,
        ],
        out_specs=rows(d_model),
        out_shape=jax.ShapeDtypeStruct((tokens, d_model), F32),
        scratch_shapes=[
            pltpu.VMEM((tm, d_model), BF16),
            pltpu.VMEM((tm // 2, d_ff), BF16),
            pltpu.VMEM((tm, half), BF16),
            pltpu.VMEM((2 * (n_dil - 1) * (half // SLAB), tm, SLAB), F32),
            pltpu.VMEM((tm, d_model), F32),
            pltpu.VMEM((tm, d_model), F32),
            pltpu.VMEM(w_out.shape, BF16),
            pltpu.VMEM(wgu.shape, BF16),
            pltpu.VMEM(wd.shape, BF16),
            pltpu.VMEM(w_gate.shape, BF16),
            pltpu.VMEM(w_proj.shape, BF16),
            pltpu.VMEM((2, WEIGHT_STAGE_ROWS, d_model), F32),
            pltpu.SemaphoreType.DMA((2,)),
        ],
        compiler_params=pltpu.CompilerParams(dimension_semantics=("arbitrary",),
                                             vmem_limit_bytes=VMEM_LIMIT_TOKEN_STAGES),
        name="outproj_ffn2_ple",
    )(h1, mix_a, *outs_b, *lses_b, p2, w_out, b_out, g_attn_post, g_pre, wgu, wd, g_post,
      g_ple_pre, w_gate, w_proj, g_ple_post)


def kernel(x, p, rel_bias, ffn1_pre_g, ffn1_w_gu, ffn1_w_down, ffn1_post_g, attn_pre_g, w_in, b_in,
           sinks, w_out, b_out, attn_post_g, ffn2_pre_g, ffn2_w_gu, ffn2_w_down, ffn2_post_g,
           ple_pre_g, w_ple_gate, w_ple_proj, ple_post_g):
    batch, seq, d_model = x.shape
    depth = p.shape[0]
    assert seq % (max(DILATIONS) * ATTN_BLOCK) == 0 and seq % TOKEN_TILE == 0
    assert x.dtype == F32

    bias = _bias_tables(rel_bias)
    h = x.reshape(batch * seq, d_model)
    for i in range(depth):
        row = lambda v: v[i].reshape(1, -1)
        h1, qa, kva, qb, kb, vb = _stage1(
            h, row(ffn1_pre_g), ffn1_w_gu[i], ffn1_w_down[i],
            row(ffn1_post_g), row(attn_pre_g), w_in[i], row(b_in), batch, seq)
        mix_a = _attn_a(qa, kva, bias[0], sinks[i])
        outs_b, lses_b = [], []
        for pat in range(len(DILATIONS)):
            o, lse = _attn_b(qb[pat], kb[pat], vb[pat], bias[1 + pat])
            outs_b.append(o)
            lses_b.append(lse)
        h = _stage3(
            h1, mix_a, outs_b, lses_b, p[i].reshape(batch * seq, -1),
            w_out[i], row(b_out), row(attn_post_g), row(ffn2_pre_g),
            ffn2_w_gu[i], ffn2_w_down[i], row(ffn2_post_g),
            row(ple_pre_g), w_ple_gate[i], w_ple_proj[i],
            row(ple_post_g), seq)
    return h.reshape(batch, seq, d_model)
```

```python
import functools
import math

import numpy as np
import jax
import jax.numpy as jnp
from jax import lax
from jax.experimental import pallas as pl
from jax.experimental.pallas import tpu as pltpu

HEAD_DIM = 64
A_Q_HEADS = 8
A_KV_HEADS = 2
A_WINDOW = 128
B_HEADS = 8
B_PATTERNS = ((128, 1), (512, 4), (2048, 16))
DILATIONS = tuple(d for _, d in B_PATTERNS)
NUM_BUCKETS = 32
MAX_DISTANCE = 2048
EPS = 1e-6
NEG_INF = -1e30

ATTN_BLOCK = 128
LANE_BLOCK = 256
HEADS_PER_LANE_BLOCK = LANE_BLOCK // HEAD_DIM
SLAB = 128
SLABS_PER_LANE_BLOCK = LANE_BLOCK // SLAB
STAGING_SLOTS = 2
FF_CHUNK = 256
TOKEN_TILE = 512
ROW_CHUNK = 16
WEIGHT_STAGE_ROWS = 256
ATTN_ROWS = 1024
VMEM_LIMIT_TOKEN_STAGES = 58 * 1024 * 1024
VMEM_LIMIT_ATTENTION = 40 * 1024 * 1024

F32 = jnp.float32
BF16 = jnp.bfloat16


def _rms(x, g):
    return x * lax.rsqrt(jnp.mean(x * x, axis=-1, keepdims=True) + EPS) * g


def _row_chunks(n_rows):
    return [slice(r, r + ROW_CHUNK) for r in range(0, n_rows, ROW_CHUNK)]


def _sigmoid(x):
    return 1.0 / (1.0 + jnp.exp(-x))


def _swiglu(xn_ref, wgu_ref, wd_ref, act_ref, side_work=None):
    d_ff = wd_ref.shape[0]
    side_work = dict(side_work or {})
    for c in range(d_ff // FF_CHUNK):
        lo = c * FF_CHUNK
        g = jnp.dot(xn_ref[...], wgu_ref[:, lo:lo + FF_CHUNK], preferred_element_type=F32)
        u = jnp.dot(xn_ref[...], wgu_ref[:, d_ff + lo:d_ff + lo + FF_CHUNK],
                    preferred_element_type=F32)
        act_ref[:, lo:lo + FF_CHUNK] = ((g * _sigmoid(g)) * u).astype(BF16)
        if c in side_work:
            side_work.pop(c)()
    assert not side_work
    return jnp.dot(act_ref[...], wd_ref[...], preferred_element_type=F32)


def _load_as_bf16(src_hbm, dst_ref, stage_ref, sem_ref):
    s_rows, s_cols = stage_ref.shape[1:]
    rows, cols = src_hbm.shape
    pieces = [(r, min(s_rows, rows - r), c, min(s_cols, cols - c))
              for r in range(0, rows, s_rows) for c in range(0, cols, s_cols)]

    def copy(i):
        r, nr, c, nc = pieces[i]
        return pltpu.make_async_copy(src_hbm.at[pl.ds(r, nr), pl.ds(c, nc)],
                                     stage_ref.at[i % 2, pl.ds(0, nr), pl.ds(0, nc)],
                                     sem_ref.at[i % 2])

    copy(0).start()
    for i, (r, nr, c, nc) in enumerate(pieces):
        if i + 1 < len(pieces):
            copy(i + 1).start()
        copy(i).wait()
        dst_ref[r:r + nr, c:c + nc] = stage_ref[i % 2, :nr, :nc].astype(BF16)


def _const_spec(shape):
    return pl.BlockSpec(shape, lambda *_: (0,) * len(shape), pipeline_mode=pl.Buffered(1))


def _t5_bucket_np(dist):
    max_exact = NUM_BUCKETS // 2
    n = np.maximum(dist, 0)
    nf = np.maximum(n, 1).astype(np.float32)
    large = max_exact + (np.log(nf / np.float32(max_exact))
                         / np.float32(math.log(MAX_DISTANCE / max_exact))
                         * np.float32(NUM_BUCKETS - max_exact)).astype(np.int32)
    large = np.minimum(large, NUM_BUCKETS - 1)
    return np.where(n < max_exact, n, large).astype(np.int32)


def _bucket_tables():
    q = np.arange(ATTN_BLOCK)[:, None]
    k = np.arange(2 * ATTN_BLOCK)[None, :]
    rel = ATTN_BLOCK + q - k
    tables = []
    for max_dist, stride in ((A_WINDOW - 1, 1),) + tuple((w // d, d) for w, d in B_PATTERNS):
        valid = (rel >= 0) & (rel <= max_dist)
        bkt = _t5_bucket_np(rel * stride)
        later = np.where(valid, bkt, -1)
        first = np.where(valid & (k >= ATTN_BLOCK), bkt, -1)
        tables.append(np.stack([first, later]))
    return np.stack(tables).astype(np.int32)


def _bias_kernel(buckets_present, rb_ref, bkt_ref, out_ref):
    head = pl.program_id(0)
    first_block_keys = lax.broadcasted_iota(jnp.int32, bkt_ref.shape[2:], 1) >= ATTN_BLOCK
    for pattern, present in enumerate(buckets_present):
        col = head if pattern == 0 else head + A_Q_HEADS
        bkt = bkt_ref[pattern, 1]
        later = jnp.full(bkt.shape, NEG_INF, F32)
        for b in present:
            later = jnp.where(bkt == b, rb_ref[b, col], later)
        out_ref[pattern, 1] = later
        out_ref[pattern, 0] = jnp.where(first_block_keys, later, NEG_INF)


def _bias_tables(rel_bias):
    bkt_np = _bucket_tables()
    assert (bkt_np[:, 0] == np.where(np.arange(2 * ATTN_BLOCK) >= ATTN_BLOCK,
                                     bkt_np[:, 1], -1)).all()
    buckets_present = tuple(tuple(int(b) for b in np.unique(t[1]) if b >= 0) for t in bkt_np)
    n_pat = bkt_np.shape[0]
    table = (2, ATTN_BLOCK, 2 * ATTN_BLOCK)
    out = pl.pallas_call(
        functools.partial(_bias_kernel, buckets_present),
        grid=(A_Q_HEADS,),
        in_specs=[
            pl.BlockSpec(memory_space=pltpu.SMEM),
            pl.BlockSpec((n_pat,) + table, lambda h: (0, 0, 0, 0)),
        ],
        out_specs=pl.BlockSpec((n_pat, None) + table, lambda h: (0, h, 0, 0, 0)),
        out_shape=jax.ShapeDtypeStruct((n_pat, A_Q_HEADS) + table, F32),
        compiler_params=pltpu.CompilerParams(dimension_semantics=("arbitrary",)),
        name="bias_tables",
    )(rel_bias, jnp.asarray(bkt_np))
    return out.reshape(n_pat, A_Q_HEADS * 2, ATTN_BLOCK, 2 * ATTN_BLOCK)


def _stage1_kernel(x_ref, xprev_ref, g_pre_ref, wgu_hbm, wd_hbm, g_post_ref, g_attn_ref, win_hbm,
                   bin_ref, h_ref, qa_ref, kva_ref, *rest):
    n_dil = len(DILATIONS)
    qb_refs, kb_refs, vb_refs = rest[:n_dil], rest[n_dil:2 * n_dil], rest[2 * n_dil:3 * n_dil]
    (xn_ref, act_ref, f_ref, xn2_ref, slab_ref, slab2_ref,
     wgu_ref, wd_ref, win_ref, sem_ref) = rest[3 * n_dil:]
    tm = x_ref.shape[0]
    step = pl.program_id(0)
    rd_slot = (step + 1) % 2
    wr_slot = step % 2

    @pl.when(step == 0)
    def _():
        _load_as_bf16(wgu_hbm, wgu_ref, f_ref, sem_ref)
        _load_as_bf16(wd_hbm, wd_ref, f_ref, sem_ref)
        _load_as_bf16(win_hbm, win_ref, f_ref, sem_ref)
        f_ref[1] = jnp.zeros(f_ref.shape[1:], F32)

    for rows in _row_chunks(tm):
        xn_ref[rows] = _rms(x_ref[rows], g_pre_ref[...]).astype(BF16)
    for rows in _row_chunks(tm):
        h = xprev_ref[rows] + 0.5 * _rms(f_ref[rd_slot, rows], g_post_ref[...])
        h_ref[rows] = h
        xn2_ref[rows] = _rms(h, g_attn_ref[...]).astype(BF16)

    def proj(col_block):
        lo = col_block * LANE_BLOCK
        return (jnp.dot(xn2_ref[...], win_ref[:, lo:lo + LANE_BLOCK], preferred_element_type=F32)
                + bin_ref[:, lo:lo + LANE_BLOCK])

    def gather_rows(src_ref, base, start, n, stride):
        return [src_ref[base + s, pl.ds(start, n, stride=stride), :]
                for s in range(SLABS_PER_LANE_BLOCK)]

    def write_classes(z, refs, jb, slot):
        ref1, ref4, ref16 = refs
        base = slot * SLABS_PER_LANE_BLOCK
        ref1[jb, 0] = z.astype(BF16)
        for s in range(SLABS_PER_LANE_BLOCK):
            slab_ref[base + s] = z[:, s * SLAB:(s + 1) * SLAB]
        n4 = tm // 4
        for c4 in range(4):
            pieces = gather_rows(slab_ref, base, c4, n4, 4)
            ref4[jb, c4] = jnp.concatenate(pieces, axis=1).astype(BF16)
            for s in range(SLABS_PER_LANE_BLOCK):
                slab2_ref[base + s, c4 * n4:(c4 + 1) * n4, :] = pieces[s]
        for c4 in range(4):
            for j in range(4):
                pieces = gather_rows(slab2_ref, base, c4 * n4 + j, n4 // 4, 4)
                ref16[jb, c4 + 4 * j] = jnp.concatenate(pieces, axis=1).astype(BF16)

    q_scale = HEAD_DIM ** -0.5
    n_slots = slab_ref.shape[0] // SLABS_PER_LANE_BLOCK

    def store_kva():
        kva_ref[...] = proj(2).astype(BF16)

    def store_qa(j):
        qa_ref[j] = (proj(j) * q_scale).astype(BF16)

    def store_classes(col_block, scale, refs, jb, slot):
        z = proj(col_block)
        write_classes(z if scale == 1.0 else z * scale, refs, jb, slot % n_slots)

    projections = [store_kva]
    for j in range(2):
        projections += [
            functools.partial(store_qa, j),
            functools.partial(store_classes, 3 + j, q_scale, qb_refs, j, 3 * j),
            functools.partial(store_classes, 5 + j, 1.0, kb_refs, j, 3 * j + 1),
            functools.partial(store_classes, 7 + j, 1.0, vb_refs, j, 3 * j + 2),
        ]
    side_work = {1 + c: work for c, work in enumerate(projections)}
    f_ref[wr_slot] = _swiglu(xn_ref, wgu_ref, wd_ref, act_ref, side_work)


def _stage1(x2, g_pre, wgu, wd, g_post, g_attn, w_in, b_in, batch, seq):
    tokens, d_model = x2.shape
    tm = TOKEN_TILE
    tiles_per_seq = seq // tm
    d_ff = wd.shape[0]
    n_blk = 2
    n_dil = len(DILATIONS)
    n_tiles = tokens // tm
    assert DILATIONS == (1, 4, 16) and tm % (16 * 16) == 0

    def lagged(i):
        j = jnp.maximum(i - 1, 0)
        return j // tiles_per_seq, j % tiles_per_seq

    def class_spec(dil):
        return pl.BlockSpec((None, n_blk, dil, tm // dil, LANE_BLOCK),
                            lambda i: (lagged(i)[0], 0, 0, lagged(i)[1], 0))

    def class_shape(dil):
        return jax.ShapeDtypeStruct((batch, n_blk, dil, seq // dil, LANE_BLOCK), BF16)

    row_spec = pl.BlockSpec((tm, d_model), lambda i: (jnp.minimum(i, n_tiles - 1), 0))
    lagged_row_spec = pl.BlockSpec((tm, d_model), lambda i: (jnp.maximum(i - 1, 0), 0))
    hbm_spec = pl.BlockSpec(memory_space=pl.ANY)
    outs = pl.pallas_call(
        _stage1_kernel,
        grid=(n_tiles + 1,),
        in_specs=[
            row_spec,
            lagged_row_spec,
            _const_spec((1, d_model)),
            hbm_spec,
            hbm_spec,
            _const_spec((1, d_model)),
            _const_spec((1, d_model)),
            hbm_spec,
            _const_spec(b_in.shape),
        ],
        out_specs=[
            lagged_row_spec,
            pl.BlockSpec((None, n_blk, tm, LANE_BLOCK),
                         lambda i: (lagged(i)[0], 0, lagged(i)[1], 0)),
            pl.BlockSpec((None, tm, LANE_BLOCK),
                         lambda i: (lagged(i)[0], lagged(i)[1], 0)),
        ] + [class_spec(d) for _ in range(3) for d in DILATIONS],
        out_shape=[
            jax.ShapeDtypeStruct((tokens, d_model), F32),
            jax.ShapeDtypeStruct((batch, n_blk, seq, LANE_BLOCK), BF16),
            jax.ShapeDtypeStruct((batch, seq, LANE_BLOCK), BF16),
        ] + [class_shape(d) for _ in range(3) for d in DILATIONS],
        scratch_shapes=[
            pltpu.VMEM((tm, d_model), BF16),
            pltpu.VMEM((tm, d_ff), BF16),
            pltpu.VMEM((2, tm, d_model), F32),
            pltpu.VMEM((tm, d_model), BF16),
            pltpu.VMEM((STAGING_SLOTS * SLABS_PER_LANE_BLOCK, tm, SLAB), F32),
            pltpu.VMEM((STAGING_SLOTS * SLABS_PER_LANE_BLOCK, tm, SLAB), F32),
            pltpu.VMEM(wgu.shape, BF16),
            pltpu.VMEM(wd.shape, BF16),
            pltpu.VMEM(w_in.shape, BF16),
            pltpu.SemaphoreType.DMA((2,)),
        ],
        compiler_params=pltpu.CompilerParams(dimension_semantics=("arbitrary",),
                                             vmem_limit_bytes=VMEM_LIMIT_TOKEN_STAGES),
        name="ffn1_inproj",
    )(x2, x2, g_pre, wgu, wd, g_post, g_attn, w_in, b_in)
    h1, qa, kva = outs[:3]
    qb, kb, vb = outs[3:3 + n_dil], outs[3 + n_dil:3 + 2 * n_dil], outs[3 + 2 * n_dil:]
    return h1, qa, kva, qb, kb, vb


def _lane_is_low(shape):
    return lax.broadcasted_iota(jnp.int32, shape, 1) < HEAD_DIM


def _masked_kv(k_grp, v_grp):
    low = _lane_is_low(k_grp.shape)
    zero = jnp.zeros_like(k_grp)
    lane = lax.broadcasted_iota(jnp.int32, k_grp.shape, 1)
    ones_low = jnp.clip(HEAD_DIM - lane, 0, 1).astype(F32).astype(BF16)
    ones_high = jnp.clip(lane - (HEAD_DIM - 1), 0, 1).astype(F32).astype(BF16)
    k_par = (jnp.where(low, k_grp, zero), jnp.where(low, zero, k_grp))
    v_par = (jnp.concatenate([jnp.where(low, v_grp, zero), ones_low], axis=1),
             jnp.concatenate([jnp.where(low, zero, v_grp), ones_high], axis=1))
    return k_par, v_par


def _pair_attention(q, k_par, v_par, biases):
    pair = 2 * HEAD_DIM
    acc = None
    maxes = []
    for par in range(2):
        s = lax.dot_general(q, k_par[par], (((1,), (1,)), ((), ())),
                            preferred_element_type=F32) + biases[par]
        m = jnp.max(s, axis=-1, keepdims=True)
        p = jnp.exp(s - m).astype(BF16)
        pv = jnp.dot(p, v_par[par], preferred_element_type=F32)
        acc = pv if acc is None else acc + pv
        maxes.append(m)
    m_pair = jnp.where(_lane_is_low((q.shape[0], pair)), maxes[0], maxes[1])
    return acc[:, :pair], acc[:, pair:], m_pair


def _attn_b_kernel(q_ref, kp_ref, kc_ref, vp_ref, vc_ref, bias_ref, o_ref, lse_ref):
    first_chunk = pl.program_id(2) == 0
    n_blk, n_cls, rows, _ = q_ref.shape
    pair = 2 * HEAD_DIM
    for cls in range(n_cls):
        for jb in range(n_blk):
            k_all = jnp.concatenate([kp_ref[jb, cls], kc_ref[jb, cls]], axis=0)
            v_all = jnp.concatenate([vp_ref[jb, cls], vc_ref[jb, cls]], axis=0)
            for grp in range(LANE_BLOCK // pair):
                lanes = slice(grp * pair, (grp + 1) * pair)
                k_par, v_par = _masked_kv(k_all[:, lanes], v_all[:, lanes])
                for t in range(rows // ATTN_BLOCK):
                    q = q_ref[jb, cls, t * ATTN_BLOCK:(t + 1) * ATTN_BLOCK, lanes]
                    keys = slice(t * ATTN_BLOCK, (t + 2) * ATTN_BLOCK)
                    variant = jnp.where(first_chunk, 0, 1) if t == 0 else 1
                    head = jb * HEADS_PER_LANE_BLOCK + grp * 2
                    biases = [bias_ref[(head + par) * 2 + variant] for par in range(2)]
                    o, den, m = _pair_attention(q, [k[keys] for k in k_par],
                                                [v[keys] for v in v_par], biases)
                    out_lanes = slice(jb * LANE_BLOCK + grp * pair,
                                      jb * LANE_BLOCK + (grp + 1) * pair)
                    out_rows = slice(t * ATTN_BLOCK, (t + 1) * ATTN_BLOCK)
                    o_ref[cls, out_rows, out_lanes] = (o / den).astype(BF16)
                    lse_ref[cls, out_rows, out_lanes] = m + jnp.log(den)


def _attn_b(qb, kb, vb, bias):
    batch, n_blk, n_cls, class_len, _ = qb.shape
    rows = min(ATTN_ROWS, class_len)
    cls_per_step = min(n_cls, ATTN_ROWS // rows)
    prev_per_chunk = rows // ATTN_BLOCK

    cur_spec = pl.BlockSpec((None, n_blk, cls_per_step, rows, LANE_BLOCK),
                            lambda b, c, i: (b, 0, c, i, 0))
    prev_spec = pl.BlockSpec(
        (None, n_blk, cls_per_step, ATTN_BLOCK, LANE_BLOCK),
        lambda b, c, i: (b, 0, c, jnp.maximum(i * prev_per_chunk - 1, 0), 0))
    width = n_blk * LANE_BLOCK
    out_spec = pl.BlockSpec((None, cls_per_step, rows, width), lambda b, c, i: (b, c, i, 0))
    return pl.pallas_call(
        _attn_b_kernel,
        grid=(batch, n_cls // cls_per_step, class_len // rows),
        in_specs=[cur_spec, prev_spec, cur_spec, prev_spec, cur_spec,
                  _const_spec(bias.shape)],
        out_specs=[out_spec, out_spec],
        out_shape=[jax.ShapeDtypeStruct((batch, n_cls, class_len, width), BF16),
                   jax.ShapeDtypeStruct((batch, n_cls, class_len, width), F32)],
        compiler_params=pltpu.CompilerParams(
            dimension_semantics=("arbitrary", "arbitrary", "arbitrary"),
            vmem_limit_bytes=VMEM_LIMIT_ATTENTION),
        name=f"mixer_b_dil{n_cls}",
    )(qb, kb, kb, vb, vb, bias)


def _swap_halves(x):
    return jnp.concatenate([x[:, HEAD_DIM:], x[:, :HEAD_DIM]], axis=1)


def _attn_a_kernel(sink_ref, q_ref, kvp_ref, kvc_ref, bias_ref, o_ref):
    first_chunk = pl.program_id(1) == 0
    rows = q_ref.shape[1]
    pair = 2 * HEAD_DIM
    low_out = _lane_is_low((ATTN_BLOCK, pair))
    kv_all = jnp.concatenate([kvp_ref[...], kvc_ref[...]], axis=0)
    k_nat, v_nat = kv_all[:, :pair], kv_all[:, pair:]

    def both_halves(x, g):
        half = x[:, g * HEAD_DIM:(g + 1) * HEAD_DIM]
        return jnp.concatenate([half, half], axis=1)

    for g in range(A_KV_HEADS):
        k_par, v_par = _masked_kv(both_halves(k_nat, g), both_halves(v_nat, g))
        for grp in range(LANE_BLOCK // pair):
            lanes = slice(grp * pair, (grp + 1) * pair)
            head = g * HEADS_PER_LANE_BLOCK + grp * 2
            sink = jnp.where(low_out, sink_ref[head], sink_ref[head + 1])
            for t in range(rows // ATTN_BLOCK):
                q = q_ref[g, t * ATTN_BLOCK:(t + 1) * ATTN_BLOCK, lanes]
                keys = slice(t * ATTN_BLOCK, (t + 2) * ATTN_BLOCK)
                variant = jnp.where(first_chunk, 0, 1) if t == 0 else 1
                biases = [bias_ref[(head + par) * 2 + variant] for par in range(2)]
                o, den, m = _pair_attention(q, [k[keys] for k in k_par],
                                            [v[keys] for v in v_par], biases)
                total = den + jnp.exp(sink - m)
                out_lanes = slice(g * LANE_BLOCK + grp * pair, g * LANE_BLOCK + (grp + 1) * pair)
                o_ref[t * ATTN_BLOCK:(t + 1) * ATTN_BLOCK, out_lanes] = (o / total).astype(BF16)


def _attn_a(qa, kva, bias, sinks):
    batch, n_blk, seq, _ = qa.shape
    rows = min(ATTN_ROWS, seq)
    prev_per_chunk = rows // ATTN_BLOCK
    width = n_blk * LANE_BLOCK
    out = pl.pallas_call(
        _attn_a_kernel,
        grid=(batch, seq // rows),
        in_specs=[
            pl.BlockSpec(memory_space=pltpu.SMEM),
            pl.BlockSpec((None, n_blk, rows, LANE_BLOCK), lambda b, i: (b, 0, i, 0)),
            pl.BlockSpec((None, ATTN_BLOCK, LANE_BLOCK),
                         lambda b, i: (b, jnp.maximum(i * prev_per_chunk - 1, 0), 0)),
            pl.BlockSpec((None, rows, LANE_BLOCK), lambda b, i: (b, i, 0)),
            _const_spec(bias.shape),
        ],
        out_specs=pl.BlockSpec((None, rows, width), lambda b, i: (b, i, 0)),
        out_shape=jax.ShapeDtypeStruct((batch, seq, width), BF16),
        compiler_params=pltpu.CompilerParams(
            dimension_semantics=("arbitrary", "arbitrary"),
            vmem_limit_bytes=VMEM_LIMIT_ATTENTION),
        name="mixer_a",
    )(sinks, qa, kva, kva, bias)
    return out.reshape(batch * seq, width)


def _stage3_kernel(*refs):
    n_dil = len(DILATIONS)
    h_ref, mixa_ref = refs[:2]
    o_refs = refs[2:2 + n_dil]
    l_refs = refs[2 + n_dil:2 + 2 * n_dil]
    (p_ref, wout_hbm, bout_ref, g_attn_post_ref, g_pre_ref, wgu_hbm, wd_hbm, g_post_ref,
     g_ple_pre_ref, wgate_hbm, wproj_hbm, g_ple_post_ref,
     out_ref, xn_ref, act_ref, mixb_ref, slab_ref, mid_ref, att_ref, f_ref,
     wout_ref, wgu_ref, wd_ref, wgate_ref, wproj_ref, stage_ref, sem_ref) = refs[2 + 2 * n_dil:]
    tm = h_ref.shape[0]
    n_slabs = mixa_ref.shape[1] // SLAB

    @pl.when(pl.program_id(0) == 0)
    def _():
        for src, dst in ((wout_hbm, wout_ref), (wgu_hbm, wgu_ref), (wd_hbm, wd_ref),
                         (wgate_hbm, wgate_ref), (wproj_hbm, wproj_ref)):
            _load_as_bf16(src, dst, stage_ref, sem_ref)

    d_a = mixa_ref.shape[1]
    half = tm // 2
    quarter = half // 4

    def sub_chunks(r0, n):
        return [slice(r, r + ROW_CHUNK) for r in range(r0, r0 + n, ROW_CHUNK)]

    def interleave_merge(s, r0):
        lanes = slice(s * SLAB, (s + 1) * SLAB)
        for k, dil in enumerate(d for d in DILATIONS if d != 1):
            idx = DILATIONS.index(dil)
            n = half // dil
            for src_ref, which in ((o_refs[idx], 0), (l_refs[idx], 1)):
                dst = (2 * k + which) * n_slabs + s
                if dil == 4:
                    sub = 2 * ROW_CHUNK
                    for c in range(dil):
                        for u0 in range(r0 // dil, r0 // dil + n, sub):
                            blk = src_ref[c, u0:u0 + sub, lanes].astype(F32)
                            slab_ref[dst, pl.ds(c + u0 * dil, sub, stride=dil), :] = blk
                else:
                    mid = which * n_slabs + s
                    n4 = half // 4
                    for c4 in range(4):
                        for j in range(4):
                            blk = src_ref[c4 + 4 * j, r0 // dil:r0 // dil + n, lanes].astype(F32)
                            mid_ref[mid, pl.ds(c4 * n4 + j, n, stride=4), :] = blk
                        for w0 in range(0, n4, 2 * ROW_CHUNK):
                            piece = mid_ref[mid, c4 * n4 + w0:c4 * n4 + w0 + 2 * ROW_CHUNK, :]
                            slab_ref[dst, pl.ds(r0 + c4 + 4 * w0, 2 * ROW_CHUNK, stride=4), :] = piece
        for q0 in range(r0, r0 + half, 2 * ROW_CHUNK):
            rows = slice(q0, q0 + 2 * ROW_CHUNK)
            outs, lses = [], []
            k = 0
            for idx, dil in enumerate(DILATIONS):
                if dil == 1:
                    outs.append(o_refs[idx][0, rows, lanes].astype(F32))
                    lses.append(l_refs[idx][0, rows, lanes])
                else:
                    outs.append(slab_ref[2 * k * n_slabs + s, rows, :])
                    lses.append(slab_ref[(2 * k + 1) * n_slabs + s, rows, :])
                    k += 1
            l_max = lses[0]
            for l in lses[1:]:
                l_max = jnp.maximum(l_max, l)
            weights = [jnp.exp(l - l_max) for l in lses]
            num = weights[0] * outs[0]
            den = weights[0]
            for w, o in zip(weights[1:], outs[1:]):
                num = num + w * o
                den = den + w
            mixb_ref[rows, lanes] = (num / den).astype(BF16)

    def out_projection(part, r0):
        rows = slice(r0, r0 + half)
        if part == 0:
            att_ref[rows] = (jnp.dot(mixa_ref[rows], wout_ref[:d_a, :], preferred_element_type=F32)
                             + bout_ref[...])
        else:
            lo = (part - 1) * 2 * SLAB
            hi = lo + 2 * SLAB
            att_ref[rows] = att_ref[rows] + jnp.dot(
                mixb_ref[rows, lo:hi], wout_ref[d_a + lo:d_a + hi, :], preferred_element_type=F32)

    def attention_residual(r0, n):
        for rows in sub_chunks(r0, n):
            h2 = h_ref[rows] + _rms(att_ref[rows], g_attn_post_ref[...])
            out_ref[rows] = h2
            xn_ref[rows] = _rms(h2, g_pre_ref[...]).astype(BF16)

    def head_pieces(r0):
        return [functools.partial(interleave_merge, 0, r0),
                functools.partial(interleave_merge, 1, r0),
                functools.partial(out_projection, 0, r0),
                functools.partial(out_projection, 1, r0),
                functools.partial(interleave_merge, 2, r0),
                functools.partial(interleave_merge, 3, r0),
                functools.partial(out_projection, 2, r0)] + [
                    functools.partial(attention_residual, r0 + j * quarter, quarter)
                    for j in range(4)]

    def ffn_residual(r0, n):
        for rows in sub_chunks(r0, n):
            h3 = out_ref[rows] + 0.5 * _rms(f_ref[rows], g_post_ref[...])
            out_ref[rows] = h3
            xn_ref[rows] = _rms(h3, g_ple_pre_ref[...]).astype(BF16)

    def gate(r0):
        rows = slice(r0, r0 + half)
        att_ref[rows] = _sigmoid(
            jnp.dot(xn_ref[rows], wgate_ref[...], preferred_element_type=F32))

    def embedding(r0):
        rows = slice(r0, r0 + half)
        att_ref[rows] = att_ref[rows] * jnp.dot(
            p_ref[rows].astype(BF16), wproj_ref[...], preferred_element_type=F32)

    def embedding_residual(r0, n):
        for rows in sub_chunks(r0, n):
            out_ref[rows] = out_ref[rows] + _rms(att_ref[rows], g_ple_post_ref[...])

    def tail_pieces(r0):
        return ([functools.partial(ffn_residual, r0 + j * quarter, quarter) for j in range(4)]
                + [functools.partial(gate, r0), functools.partial(embedding, r0)]
                + [functools.partial(embedding_residual, r0 + j * quarter, quarter)
                   for j in range(4)])

    def ffn(r0, pieces):
        rows = slice(r0, r0 + half)
        f_ref[rows] = _swiglu(xn_ref.at[rows], wgu_ref, wd_ref, act_ref, dict(enumerate(pieces)))

    for piece in head_pieces(0):
        piece()
    ffn(0, head_pieces(half))
    ffn(half, tail_pieces(0))
    for piece in tail_pieces(half):
        piece()


def _stage3(h1, mix_a, outs_b, lses_b, p2, w_out, b_out, g_attn_post, g_pre, wgu, wd, g_post,
            g_ple_pre, w_gate, w_proj, g_ple_post, seq):
    tokens, d_model = h1.shape
    tm = TOKEN_TILE
    tiles_per_seq = seq // tm
    d_ff = wd.shape[0]
    half = mix_a.shape[1]
    n_dil = len(DILATIONS)

    def rows(width):
        return pl.BlockSpec((tm, width), lambda i: (i, 0))

    def class_spec(dil):
        return pl.BlockSpec((None, dil, tm // dil, half),
                            lambda i: (i // tiles_per_seq, 0, i % tiles_per_seq, 0))

    vec = _const_spec((1, d_model))
    hbm = pl.BlockSpec(memory_space=pl.ANY)
    return pl.pallas_call(
        _stage3_kernel,
        grid=(tokens // tm,),
        in_specs=[rows(d_model), rows(half)]
        + [class_spec(d) for d in DILATIONS] * 2
        + [
            rows(p2.shape[1]),
            hbm, vec, vec, vec,
            hbm, hbm, vec, vec,
            hbm, hbm, vec,
        ],
        out_specs=rows(d_model),
        out_shape=jax.ShapeDtypeStruct((tokens, d_model), F32),
        scratch_shapes=[
            pltpu.VMEM((tm, d_model), BF16),
            pltpu.VMEM((tm // 2, d_ff), BF16),
            pltpu.VMEM((tm, half), BF16),
            pltpu.VMEM((2 * (n_dil - 1) * (half // SLAB), tm, SLAB), F32),
            pltpu.VMEM((2 * (half // SLAB), tm // 2, SLAB), F32),
            pltpu.VMEM((tm, d_model), F32),
            pltpu.VMEM((tm, d_model), F32),
            pltpu.VMEM(w_out.shape, BF16),
            pltpu.VMEM(wgu.shape, BF16),
            pltpu.VMEM(wd.shape, BF16),
            pltpu.VMEM(w_gate.shape, BF16),
            pltpu.VMEM(w_proj.shape, BF16),
            pltpu.VMEM((2, WEIGHT_STAGE_ROWS, d_model), F32),
            pltpu.SemaphoreType.DMA((2,)),
        ],
        compiler_params=pltpu.CompilerParams(dimension_semantics=("arbitrary",),
                                             vmem_limit_bytes=VMEM_LIMIT_TOKEN_STAGES),
        name="outproj_ffn2_ple",
    )(h1, mix_a, *outs_b, *lses_b, p2, w_out, b_out, g_attn_post, g_pre, wgu, wd, g_post,
      g_ple_pre, w_gate, w_proj, g_ple_post)


def kernel(x, p, rel_bias, ffn1_pre_g, ffn1_w_gu, ffn1_w_down, ffn1_post_g, attn_pre_g, w_in, b_in,
           sinks, w_out, b_out, attn_post_g, ffn2_pre_g, ffn2_w_gu, ffn2_w_down, ffn2_post_g,
           ple_pre_g, w_ple_gate, w_ple_proj, ple_post_g):
    batch, seq, d_model = x.shape
    depth = p.shape[0]
    assert seq % (max(DILATIONS) * ATTN_BLOCK) == 0 and seq % TOKEN_TILE == 0
    assert x.dtype == F32

    bias = _bias_tables(rel_bias)
    h = x.reshape(batch * seq, d_model)
    for i in range(depth):
        row = lambda v: v[i].reshape(1, -1)
        h1, qa, kva, qb, kb, vb = _stage1(
            h, row(ffn1_pre_g), ffn1_w_gu[i], ffn1_w_down[i],
            row(ffn1_post_g), row(attn_pre_g), w_in[i], row(b_in), batch, seq)
        mix_a = _attn_a(qa, kva, bias[0], sinks[i])
        outs_b, lses_b = [], []
        for pat in range(len(DILATIONS)):
            o, lse = _attn_b(qb[pat], kb[pat], vb[pat], bias[1 + pat])
            outs_b.append(o)
            lses_b.append(lse)
        h = _stage3(
            h1, mix_a, outs_b, lses_b, p[i].reshape(batch * seq, -1),
            w_out[i], row(b_out), row(attn_post_g), row(ffn2_pre_g),
            ffn2_w_gu[i], ffn2_w_down[i], row(ffn2_post_g),
            row(ple_pre_g), w_ple_gate[i], w_ple_proj[i],
            row(ple_post_g), seq)
    return h.reshape(batch, seq, d_model)
```

```python
import functools
import math

import numpy as np
import jax
import jax.numpy as jnp
from jax import lax
from jax.experimental import pallas as pl
from jax.experimental.pallas import tpu as pltpu

HEAD_DIM = 64
A_Q_HEADS = 8
A_KV_HEADS = 2
A_WINDOW = 128
B_HEADS = 8
B_PATTERNS = ((128, 1), (512, 4), (2048, 16))
DILATIONS = tuple(d for _, d in B_PATTERNS)
NUM_BUCKETS = 32
MAX_DISTANCE = 2048
EPS = 1e-6
NEG_INF = -1e30

ATTN_BLOCK = 128
LANE_BLOCK = 256
HEADS_PER_LANE_BLOCK = LANE_BLOCK // HEAD_DIM
SLAB = 128
SLABS_PER_LANE_BLOCK = LANE_BLOCK // SLAB
STAGING_SLOTS = 2
FF_CHUNK = 256
TOKEN_TILE = 512
ROW_CHUNK = 16
WEIGHT_STAGE_ROWS = 512
ATTN_ROWS = 1024
VMEM_LIMIT_TOKEN_STAGES = 58 * 1024 * 1024
VMEM_LIMIT_ATTENTION = 40 * 1024 * 1024

F32 = jnp.float32
BF16 = jnp.bfloat16


def _rms(x, g):
    return x * lax.rsqrt(jnp.mean(x * x, axis=-1, keepdims=True) + EPS) * g


def _row_chunks(n_rows):
    return [slice(r, r + ROW_CHUNK) for r in range(0, n_rows, ROW_CHUNK)]


def _sigmoid(x):
    return 1.0 / (1.0 + jnp.exp(-x))


def _swiglu(xn_ref, wgu_ref, wd_ref, act_ref, side_work=None):
    d_ff = wd_ref.shape[0]
    side_work = dict(side_work or {})
    for c in range(d_ff // FF_CHUNK):
        lo = c * FF_CHUNK
        g = jnp.dot(xn_ref[...], wgu_ref[:, lo:lo + FF_CHUNK], preferred_element_type=F32)
        u = jnp.dot(xn_ref[...], wgu_ref[:, d_ff + lo:d_ff + lo + FF_CHUNK],
                    preferred_element_type=F32)
        act_ref[:, lo:lo + FF_CHUNK] = ((g * _sigmoid(g)) * u).astype(BF16)
        if c in side_work:
            side_work.pop(c)()
    assert not side_work
    return jnp.dot(act_ref[...], wd_ref[...], preferred_element_type=F32)


def _load_as_bf16(src_hbm, dst_ref, stage_ref, sem_ref):
    s_rows, s_cols = stage_ref.shape[1:]
    rows, cols = src_hbm.shape
    pieces = [(r, min(s_rows, rows - r), c, min(s_cols, cols - c))
              for r in range(0, rows, s_rows) for c in range(0, cols, s_cols)]

    def copy(i):
        r, nr, c, nc = pieces[i]
        return pltpu.make_async_copy(src_hbm.at[pl.ds(r, nr), pl.ds(c, nc)],
                                     stage_ref.at[i % 2, pl.ds(0, nr), pl.ds(0, nc)],
                                     sem_ref.at[i % 2])

    copy(0).start()
    for i, (r, nr, c, nc) in enumerate(pieces):
        if i + 1 < len(pieces):
            copy(i + 1).start()
        copy(i).wait()
        dst_ref[r:r + nr, c:c + nc] = stage_ref[i % 2, :nr, :nc].astype(BF16)


def _const_spec(shape):
    return pl.BlockSpec(shape, lambda *_: (0,) * len(shape), pipeline_mode=pl.Buffered(1))


def _t5_bucket_np(dist):
    max_exact = NUM_BUCKETS // 2
    n = np.maximum(dist, 0)
    nf = np.maximum(n, 1).astype(np.float32)
    large = max_exact + (np.log(nf / np.float32(max_exact))
                         / np.float32(math.log(MAX_DISTANCE / max_exact))
                         * np.float32(NUM_BUCKETS - max_exact)).astype(np.int32)
    large = np.minimum(large, NUM_BUCKETS - 1)
    return np.where(n < max_exact, n, large).astype(np.int32)


def _bucket_tables():
    q = np.arange(ATTN_BLOCK)[:, None]
    k = np.arange(2 * ATTN_BLOCK)[None, :]
    rel = ATTN_BLOCK + q - k
    tables = []
    for max_dist, stride in ((A_WINDOW - 1, 1),) + tuple((w // d, d) for w, d in B_PATTERNS):
        valid = (rel >= 0) & (rel <= max_dist)
        bkt = _t5_bucket_np(rel * stride)
        later = np.where(valid, bkt, -1)
        first = np.where(valid & (k >= ATTN_BLOCK), bkt, -1)
        tables.append(np.stack([first, later]))
    return np.stack(tables).astype(np.int32)


def _bias_kernel(buckets_present, rb_ref, bkt_ref, out_ref):
    head = pl.program_id(0)
    first_block_keys = lax.broadcasted_iota(jnp.int32, bkt_ref.shape[2:], 1) >= ATTN_BLOCK
    for pattern, present in enumerate(buckets_present):
        col = head if pattern == 0 else head + A_Q_HEADS
        bkt = bkt_ref[pattern, 1]
        later = jnp.full(bkt.shape, NEG_INF, F32)
        for b in present:
            later = jnp.where(bkt == b, rb_ref[b, col], later)
        out_ref[pattern, 1] = later
        out_ref[pattern, 0] = jnp.where(first_block_keys, later, NEG_INF)


def _bias_tables(rel_bias):
    bkt_np = _bucket_tables()
    assert (bkt_np[:, 0] == np.where(np.arange(2 * ATTN_BLOCK) >= ATTN_BLOCK,
                                     bkt_np[:, 1], -1)).all()
    buckets_present = tuple(tuple(int(b) for b in np.unique(t[1]) if b >= 0) for t in bkt_np)
    n_pat = bkt_np.shape[0]
    table = (2, ATTN_BLOCK, 2 * ATTN_BLOCK)
    out = pl.pallas_call(
        functools.partial(_bias_kernel, buckets_present),
        grid=(A_Q_HEADS,),
        in_specs=[
            pl.BlockSpec(memory_space=pltpu.SMEM),
            pl.BlockSpec((n_pat,) + table, lambda h: (0, 0, 0, 0)),
        ],
        out_specs=pl.BlockSpec((n_pat, None) + table, lambda h: (0, h, 0, 0, 0)),
        out_shape=jax.ShapeDtypeStruct((n_pat, A_Q_HEADS) + table, F32),
        compiler_params=pltpu.CompilerParams(dimension_semantics=("arbitrary",)),
        name="bias_tables",
    )(rel_bias, jnp.asarray(bkt_np))
    return out.reshape(n_pat, A_Q_HEADS * 2, ATTN_BLOCK, 2 * ATTN_BLOCK)


def _stage1_kernel(x_ref, xprev_ref, g_pre_ref, wgu_hbm, wd_hbm, g_post_ref, g_attn_ref, win_hbm,
                   bin_ref, h_ref, qa_ref, kva_ref, *rest):
    n_dil = len(DILATIONS)
    qb_refs, kb_refs, vb_refs = rest[:n_dil], rest[n_dil:2 * n_dil], rest[2 * n_dil:3 * n_dil]
    (xn_ref, act_ref, f_ref, xn2_ref, slab_ref, slab2_ref,
     wgu_ref, wd_ref, win_ref, sem_ref) = rest[3 * n_dil:]
    tm = x_ref.shape[0]
    step = pl.program_id(0)
    rd_slot = (step + 1) % 2
    wr_slot = step % 2

    @pl.when(step == 0)
    def _():
        _load_as_bf16(wgu_hbm, wgu_ref, f_ref, sem_ref)
        _load_as_bf16(wd_hbm, wd_ref, f_ref, sem_ref)
        _load_as_bf16(win_hbm, win_ref, f_ref, sem_ref)
        f_ref[1] = jnp.zeros(f_ref.shape[1:], F32)

    for rows in _row_chunks(tm):
        xn_ref[rows] = _rms(x_ref[rows], g_pre_ref[...]).astype(BF16)
    for rows in _row_chunks(tm):
        h = xprev_ref[rows] + 0.5 * _rms(f_ref[rd_slot, rows], g_post_ref[...])
        h_ref[rows] = h
        xn2_ref[rows] = _rms(h, g_attn_ref[...]).astype(BF16)

    def proj(col_block):
        lo = col_block * LANE_BLOCK
        return (jnp.dot(xn2_ref[...], win_ref[:, lo:lo + LANE_BLOCK], preferred_element_type=F32)
                + bin_ref[:, lo:lo + LANE_BLOCK])

    def gather_rows(src_ref, base, start, n, stride):
        return [src_ref[base + s, pl.ds(start, n, stride=stride), :]
                for s in range(SLABS_PER_LANE_BLOCK)]

    def write_classes(z, refs, jb, slot):
        ref1, ref4, ref16 = refs
        base = slot * SLABS_PER_LANE_BLOCK
        ref1[jb, 0] = z.astype(BF16)
        for s in range(SLABS_PER_LANE_BLOCK):
            slab_ref[base + s] = z[:, s * SLAB:(s + 1) * SLAB]
        n4 = tm // 4
        for c4 in range(4):
            pieces = gather_rows(slab_ref, base, c4, n4, 4)
            ref4[jb, c4] = jnp.concatenate(pieces, axis=1).astype(BF16)
            for s in range(SLABS_PER_LANE_BLOCK):
                slab2_ref[base + s, c4 * n4:(c4 + 1) * n4, :] = pieces[s]
        for c4 in range(4):
            for j in range(4):
                pieces = gather_rows(slab2_ref, base, c4 * n4 + j, n4 // 4, 4)
                ref16[jb, c4 + 4 * j] = jnp.concatenate(pieces, axis=1).astype(BF16)

    q_scale = HEAD_DIM ** -0.5
    n_slots = slab_ref.shape[0] // SLABS_PER_LANE_BLOCK

    def store_kva():
        kva_ref[...] = proj(2).astype(BF16)

    def store_qa(j):
        qa_ref[j] = (proj(j) * q_scale).astype(BF16)

    def store_classes(col_block, scale, refs, jb, slot):
        z = proj(col_block)
        write_classes(z if scale == 1.0 else z * scale, refs, jb, slot % n_slots)

    projections = [store_kva]
    for j in range(2):
        projections += [
            functools.partial(store_qa, j),
            functools.partial(store_classes, 3 + j, q_scale, qb_refs, j, 3 * j),
            functools.partial(store_classes, 5 + j, 1.0, kb_refs, j, 3 * j + 1),
            functools.partial(store_classes, 7 + j, 1.0, vb_refs, j, 3 * j + 2),
        ]
    side_work = {1 + c: work for c, work in enumerate(projections)}
    f_ref[wr_slot] = _swiglu(xn_ref, wgu_ref, wd_ref, act_ref, side_work)


def _stage1(x2, g_pre, wgu, wd, g_post, g_attn, w_in, b_in, batch, seq):
    tokens, d_model = x2.shape
    tm = TOKEN_TILE
    tiles_per_seq = seq // tm
    d_ff = wd.shape[0]
    n_blk = 2
    n_dil = len(DILATIONS)
    n_tiles = tokens // tm
    assert DILATIONS == (1, 4, 16) and tm % (16 * 16) == 0

    def lagged(i):
        j = jnp.maximum(i - 1, 0)
        return j // tiles_per_seq, j % tiles_per_seq

    def class_spec(dil):
        return pl.BlockSpec((None, n_blk, dil, tm // dil, LANE_BLOCK),
                            lambda i: (lagged(i)[0], 0, 0, lagged(i)[1], 0))

    def class_shape(dil):
        return jax.ShapeDtypeStruct((batch, n_blk, dil, seq // dil, LANE_BLOCK), BF16)

    row_spec = pl.BlockSpec((tm, d_model), lambda i: (jnp.minimum(i, n_tiles - 1), 0))
    lagged_row_spec = pl.BlockSpec((tm, d_model), lambda i: (jnp.maximum(i - 1, 0), 0))
    hbm_spec = pl.BlockSpec(memory_space=pl.ANY)
    outs = pl.pallas_call(
        _stage1_kernel,
        grid=(n_tiles + 1,),
        in_specs=[
            row_spec,
            lagged_row_spec,
            _const_spec((1, d_model)),
            hbm_spec,
            hbm_spec,
            _const_spec((1, d_model)),
            _const_spec((1, d_model)),
            hbm_spec,
            _const_spec(b_in.shape),
        ],
        out_specs=[
            lagged_row_spec,
            pl.BlockSpec((None, n_blk, tm, LANE_BLOCK),
                         lambda i: (lagged(i)[0], 0, lagged(i)[1], 0)),
            pl.BlockSpec((None, tm, LANE_BLOCK),
                         lambda i: (lagged(i)[0], lagged(i)[1], 0)),
        ] + [class_spec(d) for _ in range(3) for d in DILATIONS],
        out_shape=[
            jax.ShapeDtypeStruct((tokens, d_model), F32),
            jax.ShapeDtypeStruct((batch, n_blk, seq, LANE_BLOCK), BF16),
            jax.ShapeDtypeStruct((batch, seq, LANE_BLOCK), BF16),
        ] + [class_shape(d) for _ in range(3) for d in DILATIONS],
        scratch_shapes=[
            pltpu.VMEM((tm, d_model), BF16),
            pltpu.VMEM((tm, d_ff), BF16),
            pltpu.VMEM((2, tm, d_model), F32),
            pltpu.VMEM((tm, d_model), BF16),
            pltpu.VMEM((STAGING_SLOTS * SLABS_PER_LANE_BLOCK, tm, SLAB), F32),
            pltpu.VMEM((STAGING_SLOTS * SLABS_PER_LANE_BLOCK, tm, SLAB), F32),
            pltpu.VMEM(wgu.shape, BF16),
            pltpu.VMEM(wd.shape, BF16),
            pltpu.VMEM(w_in.shape, BF16),
            pltpu.SemaphoreType.DMA((2,)),
        ],
        compiler_params=pltpu.CompilerParams(dimension_semantics=("arbitrary",),
                                             vmem_limit_bytes=VMEM_LIMIT_TOKEN_STAGES),
        name="ffn1_inproj",
    )(x2, x2, g_pre, wgu, wd, g_post, g_attn, w_in, b_in)
    h1, qa, kva = outs[:3]
    qb, kb, vb = outs[3:3 + n_dil], outs[3 + n_dil:3 + 2 * n_dil], outs[3 + 2 * n_dil:]
    return h1, qa, kva, qb, kb, vb


def _lane_is_low(shape):
    return lax.broadcasted_iota(jnp.int32, shape, 1) < HEAD_DIM


def _masked_kv(k_grp, v_grp):
    low = _lane_is_low(k_grp.shape)
    zero = jnp.zeros_like(k_grp)
    lane = lax.broadcasted_iota(jnp.int32, k_grp.shape, 1)
    ones_low = jnp.clip(HEAD_DIM - lane, 0, 1).astype(F32).astype(BF16)
    ones_high = jnp.clip(lane - (HEAD_DIM - 1), 0, 1).astype(F32).astype(BF16)
    k_par = (jnp.where(low, k_grp, zero), jnp.where(low, zero, k_grp))
    v_par = (jnp.concatenate([jnp.where(low, v_grp, zero), ones_low], axis=1),
             jnp.concatenate([jnp.where(low, zero, v_grp), ones_high], axis=1))
    return k_par, v_par


def _pair_attention(q, k_par, v_par, biases):
    pair = 2 * HEAD_DIM
    acc = None
    maxes = []
    for par in range(2):
        s = lax.dot_general(q, k_par[par], (((1,), (1,)), ((), ())),
                            preferred_element_type=F32) + biases[par]
        m = jnp.max(s, axis=-1, keepdims=True)
        p = jnp.exp(s - m).astype(BF16)
        pv = jnp.dot(p, v_par[par], preferred_element_type=F32)
        acc = pv if acc is None else acc + pv
        maxes.append(m)
    m_pair = jnp.where(_lane_is_low((q.shape[0], pair)), maxes[0], maxes[1])
    return acc[:, :pair], acc[:, pair:], m_pair


def _attn_b_kernel(q_ref, kp_ref, kc_ref, vp_ref, vc_ref, bias_ref, o_ref, lse_ref):
    first_chunk = pl.program_id(2) == 0
    n_blk, n_cls, rows, _ = q_ref.shape
    pair = 2 * HEAD_DIM
    for cls in range(n_cls):
        for jb in range(n_blk):
            k_all = jnp.concatenate([kp_ref[jb, cls], kc_ref[jb, cls]], axis=0)
            v_all = jnp.concatenate([vp_ref[jb, cls], vc_ref[jb, cls]], axis=0)
            for grp in range(LANE_BLOCK // pair):
                lanes = slice(grp * pair, (grp + 1) * pair)
                k_par, v_par = _masked_kv(k_all[:, lanes], v_all[:, lanes])
                for t in range(rows // ATTN_BLOCK):
                    q = q_ref[jb, cls, t * ATTN_BLOCK:(t + 1) * ATTN_BLOCK, lanes]
                    keys = slice(t * ATTN_BLOCK, (t + 2) * ATTN_BLOCK)
                    variant = jnp.where(first_chunk, 0, 1) if t == 0 else 1
                    head = jb * HEADS_PER_LANE_BLOCK + grp * 2
                    biases = [bias_ref[(head + par) * 2 + variant] for par in range(2)]
                    o, den, m = _pair_attention(q, [k[keys] for k in k_par],
                                                [v[keys] for v in v_par], biases)
                    out_lanes = slice(jb * LANE_BLOCK + grp * pair,
                                      jb * LANE_BLOCK + (grp + 1) * pair)
                    out_rows = slice(t * ATTN_BLOCK, (t + 1) * ATTN_BLOCK)
                    o_ref[cls, out_rows, out_lanes] = (o / den).astype(BF16)
                    lse_ref[cls, out_rows, out_lanes] = m + jnp.log(den)


def _attn_b(qb, kb, vb, bias):
    batch, n_blk, n_cls, class_len, _ = qb.shape
    rows = min(ATTN_ROWS, class_len)
    cls_per_step = min(n_cls, ATTN_ROWS // rows)
    prev_per_chunk = rows // ATTN_BLOCK

    cur_spec = pl.BlockSpec((None, n_blk, cls_per_step, rows, LANE_BLOCK),
                            lambda b, c, i: (b, 0, c, i, 0))
    prev_spec = pl.BlockSpec(
        (None, n_blk, cls_per_step, ATTN_BLOCK, LANE_BLOCK),
        lambda b, c, i: (b, 0, c, jnp.maximum(i * prev_per_chunk - 1, 0), 0))
    width = n_blk * LANE_BLOCK
    out_spec = pl.BlockSpec((None, cls_per_step, rows, width), lambda b, c, i: (b, c, i, 0))
    return pl.pallas_call(
        _attn_b_kernel,
        grid=(batch, n_cls // cls_per_step, class_len // rows),
        in_specs=[cur_spec, prev_spec, cur_spec, prev_spec, cur_spec,
                  _const_spec(bias.shape)],
        out_specs=[out_spec, out_spec],
        out_shape=[jax.ShapeDtypeStruct((batch, n_cls, class_len, width), BF16),
                   jax.ShapeDtypeStruct((batch, n_cls, class_len, width), F32)],
        compiler_params=pltpu.CompilerParams(
            dimension_semantics=("arbitrary", "arbitrary", "arbitrary"),
            vmem_limit_bytes=VMEM_LIMIT_ATTENTION),
        name=f"mixer_b_dil{n_cls}",
    )(qb, kb, kb, vb, vb, bias)


def _swap_halves(x):
    return jnp.concatenate([x[:, HEAD_DIM:], x[:, :HEAD_DIM]], axis=1)


def _attn_a_kernel(sink_ref, q_ref, kvp_ref, kvc_ref, bias_ref, o_ref):
    first_chunk = pl.program_id(1) == 0
    rows = q_ref.shape[1]
    pair = 2 * HEAD_DIM
    low_out = _lane_is_low((ATTN_BLOCK, pair))
    kv_all = jnp.concatenate([kvp_ref[...], kvc_ref[...]], axis=0)
    k_nat, v_nat = kv_all[:, :pair], kv_all[:, pair:]

    def both_halves(x, g):
        half = x[:, g * HEAD_DIM:(g + 1) * HEAD_DIM]
        return jnp.concatenate([half, half], axis=1)

    n_grp = LANE_BLOCK // pair
    for g in range(A_KV_HEADS):
        k_par, v_par = _masked_kv(both_halves(k_nat, g), both_halves(v_nat, g))
        heads = [g * HEADS_PER_LANE_BLOCK + grp * 2 for grp in range(n_grp)]
        sink = jnp.concatenate(
            [jnp.where(low_out, sink_ref[head], sink_ref[head + 1]) for head in heads], axis=0)
        for t in range(rows // ATTN_BLOCK):
            q_rows = slice(t * ATTN_BLOCK, (t + 1) * ATTN_BLOCK)
            q = jnp.concatenate(
                [q_ref[g, q_rows, grp * pair:(grp + 1) * pair] for grp in range(n_grp)], axis=0)
            keys = slice(t * ATTN_BLOCK, (t + 2) * ATTN_BLOCK)
            variant = jnp.where(first_chunk, 0, 1) if t == 0 else 1
            biases = [jnp.concatenate([bias_ref[(head + par) * 2 + variant] for head in heads],
                                      axis=0) for par in range(2)]
            o, den, m = _pair_attention(q, [k[keys] for k in k_par],
                                        [v[keys] for v in v_par], biases)
            out = (o / (den + jnp.exp(sink - m))).astype(BF16)
            for grp in range(n_grp):
                out_lanes = slice(g * LANE_BLOCK + grp * pair, g * LANE_BLOCK + (grp + 1) * pair)
                o_ref[q_rows, out_lanes] = out[grp * ATTN_BLOCK:(grp + 1) * ATTN_BLOCK]


def _attn_a(qa, kva, bias, sinks):
    batch, n_blk, seq, _ = qa.shape
    rows = min(ATTN_ROWS, seq)
    prev_per_chunk = rows // ATTN_BLOCK
    width = n_blk * LANE_BLOCK
    out = pl.pallas_call(
        _attn_a_kernel,
        grid=(batch, seq // rows),
        in_specs=[
            pl.BlockSpec(memory_space=pltpu.SMEM),
            pl.BlockSpec((None, n_blk, rows, LANE_BLOCK), lambda b, i: (b, 0, i, 0)),
            pl.BlockSpec((None, ATTN_BLOCK, LANE_BLOCK),
                         lambda b, i: (b, jnp.maximum(i * prev_per_chunk - 1, 0), 0)),
            pl.BlockSpec((None, rows, LANE_BLOCK), lambda b, i: (b, i, 0)),
            _const_spec(bias.shape),
        ],
        out_specs=pl.BlockSpec((None, rows, width), lambda b, i: (b, i, 0)),
        out_shape=jax.ShapeDtypeStruct((batch, seq, width), BF16),
        compiler_params=pltpu.CompilerParams(
            dimension_semantics=("arbitrary", "arbitrary"),
            vmem_limit_bytes=VMEM_LIMIT_ATTENTION),
        name="mixer_a",
    )(sinks, qa, kva, kva, bias)
    return out.reshape(batch * seq, width)


def _stage3_kernel(*refs):
    n_dil = len(DILATIONS)
    h_ref, mixa_ref = refs[:2]
    o_refs = refs[2:2 + n_dil]
    l_refs = refs[2 + n_dil:2 + 2 * n_dil]
    (p_ref, wout_hbm, bout_ref, g_attn_post_ref, g_pre_ref, wgu_hbm, wd_hbm, g_post_ref,
     g_ple_pre_ref, wgate_hbm, wproj_hbm, g_ple_post_ref,
     out_ref, xn_ref, act_ref, mixb_ref, slab_ref, mid_ref, att_ref, f_ref,
     wout_ref, wgu_ref, wd_ref, wgate_ref, wproj_ref, stage_ref, sem_ref) = refs[2 + 2 * n_dil:]
    tm = h_ref.shape[0]
    n_slabs = mixa_ref.shape[1] // SLAB

    @pl.when(pl.program_id(0) == 0)
    def _():
        for src, dst in ((wout_hbm, wout_ref), (wgu_hbm, wgu_ref), (wd_hbm, wd_ref),
                         (wgate_hbm, wgate_ref), (wproj_hbm, wproj_ref)):
            _load_as_bf16(src, dst, stage_ref, sem_ref)

    d_a = mixa_ref.shape[1]
    half = tm // 2
    quarter = half // 4

    def sub_chunks(r0, n):
        return [slice(r, r + ROW_CHUNK) for r in range(r0, r0 + n, ROW_CHUNK)]

    def interleave_merge(s, r0):
        lanes = slice(s * SLAB, (s + 1) * SLAB)
        for k, dil in enumerate(d for d in DILATIONS if d != 1):
            idx = DILATIONS.index(dil)
            n = half // dil
            for src_ref, which in ((o_refs[idx], 0), (l_refs[idx], 1)):
                dst = (2 * k + which) * n_slabs + s
                if dil == 4:
                    sub = 2 * ROW_CHUNK
                    for c in range(dil):
                        for u0 in range(r0 // dil, r0 // dil + n, sub):
                            blk = src_ref[c, u0:u0 + sub, lanes].astype(F32)
                            slab_ref[dst, pl.ds(c + u0 * dil, sub, stride=dil), :] = blk
                else:
                    mid = which * n_slabs + s
                    n4 = half // 4
                    for c4 in range(4):
                        for j in range(4):
                            blk = src_ref[c4 + 4 * j, r0 // dil:r0 // dil + n, lanes].astype(F32)
                            mid_ref[mid, pl.ds(c4 * n4 + j, n, stride=4), :] = blk
                        for w0 in range(0, n4, 2 * ROW_CHUNK):
                            piece = mid_ref[mid, c4 * n4 + w0:c4 * n4 + w0 + 2 * ROW_CHUNK, :]
                            slab_ref[dst, pl.ds(r0 + c4 + 4 * w0, 2 * ROW_CHUNK, stride=4), :] = piece
        for q0 in range(r0, r0 + half, 2 * ROW_CHUNK):
            rows = slice(q0, q0 + 2 * ROW_CHUNK)
            outs, lses = [], []
            k = 0
            for idx, dil in enumerate(DILATIONS):
                if dil == 1:
                    outs.append(o_refs[idx][0, rows, lanes].astype(F32))
                    lses.append(l_refs[idx][0, rows, lanes])
                else:
                    outs.append(slab_ref[2 * k * n_slabs + s, rows, :])
                    lses.append(slab_ref[(2 * k + 1) * n_slabs + s, rows, :])
                    k += 1
            l_max = lses[0]
            for l in lses[1:]:
                l_max = jnp.maximum(l_max, l)
            weights = [jnp.exp(l - l_max) for l in lses]
            num = weights[0] * outs[0]
            den = weights[0]
            for w, o in zip(weights[1:], outs[1:]):
                num = num + w * o
                den = den + w
            mixb_ref[rows, lanes] = (num / den).astype(BF16)

    def out_projection(part, r0):
        rows = slice(r0, r0 + half)
        if part == 0:
            att_ref[rows] = (jnp.dot(mixa_ref[rows], wout_ref[:d_a, :], preferred_element_type=F32)
                             + bout_ref[...])
        else:
            lo = (part - 1) * 2 * SLAB
            hi = lo + 2 * SLAB
            att_ref[rows] = att_ref[rows] + jnp.dot(
                mixb_ref[rows, lo:hi], wout_ref[d_a + lo:d_a + hi, :], preferred_element_type=F32)

    def attention_residual(r0, n):
        for rows in sub_chunks(r0, n):
            h2 = h_ref[rows] + _rms(att_ref[rows], g_attn_post_ref[...])
            out_ref[rows] = h2
            xn_ref[rows] = _rms(h2, g_pre_ref[...]).astype(BF16)

    def head_pieces(r0):
        return [functools.partial(interleave_merge, 0, r0),
                functools.partial(interleave_merge, 1, r0),
                functools.partial(out_projection, 0, r0),
                functools.partial(out_projection, 1, r0),
                functools.partial(interleave_merge, 2, r0),
                functools.partial(interleave_merge, 3, r0),
                functools.partial(out_projection, 2, r0)] + [
                    functools.partial(attention_residual, r0 + j * quarter, quarter)
                    for j in range(4)]

    def ffn_residual(r0, n):
        for rows in sub_chunks(r0, n):
            h3 = out_ref[rows] + 0.5 * _rms(f_ref[rows], g_post_ref[...])
            out_ref[rows] = h3
            xn_ref[rows] = _rms(h3, g_ple_pre_ref[...]).astype(BF16)

    def gate(r0):
        rows = slice(r0, r0 + half)
        att_ref[rows] = _sigmoid(
            jnp.dot(xn_ref[rows], wgate_ref[...], preferred_element_type=F32))

    def embedding(r0):
        rows = slice(r0, r0 + half)
        att_ref[rows] = att_ref[rows] * jnp.dot(
            p_ref[rows].astype(BF16), wproj_ref[...], preferred_element_type=F32)

    def embedding_residual(r0, n):
        for rows in sub_chunks(r0, n):
            out_ref[rows] = out_ref[rows] + _rms(att_ref[rows], g_ple_post_ref[...])

    def tail_pieces(r0):
        return ([functools.partial(ffn_residual, r0 + j * quarter, quarter) for j in range(4)]
                + [functools.partial(gate, r0), functools.partial(embedding, r0)]
                + [functools.partial(embedding_residual, r0 + j * quarter, quarter)
                   for j in range(4)])

    def ffn(r0, pieces):
        rows = slice(r0, r0 + half)
        f_ref[rows] = _swiglu(xn_ref.at[rows], wgu_ref, wd_ref, act_ref, dict(enumerate(pieces)))

    for piece in head_pieces(0):
        piece()
    ffn(0, head_pieces(half))
    ffn(half, tail_pieces(0))
    for piece in tail_pieces(half):
        piece()


def _stage3(h1, mix_a, outs_b, lses_b, p2, w_out, b_out, g_attn_post, g_pre, wgu, wd, g_post,
            g_ple_pre, w_gate, w_proj, g_ple_post, seq):
    tokens, d_model = h1.shape
    tm = TOKEN_TILE
    tiles_per_seq = seq // tm
    d_ff = wd.shape[0]
    half = mix_a.shape[1]
    n_dil = len(DILATIONS)

    def rows(width):
        return pl.BlockSpec((tm, width), lambda i: (i, 0))

    def class_spec(dil):
        return pl.BlockSpec((None, dil, tm // dil, half),
                            lambda i: (i // tiles_per_seq, 0, i % tiles_per_seq, 0))

    vec = _const_spec((1, d_model))
    hbm = pl.BlockSpec(memory_space=pl.ANY)
    return pl.pallas_call(
        _stage3_kernel,
        grid=(tokens // tm,),
        in_specs=[rows(d_model), rows(half)]
        + [class_spec(d) for d in DILATIONS] * 2
        + [
            rows(p2.shape[1]),
            hbm, vec, vec, vec,
            hbm, hbm, vec, vec,
            hbm, hbm, vec,
        ],
        out_specs=rows(d_model),
        out_shape=jax.ShapeDtypeStruct((tokens, d_model), F32),
        scratch_shapes=[
            pltpu.VMEM((tm, d_model), BF16),
            pltpu.VMEM((tm // 2, d_ff), BF16),
            pltpu.VMEM((tm, half), BF16),
            pltpu.VMEM((2 * (n_dil - 1) * (half // SLAB), tm, SLAB), F32),
            pltpu.VMEM((2 * (half // SLAB), tm // 2, SLAB), F32),
            pltpu.VMEM((tm, d_model), F32),
            pltpu.VMEM((tm, d_model), F32),
            pltpu.VMEM(w_out.shape, BF16),
            pltpu.VMEM(wgu.shape, BF16),
            pltpu.VMEM(wd.shape, BF16),
            pltpu.VMEM(w_gate.shape, BF16),
            pltpu.VMEM(w_proj.shape, BF16),
            pltpu.VMEM((2, WEIGHT_STAGE_ROWS, d_model), F32),
            pltpu.SemaphoreType.DMA((2,)),
        ],
        compiler_params=pltpu.CompilerParams(dimension_semantics=("arbitrary",),
                                             vmem_limit_bytes=VMEM_LIMIT_TOKEN_STAGES),
        name="outproj_ffn2_ple",
    )(h1, mix_a, *outs_b, *lses_b, p2, w_out, b_out, g_attn_post, g_pre, wgu, wd, g_post,
      g_ple_pre, w_gate, w_proj, g_ple_post)


def kernel(x, p, rel_bias, ffn1_pre_g, ffn1_w_gu, ffn1_w_down, ffn1_post_g, attn_pre_g, w_in, b_in,
           sinks, w_out, b_out, attn_post_g, ffn2_pre_g, ffn2_w_gu, ffn2_w_down, ffn2_post_g,
           ple_pre_g, w_ple_gate, w_ple_proj, ple_post_g):
    batch, seq, d_model = x.shape
    depth = p.shape[0]
    assert seq % (max(DILATIONS) * ATTN_BLOCK) == 0 and seq % TOKEN_TILE == 0
    assert x.dtype == F32

    bias = _bias_tables(rel_bias)
    h = x.reshape(batch * seq, d_model)
    for i in range(depth):
        row = lambda v: v[i].reshape(1, -1)
        h1, qa, kva, qb, kb, vb = _stage1(
            h, row(ffn1_pre_g), ffn1_w_gu[i], ffn1_w_down[i],
            row(ffn1_post_g), row(attn_pre_g), w_in[i], row(b_in), batch, seq)
        mix_a = _attn_a(qa, kva, bias[0], sinks[i])
        outs_b, lses_b = [], []
        for pat in range(len(DILATIONS)):
            o, lse = _attn_b(qb[pat], kb[pat], vb[pat], bias[1 + pat])
            outs_b.append(o)
            lses_b.append(lse)
        h = _stage3(
            h1, mix_a, outs_b, lses_b, p[i].reshape(batch * seq, -1),
            w_out[i], row(b_out), row(attn_post_g), row(ffn2_pre_g),
            ffn2_w_gu[i], ffn2_w_down[i], row(ffn2_post_g),
            row(ple_pre_g), w_ple_gate[i], w_ple_proj[i],
            row(ple_post_g), seq)
    return h.reshape(batch, seq, d_model)
```

```python
import functools
import math

import numpy as np
import jax
import jax.numpy as jnp
from jax import lax
from jax.experimental import pallas as pl
from jax.experimental.pallas import tpu as pltpu

HEAD_DIM = 64
A_Q_HEADS = 8
A_KV_HEADS = 2
A_WINDOW = 128
B_HEADS = 8
B_PATTERNS = ((128, 1), (512, 4), (2048, 16))
DILATIONS = tuple(d for _, d in B_PATTERNS)
NUM_BUCKETS = 32
MAX_DISTANCE = 2048
EPS = 1e-6
NEG_INF = -1e30

ATTN_BLOCK = 128
LANE_BLOCK = 256
HEADS_PER_LANE_BLOCK = LANE_BLOCK // HEAD_DIM
SLAB = 128
SLABS_PER_LANE_BLOCK = LANE_BLOCK // SLAB
STAGING_SLOTS = 2
FF_CHUNK = 256
TOKEN_TILE = 512
ROW_CHUNK = 16
WEIGHT_STAGE_ROWS = 512
ATTN_ROWS = 2048
VMEM_LIMIT_TOKEN_STAGES = 58 * 1024 * 1024
VMEM_LIMIT_ATTENTION = 40 * 1024 * 1024

F32 = jnp.float32
BF16 = jnp.bfloat16


def _rms(x, g):
    return x * lax.rsqrt(jnp.mean(x * x, axis=-1, keepdims=True) + EPS) * g


def _row_chunks(n_rows):
    return [slice(r, r + ROW_CHUNK) for r in range(0, n_rows, ROW_CHUNK)]


def _sigmoid(x):
    return 1.0 / (1.0 + jnp.exp(-x))


def _swiglu(xn_ref, wgu_ref, wd_ref, act_ref, side_work=None):
    d_ff = wd_ref.shape[0]
    side_work = dict(side_work or {})
    for c in range(d_ff // FF_CHUNK):
        lo = c * FF_CHUNK
        g = jnp.dot(xn_ref[...], wgu_ref[:, lo:lo + FF_CHUNK], preferred_element_type=F32)
        u = jnp.dot(xn_ref[...], wgu_ref[:, d_ff + lo:d_ff + lo + FF_CHUNK],
                    preferred_element_type=F32)
        act_ref[:, lo:lo + FF_CHUNK] = ((g * _sigmoid(g)) * u).astype(BF16)
        if c in side_work:
            side_work.pop(c)()
    assert not side_work
    return jnp.dot(act_ref[...], wd_ref[...], preferred_element_type=F32)


def _load_as_bf16(src_hbm, dst_ref, stage_ref, sem_ref):
    s_rows, s_cols = stage_ref.shape[1:]
    rows, cols = src_hbm.shape
    pieces = [(r, min(s_rows, rows - r), c, min(s_cols, cols - c))
              for r in range(0, rows, s_rows) for c in range(0, cols, s_cols)]

    def copy(i):
        r, nr, c, nc = pieces[i]
        return pltpu.make_async_copy(src_hbm.at[pl.ds(r, nr), pl.ds(c, nc)],
                                     stage_ref.at[i % 2, pl.ds(0, nr), pl.ds(0, nc)],
                                     sem_ref.at[i % 2])

    copy(0).start()
    for i, (r, nr, c, nc) in enumerate(pieces):
        if i + 1 < len(pieces):
            copy(i + 1).start()
        copy(i).wait()
        dst_ref[r:r + nr, c:c + nc] = stage_ref[i % 2, :nr, :nc].astype(BF16)


def _const_spec(shape):
    return pl.BlockSpec(shape, lambda *_: (0,) * len(shape), pipeline_mode=pl.Buffered(1))


def _t5_bucket_np(dist):
    max_exact = NUM_BUCKETS // 2
    n = np.maximum(dist, 0)
    nf = np.maximum(n, 1).astype(np.float32)
    large = max_exact + (np.log(nf / np.float32(max_exact))
                         / np.float32(math.log(MAX_DISTANCE / max_exact))
                         * np.float32(NUM_BUCKETS - max_exact)).astype(np.int32)
    large = np.minimum(large, NUM_BUCKETS - 1)
    return np.where(n < max_exact, n, large).astype(np.int32)


def _bucket_tables():
    q = np.arange(ATTN_BLOCK)[:, None]
    k = np.arange(2 * ATTN_BLOCK)[None, :]
    rel = ATTN_BLOCK + q - k
    tables = []
    for max_dist, stride in ((A_WINDOW - 1, 1),) + tuple((w // d, d) for w, d in B_PATTERNS):
        valid = (rel >= 0) & (rel <= max_dist)
        bkt = _t5_bucket_np(rel * stride)
        later = np.where(valid, bkt, -1)
        first = np.where(valid & (k >= ATTN_BLOCK), bkt, -1)
        tables.append(np.stack([first, later]))
    return np.stack(tables).astype(np.int32)


def _bias_kernel(buckets_present, rb_ref, bkt_ref, out_ref):
    head = pl.program_id(0)
    first_block_keys = lax.broadcasted_iota(jnp.int32, bkt_ref.shape[2:], 1) >= ATTN_BLOCK
    for pattern, present in enumerate(buckets_present):
        col = head if pattern == 0 else head + A_Q_HEADS
        bkt = bkt_ref[pattern, 1]
        later = jnp.full(bkt.shape, NEG_INF, F32)
        for b in present:
            later = jnp.where(bkt == b, rb_ref[b, col], later)
        out_ref[pattern, 1] = later
        out_ref[pattern, 0] = jnp.where(first_block_keys, later, NEG_INF)


def _bias_tables(rel_bias):
    bkt_np = _bucket_tables()
    assert (bkt_np[:, 0] == np.where(np.arange(2 * ATTN_BLOCK) >= ATTN_BLOCK,
                                     bkt_np[:, 1], -1)).all()
    buckets_present = tuple(tuple(int(b) for b in np.unique(t[1]) if b >= 0) for t in bkt_np)
    n_pat = bkt_np.shape[0]
    table = (2, ATTN_BLOCK, 2 * ATTN_BLOCK)
    out = pl.pallas_call(
        functools.partial(_bias_kernel, buckets_present),
        grid=(A_Q_HEADS,),
        in_specs=[
            pl.BlockSpec(memory_space=pltpu.SMEM),
            pl.BlockSpec((n_pat,) + table, lambda h: (0, 0, 0, 0)),
        ],
        out_specs=pl.BlockSpec((n_pat, None) + table, lambda h: (0, h, 0, 0, 0)),
        out_shape=jax.ShapeDtypeStruct((n_pat, A_Q_HEADS) + table, F32),
        compiler_params=pltpu.CompilerParams(dimension_semantics=("arbitrary",)),
        name="bias_tables",
    )(rel_bias, jnp.asarray(bkt_np))
    return out.reshape(n_pat, A_Q_HEADS * 2, ATTN_BLOCK, 2 * ATTN_BLOCK)


def _stage1_kernel(x_ref, xprev_ref, g_pre_ref, wgu_hbm, wd_hbm, g_post_ref, g_attn_ref, win_hbm,
                   bin_ref, h_ref, qa_ref, kva_ref, *rest):
    n_dil = len(DILATIONS)
    qb_refs, kb_refs, vb_refs = rest[:n_dil], rest[n_dil:2 * n_dil], rest[2 * n_dil:3 * n_dil]
    (xn_ref, act_ref, f_ref, xn2_ref, slab_ref, slab2_ref,
     wgu_ref, wd_ref, win_ref, sem_ref) = rest[3 * n_dil:]
    tm = x_ref.shape[0]
    step = pl.program_id(0)
    rd_slot = (step + 1) % 2
    wr_slot = step % 2

    @pl.when(step == 0)
    def _():
        _load_as_bf16(wgu_hbm, wgu_ref, f_ref, sem_ref)
        _load_as_bf16(wd_hbm, wd_ref, f_ref, sem_ref)
        _load_as_bf16(win_hbm, win_ref, f_ref, sem_ref)
        f_ref[1] = jnp.zeros(f_ref.shape[1:], F32)

    for rows in _row_chunks(tm):
        xn_ref[rows] = _rms(x_ref[rows], g_pre_ref[...]).astype(BF16)
    for rows in _row_chunks(tm):
        h = xprev_ref[rows] + 0.5 * _rms(f_ref[rd_slot, rows], g_post_ref[...])
        h_ref[rows] = h
        xn2_ref[rows] = _rms(h, g_attn_ref[...]).astype(BF16)

    def proj(col_block):
        lo = col_block * LANE_BLOCK
        return (jnp.dot(xn2_ref[...], win_ref[:, lo:lo + LANE_BLOCK], preferred_element_type=F32)
                + bin_ref[:, lo:lo + LANE_BLOCK])

    def gather_rows(src_ref, base, start, n, stride):
        return [src_ref[base + s, pl.ds(start, n, stride=stride), :]
                for s in range(SLABS_PER_LANE_BLOCK)]

    def write_classes(z, refs, jb, slot):
        ref1, ref4, ref16 = refs
        base = slot * SLABS_PER_LANE_BLOCK
        ref1[jb, 0] = z.astype(BF16)
        for s in range(SLABS_PER_LANE_BLOCK):
            slab_ref[base + s] = z[:, s * SLAB:(s + 1) * SLAB]
        n4 = tm // 4
        for c4 in range(4):
            pieces = gather_rows(slab_ref, base, c4, n4, 4)
            ref4[jb, c4] = jnp.concatenate(pieces, axis=1).astype(BF16)
            for s in range(SLABS_PER_LANE_BLOCK):
                slab2_ref[base + s, c4 * n4:(c4 + 1) * n4, :] = pieces[s]
        for c4 in range(4):
            for j in range(4):
                pieces = gather_rows(slab2_ref, base, c4 * n4 + j, n4 // 4, 4)
                ref16[jb, c4 + 4 * j] = jnp.concatenate(pieces, axis=1).astype(BF16)

    q_scale = HEAD_DIM ** -0.5
    n_slots = slab_ref.shape[0] // SLABS_PER_LANE_BLOCK

    def store_kva():
        kva_ref[...] = proj(2).astype(BF16)

    def store_qa(j):
        qa_ref[j] = (proj(j) * q_scale).astype(BF16)

    def store_classes(col_block, scale, refs, jb, slot):
        z = proj(col_block)
        write_classes(z if scale == 1.0 else z * scale, refs, jb, slot % n_slots)

    projections = [store_kva]
    for j in range(2):
        projections += [
            functools.partial(store_qa, j),
            functools.partial(store_classes, 3 + j, q_scale, qb_refs, j, 3 * j),
            functools.partial(store_classes, 5 + j, 1.0, kb_refs, j, 3 * j + 1),
            functools.partial(store_classes, 7 + j, 1.0, vb_refs, j, 3 * j + 2),
        ]
    side_work = {1 + c: work for c, work in enumerate(projections)}
    f_ref[wr_slot] = _swiglu(xn_ref, wgu_ref, wd_ref, act_ref, side_work)


def _stage1(x2, g_pre, wgu, wd, g_post, g_attn, w_in, b_in, batch, seq):
    tokens, d_model = x2.shape
    tm = TOKEN_TILE
    tiles_per_seq = seq // tm
    d_ff = wd.shape[0]
    n_blk = 2
    n_dil = len(DILATIONS)
    n_tiles = tokens // tm
    assert DILATIONS == (1, 4, 16) and tm % (16 * 16) == 0

    def lagged(i):
        j = jnp.maximum(i - 1, 0)
        return j // tiles_per_seq, j % tiles_per_seq

    def class_spec(dil):
        return pl.BlockSpec((None, n_blk, dil, tm // dil, LANE_BLOCK),
                            lambda i: (lagged(i)[0], 0, 0, lagged(i)[1], 0))

    def class_shape(dil):
        return jax.ShapeDtypeStruct((batch, n_blk, dil, seq // dil, LANE_BLOCK), BF16)

    row_spec = pl.BlockSpec((tm, d_model), lambda i: (jnp.minimum(i, n_tiles - 1), 0))
    lagged_row_spec = pl.BlockSpec((tm, d_model), lambda i: (jnp.maximum(i - 1, 0), 0))
    hbm_spec = pl.BlockSpec(memory_space=pl.ANY)
    outs = pl.pallas_call(
        _stage1_kernel,
        grid=(n_tiles + 1,),
        in_specs=[
            row_spec,
            lagged_row_spec,
            _const_spec((1, d_model)),
            hbm_spec,
            hbm_spec,
            _const_spec((1, d_model)),
            _const_spec((1, d_model)),
            hbm_spec,
            _const_spec(b_in.shape),
        ],
        out_specs=[
            lagged_row_spec,
            pl.BlockSpec((None, n_blk, tm, LANE_BLOCK),
                         lambda i: (lagged(i)[0], 0, lagged(i)[1], 0)),
            pl.BlockSpec((None, tm, LANE_BLOCK),
                         lambda i: (lagged(i)[0], lagged(i)[1], 0)),
        ] + [class_spec(d) for _ in range(3) for d in DILATIONS],
        out_shape=[
            jax.ShapeDtypeStruct((tokens, d_model), F32),
            jax.ShapeDtypeStruct((batch, n_blk, seq, LANE_BLOCK), BF16),
            jax.ShapeDtypeStruct((batch, seq, LANE_BLOCK), BF16),
        ] + [class_shape(d) for _ in range(3) for d in DILATIONS],
        scratch_shapes=[
            pltpu.VMEM((tm, d_model), BF16),
            pltpu.VMEM((tm, d_ff), BF16),
            pltpu.VMEM((2, tm, d_model), F32),
            pltpu.VMEM((tm, d_model), BF16),
            pltpu.VMEM((STAGING_SLOTS * SLABS_PER_LANE_BLOCK, tm, SLAB), F32),
            pltpu.VMEM((STAGING_SLOTS * SLABS_PER_LANE_BLOCK, tm, SLAB), F32),
            pltpu.VMEM(wgu.shape, BF16),
            pltpu.VMEM(wd.shape, BF16),
            pltpu.VMEM(w_in.shape, BF16),
            pltpu.SemaphoreType.DMA((2,)),
        ],
        compiler_params=pltpu.CompilerParams(dimension_semantics=("arbitrary",),
                                             vmem_limit_bytes=VMEM_LIMIT_TOKEN_STAGES),
        name="ffn1_inproj",
    )(x2, x2, g_pre, wgu, wd, g_post, g_attn, w_in, b_in)
    h1, qa, kva = outs[:3]
    qb, kb, vb = outs[3:3 + n_dil], outs[3 + n_dil:3 + 2 * n_dil], outs[3 + 2 * n_dil:]
    return h1, qa, kva, qb, kb, vb


def _lane_is_low(shape):
    return lax.broadcasted_iota(jnp.int32, shape, 1) < HEAD_DIM


def _masked_kv(k_grp, v_grp):
    low = _lane_is_low(k_grp.shape)
    zero = jnp.zeros_like(k_grp)
    lane = lax.broadcasted_iota(jnp.int32, k_grp.shape, 1)
    ones_low = jnp.clip(HEAD_DIM - lane, 0, 1).astype(F32).astype(BF16)
    ones_high = jnp.clip(lane - (HEAD_DIM - 1), 0, 1).astype(F32).astype(BF16)
    k_par = (jnp.where(low, k_grp, zero), jnp.where(low, zero, k_grp))
    v_par = (jnp.concatenate([jnp.where(low, v_grp, zero), ones_low], axis=1),
             jnp.concatenate([jnp.where(low, zero, v_grp), ones_high], axis=1))
    return k_par, v_par


def _pair_attention(q, k_par, v_par, biases):
    pair = 2 * HEAD_DIM
    acc = None
    maxes = []
    for par in range(2):
        s = lax.dot_general(q, k_par[par], (((1,), (1,)), ((), ())),
                            preferred_element_type=F32) + biases[par]
        m = jnp.max(s, axis=-1, keepdims=True)
        p = jnp.exp(s - m).astype(BF16)
        pv = jnp.dot(p, v_par[par], preferred_element_type=F32)
        acc = pv if acc is None else acc + pv
        maxes.append(m)
    m_pair = jnp.where(_lane_is_low((q.shape[0], pair)), maxes[0], maxes[1])
    return acc[:, :pair], acc[:, pair:], m_pair


def _attn_b_kernel(q_ref, kp_ref, kc_ref, vp_ref, vc_ref, bias_ref, o_ref, lse_ref):
    first_chunk = pl.program_id(2) == 0
    n_blk, n_cls, rows, _ = q_ref.shape
    pair = 2 * HEAD_DIM
    for cls in range(n_cls):
        for jb in range(n_blk):
            k_all = jnp.concatenate([kp_ref[jb, cls], kc_ref[jb, cls]], axis=0)
            v_all = jnp.concatenate([vp_ref[jb, cls], vc_ref[jb, cls]], axis=0)
            for grp in range(LANE_BLOCK // pair):
                lanes = slice(grp * pair, (grp + 1) * pair)
                k_par, v_par = _masked_kv(k_all[:, lanes], v_all[:, lanes])
                for t in range(rows // ATTN_BLOCK):
                    q = q_ref[jb, cls, t * ATTN_BLOCK:(t + 1) * ATTN_BLOCK, lanes]
                    keys = slice(t * ATTN_BLOCK, (t + 2) * ATTN_BLOCK)
                    variant = jnp.where(first_chunk, 0, 1) if t == 0 else 1
                    head = jb * HEADS_PER_LANE_BLOCK + grp * 2
                    biases = [bias_ref[(head + par) * 2 + variant] for par in range(2)]
                    o, den, m = _pair_attention(q, [k[keys] for k in k_par],
                                                [v[keys] for v in v_par], biases)
                    out_lanes = slice(jb * LANE_BLOCK + grp * pair,
                                      jb * LANE_BLOCK + (grp + 1) * pair)
                    out_rows = slice(t * ATTN_BLOCK, (t + 1) * ATTN_BLOCK)
                    o_ref[cls, out_rows, out_lanes] = (o / den).astype(BF16)
                    lse_ref[cls, out_rows, out_lanes] = m + jnp.log(den)


def _attn_b(qb, kb, vb, bias):
    batch, n_blk, n_cls, class_len, _ = qb.shape
    rows = min(ATTN_ROWS, class_len)
    cls_per_step = min(n_cls, ATTN_ROWS // rows)
    prev_per_chunk = rows // ATTN_BLOCK

    cur_spec = pl.BlockSpec((None, n_blk, cls_per_step, rows, LANE_BLOCK),
                            lambda b, c, i: (b, 0, c, i, 0))
    prev_spec = pl.BlockSpec(
        (None, n_blk, cls_per_step, ATTN_BLOCK, LANE_BLOCK),
        lambda b, c, i: (b, 0, c, jnp.maximum(i * prev_per_chunk - 1, 0), 0))
    width = n_blk * LANE_BLOCK
    out_spec = pl.BlockSpec((None, cls_per_step, rows, width), lambda b, c, i: (b, c, i, 0))
    return pl.pallas_call(
        _attn_b_kernel,
        grid=(batch, n_cls // cls_per_step, class_len // rows),
        in_specs=[cur_spec, prev_spec, cur_spec, prev_spec, cur_spec,
                  _const_spec(bias.shape)],
        out_specs=[out_spec, out_spec],
        out_shape=[jax.ShapeDtypeStruct((batch, n_cls, class_len, width), BF16),
                   jax.ShapeDtypeStruct((batch, n_cls, class_len, width), F32)],
        compiler_params=pltpu.CompilerParams(
            dimension_semantics=("arbitrary", "arbitrary", "arbitrary"),
            vmem_limit_bytes=VMEM_LIMIT_ATTENTION),
        name=f"mixer_b_dil{n_cls}",
    )(qb, kb, kb, vb, vb, bias)


def _swap_halves(x):
    return jnp.concatenate([x[:, HEAD_DIM:], x[:, :HEAD_DIM]], axis=1)


def _attn_a_kernel(sink_ref, q_ref, kvp_ref, kvc_ref, bias_ref, o_ref):
    first_chunk = pl.program_id(1) == 0
    rows = q_ref.shape[1]
    pair = 2 * HEAD_DIM
    low_out = _lane_is_low((ATTN_BLOCK, pair))
    kv_all = jnp.concatenate([kvp_ref[...], kvc_ref[...]], axis=0)
    k_nat, v_nat = kv_all[:, :pair], kv_all[:, pair:]

    def both_halves(x, g):
        half = x[:, g * HEAD_DIM:(g + 1) * HEAD_DIM]
        return jnp.concatenate([half, half], axis=1)

    for g in range(A_KV_HEADS):
        k_par, v_par = _masked_kv(both_halves(k_nat, g), both_halves(v_nat, g))
        for grp in range(LANE_BLOCK // pair):
            lanes = slice(grp * pair, (grp + 1) * pair)
            head = g * HEADS_PER_LANE_BLOCK + grp * 2
            sink = jnp.where(low_out, sink_ref[head], sink_ref[head + 1])
            for t in range(rows // ATTN_BLOCK):
                q = q_ref[g, t * ATTN_BLOCK:(t + 1) * ATTN_BLOCK, lanes]
                keys = slice(t * ATTN_BLOCK, (t + 2) * ATTN_BLOCK)
                variant = jnp.where(first_chunk, 0, 1) if t == 0 else 1
                biases = [bias_ref[(head + par) * 2 + variant] for par in range(2)]
                o, den, m = _pair_attention(q, [k[keys] for k in k_par],
                                            [v[keys] for v in v_par], biases)
                total = den + jnp.exp(sink - m)
                out_lanes = slice(g * LANE_BLOCK + grp * pair, g * LANE_BLOCK + (grp + 1) * pair)
                o_ref[t * ATTN_BLOCK:(t + 1) * ATTN_BLOCK, out_lanes] = (o / total).astype(BF16)


def _attn_a(qa, kva, bias, sinks):
    batch, n_blk, seq, _ = qa.shape
    rows = min(ATTN_ROWS, seq)
    prev_per_chunk = rows // ATTN_BLOCK
    width = n_blk * LANE_BLOCK
    out = pl.pallas_call(
        _attn_a_kernel,
        grid=(batch, seq // rows),
        in_specs=[
            pl.BlockSpec(memory_space=pltpu.SMEM),
            pl.BlockSpec((None, n_blk, rows, LANE_BLOCK), lambda b, i: (b, 0, i, 0)),
            pl.BlockSpec((None, ATTN_BLOCK, LANE_BLOCK),
                         lambda b, i: (b, jnp.maximum(i * prev_per_chunk - 1, 0), 0)),
            pl.BlockSpec((None, rows, LANE_BLOCK), lambda b, i: (b, i, 0)),
            _const_spec(bias.shape),
        ],
        out_specs=pl.BlockSpec((None, rows, width), lambda b, i: (b, i, 0)),
        out_shape=jax.ShapeDtypeStruct((batch, seq, width), BF16),
        compiler_params=pltpu.CompilerParams(
            dimension_semantics=("arbitrary", "arbitrary"),
            vmem_limit_bytes=VMEM_LIMIT_ATTENTION),
        name="mixer_a",
    )(sinks, qa, kva, kva, bias)
    return out.reshape(batch * seq, width)


def _stage3_kernel(*refs):
    n_dil = len(DILATIONS)
    h_ref, mixa_ref = refs[:2]
    o_refs = refs[2:2 + n_dil]
    l_refs = refs[2 + n_dil:2 + 2 * n_dil]
    (p_ref, wout_hbm, bout_ref, g_attn_post_ref, g_pre_ref, wgu_hbm, wd_hbm, g_post_ref,
     g_ple_pre_ref, wgate_hbm, wproj_hbm, g_ple_post_ref,
     out_ref, xn_ref, act_ref, mixb_ref, slab_ref, mid_ref, att_ref, f_ref,
     wout_ref, wgu_ref, wd_ref, wgate_ref, wproj_ref, stage_ref, sem_ref) = refs[2 + 2 * n_dil:]
    tm = h_ref.shape[0]
    n_slabs = mixa_ref.shape[1] // SLAB

    @pl.when(pl.program_id(0) == 0)
    def _():
        for src, dst in ((wout_hbm, wout_ref), (wgu_hbm, wgu_ref), (wd_hbm, wd_ref),
                         (wgate_hbm, wgate_ref), (wproj_hbm, wproj_ref)):
            _load_as_bf16(src, dst, stage_ref, sem_ref)

    d_a = mixa_ref.shape[1]
    half = tm // 2
    quarter = half // 4

    def sub_chunks(r0, n):
        return [slice(r, r + ROW_CHUNK) for r in range(r0, r0 + n, ROW_CHUNK)]

    def interleave_merge(s, r0):
        lanes = slice(s * SLAB, (s + 1) * SLAB)
        for k, dil in enumerate(d for d in DILATIONS if d != 1):
            idx = DILATIONS.index(dil)
            n = half // dil
            for src_ref, which in ((o_refs[idx], 0), (l_refs[idx], 1)):
                dst = (2 * k + which) * n_slabs + s
                if dil == 4:
                    sub = 2 * ROW_CHUNK
                    for c in range(dil):
                        for u0 in range(r0 // dil, r0 // dil + n, sub):
                            blk = src_ref[c, u0:u0 + sub, lanes].astype(F32)
                            slab_ref[dst, pl.ds(c + u0 * dil, sub, stride=dil), :] = blk
                else:
                    mid = which * n_slabs + s
                    n4 = half // 4
                    for c4 in range(4):
                        for j in range(4):
                            blk = src_ref[c4 + 4 * j, r0 // dil:r0 // dil + n, lanes].astype(F32)
                            mid_ref[mid, pl.ds(c4 * n4 + j, n, stride=4), :] = blk
                        for w0 in range(0, n4, 2 * ROW_CHUNK):
                            piece = mid_ref[mid, c4 * n4 + w0:c4 * n4 + w0 + 2 * ROW_CHUNK, :]
                            slab_ref[dst, pl.ds(r0 + c4 + 4 * w0, 2 * ROW_CHUNK, stride=4), :] = piece
        for q0 in range(r0, r0 + half, 2 * ROW_CHUNK):
            rows = slice(q0, q0 + 2 * ROW_CHUNK)
            outs, lses = [], []
            k = 0
            for idx, dil in enumerate(DILATIONS):
                if dil == 1:
                    outs.append(o_refs[idx][0, rows, lanes].astype(F32))
                    lses.append(l_refs[idx][0, rows, lanes])
                else:
                    outs.append(slab_ref[2 * k * n_slabs + s, rows, :])
                    lses.append(slab_ref[(2 * k + 1) * n_slabs + s, rows, :])
                    k += 1
            l_max = lses[0]
            for l in lses[1:]:
                l_max = jnp.maximum(l_max, l)
            weights = [jnp.exp(l - l_max) for l in lses]
            num = weights[0] * outs[0]
            den = weights[0]
            for w, o in zip(weights[1:], outs[1:]):
                num = num + w * o
                den = den + w
            mixb_ref[rows, lanes] = (num / den).astype(BF16)

    def out_projection(part, r0):
        rows = slice(r0, r0 + half)
        if part == 0:
            att_ref[rows] = (jnp.dot(mixa_ref[rows], wout_ref[:d_a, :], preferred_element_type=F32)
                             + bout_ref[...])
        else:
            lo = (part - 1) * 2 * SLAB
            hi = lo + 2 * SLAB
            att_ref[rows] = att_ref[rows] + jnp.dot(
                mixb_ref[rows, lo:hi], wout_ref[d_a + lo:d_a + hi, :], preferred_element_type=F32)

    def attention_residual(r0, n):
        for rows in sub_chunks(r0, n):
            h2 = h_ref[rows] + _rms(att_ref[rows], g_attn_post_ref[...])
            out_ref[rows] = h2
            xn_ref[rows] = _rms(h2, g_pre_ref[...]).astype(BF16)

    def head_pieces(r0):
        return [functools.partial(interleave_merge, 0, r0),
                functools.partial(interleave_merge, 1, r0),
                functools.partial(out_projection, 0, r0),
                functools.partial(out_projection, 1, r0),
                functools.partial(interleave_merge, 2, r0),
                functools.partial(interleave_merge, 3, r0),
                functools.partial(out_projection, 2, r0)] + [
                    functools.partial(attention_residual, r0 + j * quarter, quarter)
                    for j in range(4)]

    def ffn_residual(r0, n):
        for rows in sub_chunks(r0, n):
            h3 = out_ref[rows] + 0.5 * _rms(f_ref[rows], g_post_ref[...])
            out_ref[rows] = h3
            xn_ref[rows] = _rms(h3, g_ple_pre_ref[...]).astype(BF16)

    def gate(r0):
        rows = slice(r0, r0 + half)
        att_ref[rows] = _sigmoid(
            jnp.dot(xn_ref[rows], wgate_ref[...], preferred_element_type=F32))

    def embedding(r0):
        rows = slice(r0, r0 + half)
        att_ref[rows] = att_ref[rows] * jnp.dot(
            p_ref[rows].astype(BF16), wproj_ref[...], preferred_element_type=F32)

    def embedding_residual(r0, n):
        for rows in sub_chunks(r0, n):
            out_ref[rows] = out_ref[rows] + _rms(att_ref[rows], g_ple_post_ref[...])

    def tail_pieces(r0):
        return ([functools.partial(ffn_residual, r0 + j * quarter, quarter) for j in range(4)]
                + [functools.partial(gate, r0), functools.partial(embedding, r0)]
                + [functools.partial(embedding_residual, r0 + j * quarter, quarter)
                   for j in range(4)])

    def ffn(r0, pieces):
        rows = slice(r0, r0 + half)
        f_ref[rows] = _swiglu(xn_ref.at[rows], wgu_ref, wd_ref, act_ref, dict(enumerate(pieces)))

    for piece in head_pieces(0):
        piece()
    ffn(0, head_pieces(half))
    ffn(half, tail_pieces(0))
    for piece in tail_pieces(half):
        piece()


def _stage3(h1, mix_a, outs_b, lses_b, p2, w_out, b_out, g_attn_post, g_pre, wgu, wd, g_post,
            g_ple_pre, w_gate, w_proj, g_ple_post, seq):
    tokens, d_model = h1.shape
    tm = TOKEN_TILE
    tiles_per_seq = seq // tm
    d_ff = wd.shape[0]
    half = mix_a.shape[1]
    n_dil = len(DILATIONS)

    def rows(width):
        return pl.BlockSpec((tm, width), lambda i: (i, 0))

    def class_spec(dil):
        return pl.BlockSpec((None, dil, tm // dil, half),
                            lambda i: (i // tiles_per_seq, 0, i % tiles_per_seq, 0))

    vec = _const_spec((1, d_model))
    hbm = pl.BlockSpec(memory_space=pl.ANY)
    return pl.pallas_call(
        _stage3_kernel,
        grid=(tokens // tm,),
        in_specs=[rows(d_model), rows(half)]
        + [class_spec(d) for d in DILATIONS] * 2
        + [
            rows(p2.shape[1]),
            hbm, vec, vec, vec,
            hbm, hbm, vec, vec,
            hbm, hbm, vec,
        ],
        out_specs=rows(d_model),
        out_shape=jax.ShapeDtypeStruct((tokens, d_model), F32),
        scratch_shapes=[
            pltpu.VMEM((tm, d_model), BF16),
            pltpu.VMEM((tm // 2, d_ff), BF16),
            pltpu.VMEM((tm, half), BF16),
            pltpu.VMEM((2 * (n_dil - 1) * (half // SLAB), tm, SLAB), F32),
            pltpu.VMEM((2 * (half // SLAB), tm // 2, SLAB), F32),
            pltpu.VMEM((tm, d_model), F32),
            pltpu.VMEM((tm, d_model), F32),
            pltpu.VMEM(w_out.shape, BF16),
            pltpu.VMEM(wgu.shape, BF16),
            pltpu.VMEM(wd.shape, BF16),
            pltpu.VMEM(w_gate.shape, BF16),
            pltpu.VMEM(w_proj.shape, BF16),
            pltpu.VMEM((2, WEIGHT_STAGE_ROWS, d_model), F32),
            pltpu.SemaphoreType.DMA((2,)),
        ],
        compiler_params=pltpu.CompilerParams(dimension_semantics=("arbitrary",),
                                             vmem_limit_bytes=VMEM_LIMIT_TOKEN_STAGES),
        name="outproj_ffn2_ple",
    )(h1, mix_a, *outs_b, *lses_b, p2, w_out, b_out, g_attn_post, g_pre, wgu, wd, g_post,
      g_ple_pre, w_gate, w_proj, g_ple_post)


def kernel(x, p, rel_bias, ffn1_pre_g, ffn1_w_gu, ffn1_w_down, ffn1_post_g, attn_pre_g, w_in, b_in,
           sinks, w_out, b_out, attn_post_g, ffn2_pre_g, ffn2_w_gu, ffn2_w_down, ffn2_post_g,
           ple_pre_g, w_ple_gate, w_ple_proj, ple_post_g):
    batch, seq, d_model = x.shape
    depth = p.shape[0]
    assert seq % (max(DILATIONS) * ATTN_BLOCK) == 0 and seq % TOKEN_TILE == 0
    assert x.dtype == F32

    bias = _bias_tables(rel_bias)
    h = x.reshape(batch * seq, d_model)
    for i in range(depth):
        row = lambda v: v[i].reshape(1, -1)
        h1, qa, kva, qb, kb, vb = _stage1(
            h, row(ffn1_pre_g), ffn1_w_gu[i], ffn1_w_down[i],
            row(ffn1_post_g), row(attn_pre_g), w_in[i], row(b_in), batch, seq)
        mix_a = _attn_a(qa, kva, bias[0], sinks[i])
        outs_b, lses_b = [], []
        for pat in range(len(DILATIONS)):
            o, lse = _attn_b(qb[pat], kb[pat], vb[pat], bias[1 + pat])
            outs_b.append(o)
            lses_b.append(lse)
        h = _stage3(
            h1, mix_a, outs_b, lses_b, p[i].reshape(batch * seq, -1),
            w_out[i], row(b_out), row(attn_post_g), row(ffn2_pre_g),
            ffn2_w_gu[i], ffn2_w_down[i], row(ffn2_post_g),
            row(ple_pre_g), w_ple_gate[i], w_ple_proj[i],
            row(ple_post_g), seq)
    return h.reshape(batch, seq, d_model)
```

```python
import functools
import math

import numpy as np
import jax
import jax.numpy as jnp
from jax import lax
from jax.experimental import pallas as pl
from jax.experimental.pallas import tpu as pltpu

HEAD_DIM = 64
A_Q_HEADS = 8
A_KV_HEADS = 2
A_WINDOW = 128
B_PATTERNS = ((128, 1), (512, 4), (2048, 16))
DILATIONS = tuple(d for _, d in B_PATTERNS)
NUM_BUCKETS = 32
MAX_DISTANCE = 2048
EPS = 1e-6
NEG_INF = -1e30

ATTN_BLOCK = 128
LANE_BLOCK = 256
HEADS_PER_LANE_BLOCK = LANE_BLOCK // HEAD_DIM
SLAB = 128
SLABS_PER_LANE_BLOCK = LANE_BLOCK // SLAB
STAGING_SLOTS = 2
FF_CHUNK = 256
TOKEN_TILE = 512
ROW_CHUNK = 16
WEIGHT_STAGE_ROWS = 512
ATTN_ROWS = 2048
V7X_VMEM_BYTES = 64 * 1024 * 1024
VMEM_LIMIT_TOKEN_STAGES = V7X_VMEM_BYTES - 6 * 1024 * 1024
VMEM_LIMIT_ATTENTION = V7X_VMEM_BYTES - 24 * 1024 * 1024

F32 = jnp.float32
BF16 = jnp.bfloat16


def _rms(x, g):
    return x * lax.rsqrt(jnp.mean(x * x, axis=-1, keepdims=True) + EPS) * g


def _row_chunks(n_rows):
    return [slice(r, r + ROW_CHUNK) for r in range(0, n_rows, ROW_CHUNK)]


def _sigmoid(x):
    return 1.0 / (1.0 + jnp.exp(-x))


def _swiglu(xn_ref, wgu_ref, wd_ref, act_ref, side_work=None):
    d_ff = wd_ref.shape[0]
    side_work = dict(side_work or {})
    for c in range(d_ff // FF_CHUNK):
        lo = c * FF_CHUNK
        g = jnp.dot(xn_ref[...], wgu_ref[:, lo:lo + FF_CHUNK], preferred_element_type=F32)
        u = jnp.dot(xn_ref[...], wgu_ref[:, d_ff + lo:d_ff + lo + FF_CHUNK],
                    preferred_element_type=F32)
        act_ref[:, lo:lo + FF_CHUNK] = ((g * _sigmoid(g)) * u).astype(BF16)
        if c in side_work:
            side_work.pop(c)()
    assert not side_work
    return jnp.dot(act_ref[...], wd_ref[...], preferred_element_type=F32)


def _load_as_bf16(src_hbm, dst_ref, stage_ref, sem_ref):
    s_rows, s_cols = stage_ref.shape[1:]
    rows, cols = src_hbm.shape
    pieces = [(r, min(s_rows, rows - r), c, min(s_cols, cols - c))
              for r in range(0, rows, s_rows) for c in range(0, cols, s_cols)]

    def copy(i):
        r, nr, c, nc = pieces[i]
        return pltpu.make_async_copy(src_hbm.at[pl.ds(r, nr), pl.ds(c, nc)],
                                     stage_ref.at[i % 2, pl.ds(0, nr), pl.ds(0, nc)],
                                     sem_ref.at[i % 2])

    copy(0).start()
    for i, (r, nr, c, nc) in enumerate(pieces):
        if i + 1 < len(pieces):
            copy(i + 1).start()
        copy(i).wait()
        dst_ref[r:r + nr, c:c + nc] = stage_ref[i % 2, :nr, :nc].astype(BF16)


def _const_spec(shape):
    return pl.BlockSpec(shape, lambda *_: (0,) * len(shape), pipeline_mode=pl.Buffered(1))


def _t5_bucket_np(dist):
    max_exact = NUM_BUCKETS // 2
    n = np.maximum(dist, 0)
    nf = np.maximum(n, 1).astype(np.float32)
    large = max_exact + (np.log(nf / np.float32(max_exact))
                         / np.float32(math.log(MAX_DISTANCE / max_exact))
                         * np.float32(NUM_BUCKETS - max_exact)).astype(np.int32)
    large = np.minimum(large, NUM_BUCKETS - 1)
    return np.where(n < max_exact, n, large).astype(np.int32)


def _bucket_tables():
    q = np.arange(ATTN_BLOCK)[:, None]
    k = np.arange(2 * ATTN_BLOCK)[None, :]
    rel = ATTN_BLOCK + q - k
    tables = []
    for max_dist, stride in ((A_WINDOW - 1, 1),) + tuple((w // d, d) for w, d in B_PATTERNS):
        valid = (rel >= 0) & (rel <= max_dist)
        bkt = _t5_bucket_np(rel * stride)
        later = np.where(valid, bkt, -1)
        first = np.where(valid & (k >= ATTN_BLOCK), bkt, -1)
        tables.append(np.stack([first, later]))
    return np.stack(tables).astype(np.int32)


def _bias_kernel(buckets_present, rb_ref, bkt_ref, out_ref):
    head = pl.program_id(0)
    first_block_keys = lax.broadcasted_iota(jnp.int32, bkt_ref.shape[2:], 1) >= ATTN_BLOCK
    for pattern, present in enumerate(buckets_present):
        col = head if pattern == 0 else head + A_Q_HEADS
        bkt = bkt_ref[pattern, 1]
        later = jnp.full(bkt.shape, NEG_INF, F32)
        for b in present:
            later = jnp.where(bkt == b, rb_ref[b, col], later)
        out_ref[pattern, 1] = later
        out_ref[pattern, 0] = jnp.where(first_block_keys, later, NEG_INF)


def _bias_tables(rel_bias):
    bkt_np = _bucket_tables()
    assert (bkt_np[:, 0] == np.where(np.arange(2 * ATTN_BLOCK) >= ATTN_BLOCK,
                                     bkt_np[:, 1], -1)).all()
    buckets_present = tuple(tuple(int(b) for b in np.unique(t[1]) if b >= 0) for t in bkt_np)
    n_pat = bkt_np.shape[0]
    table = (2, ATTN_BLOCK, 2 * ATTN_BLOCK)
    out = pl.pallas_call(
        functools.partial(_bias_kernel, buckets_present),
        grid=(A_Q_HEADS,),
        in_specs=[
            pl.BlockSpec(memory_space=pltpu.SMEM),
            pl.BlockSpec((n_pat,) + table, lambda h: (0, 0, 0, 0)),
        ],
        out_specs=pl.BlockSpec((n_pat, None) + table, lambda h: (0, h, 0, 0, 0)),
        out_shape=jax.ShapeDtypeStruct((n_pat, A_Q_HEADS) + table, F32),
        compiler_params=pltpu.CompilerParams(dimension_semantics=("arbitrary",)),
        name="bias_tables",
    )(rel_bias, jnp.asarray(bkt_np))
    return out.reshape(n_pat, A_Q_HEADS * 2, ATTN_BLOCK, 2 * ATTN_BLOCK)


def _stage1_kernel(x_ref, xprev_ref, g_pre_ref, wgu_hbm, wd_hbm, g_post_ref, g_attn_ref, win_hbm,
                   bin_ref, h_ref, qa_ref, kva_ref, *rest):
    n_dil = len(DILATIONS)
    qb_refs, kb_refs, vb_refs = rest[:n_dil], rest[n_dil:2 * n_dil], rest[2 * n_dil:3 * n_dil]
    (xn_ref, act_ref, f_ref, xn2_ref, slab_ref, slab2_ref,
     wgu_ref, wd_ref, win_ref, sem_ref) = rest[3 * n_dil:]
    tm = x_ref.shape[0]
    step = pl.program_id(0)
    rd_slot = (step + 1) % 2
    wr_slot = step % 2

    @pl.when(step == 0)
    def _():
        _load_as_bf16(wgu_hbm, wgu_ref, f_ref, sem_ref)
        _load_as_bf16(wd_hbm, wd_ref, f_ref, sem_ref)
        _load_as_bf16(win_hbm, win_ref, f_ref, sem_ref)
        f_ref[1] = jnp.zeros(f_ref.shape[1:], F32)

    for rows in _row_chunks(tm):
        xn_ref[rows] = _rms(x_ref[rows], g_pre_ref[...]).astype(BF16)
    for rows in _row_chunks(tm):
        h = xprev_ref[rows] + 0.5 * _rms(f_ref[rd_slot, rows], g_post_ref[...])
        h_ref[rows] = h
        xn2_ref[rows] = _rms(h, g_attn_ref[...]).astype(BF16)

    def proj(col_block):
        lo = col_block * LANE_BLOCK
        return (jnp.dot(xn2_ref[...], win_ref[:, lo:lo + LANE_BLOCK], preferred_element_type=F32)
                + bin_ref[:, lo:lo + LANE_BLOCK])

    def gather_rows(src_ref, base, start, n, stride):
        return [src_ref[base + s, pl.ds(start, n, stride=stride), :]
                for s in range(SLABS_PER_LANE_BLOCK)]

    def write_classes(z, refs, jb, slot):
        ref1, ref4, ref16 = refs
        base = slot * SLABS_PER_LANE_BLOCK
        ref1[jb, 0] = z.astype(BF16)
        for s in range(SLABS_PER_LANE_BLOCK):
            slab_ref[base + s] = z[:, s * SLAB:(s + 1) * SLAB]
        n4 = tm // 4
        for c4 in range(4):
            pieces = gather_rows(slab_ref, base, c4, n4, 4)
            ref4[jb, c4] = jnp.concatenate(pieces, axis=1).astype(BF16)
            for s in range(SLABS_PER_LANE_BLOCK):
                slab2_ref[base + s, c4 * n4:(c4 + 1) * n4, :] = pieces[s]
        for c4 in range(4):
            for j in range(4):
                pieces = gather_rows(slab2_ref, base, c4 * n4 + j, n4 // 4, 4)
                ref16[jb, c4 + 4 * j] = jnp.concatenate(pieces, axis=1).astype(BF16)

    q_scale = HEAD_DIM ** -0.5
    n_slots = slab_ref.shape[0] // SLABS_PER_LANE_BLOCK

    def store_kva():
        kva_ref[...] = proj(2).astype(BF16)

    def store_qa(j):
        qa_ref[j] = (proj(j) * q_scale).astype(BF16)

    def store_classes(col_block, scale, refs, jb, slot):
        z = proj(col_block)
        write_classes(z if scale == 1.0 else z * scale, refs, jb, slot % n_slots)

    projections = [store_kva]
    for j in range(2):
        projections += [
            functools.partial(store_qa, j),
            functools.partial(store_classes, 3 + j, q_scale, qb_refs, j, 3 * j),
            functools.partial(store_classes, 5 + j, 1.0, kb_refs, j, 3 * j + 1),
            functools.partial(store_classes, 7 + j, 1.0, vb_refs, j, 3 * j + 2),
        ]
    side_work = {1 + c: work for c, work in enumerate(projections)}
    f_ref[wr_slot] = _swiglu(xn_ref, wgu_ref, wd_ref, act_ref, side_work)


def _stage1(x2, g_pre, wgu, wd, g_post, g_attn, w_in, b_in, batch, seq):
    tokens, d_model = x2.shape
    tm = TOKEN_TILE
    tiles_per_seq = seq // tm
    d_ff = wd.shape[0]
    n_blk = 2
    n_dil = len(DILATIONS)
    n_tiles = tokens // tm
    assert DILATIONS == (1, 4, 16) and tm % (16 * 16) == 0

    def lagged(i):
        j = jnp.maximum(i - 1, 0)
        return j // tiles_per_seq, j % tiles_per_seq

    def class_spec(dil):
        return pl.BlockSpec((None, n_blk, dil, tm // dil, LANE_BLOCK),
                            lambda i: (lagged(i)[0], 0, 0, lagged(i)[1], 0))

    def class_shape(dil):
        return jax.ShapeDtypeStruct((batch, n_blk, dil, seq // dil, LANE_BLOCK), BF16)

    row_spec = pl.BlockSpec((tm, d_model), lambda i: (jnp.minimum(i, n_tiles - 1), 0))
    lagged_row_spec = pl.BlockSpec((tm, d_model), lambda i: (jnp.maximum(i - 1, 0), 0))
    hbm_spec = pl.BlockSpec(memory_space=pl.ANY)
    outs = pl.pallas_call(
        _stage1_kernel,
        grid=(n_tiles + 1,),
        in_specs=[
            row_spec,
            lagged_row_spec,
            _const_spec((1, d_model)),
            hbm_spec,
            hbm_spec,
            _const_spec((1, d_model)),
            _const_spec((1, d_model)),
            hbm_spec,
            _const_spec(b_in.shape),
        ],
        out_specs=[
            lagged_row_spec,
            pl.BlockSpec((None, n_blk, tm, LANE_BLOCK),
                         lambda i: (lagged(i)[0], 0, lagged(i)[1], 0)),
            pl.BlockSpec((None, tm, LANE_BLOCK),
                         lambda i: (lagged(i)[0], lagged(i)[1], 0)),
        ] + [class_spec(d) for _ in range(3) for d in DILATIONS],
        out_shape=[
            jax.ShapeDtypeStruct((tokens, d_model), F32),
            jax.ShapeDtypeStruct((batch, n_blk, seq, LANE_BLOCK), BF16),
            jax.ShapeDtypeStruct((batch, seq, LANE_BLOCK), BF16),
        ] + [class_shape(d) for _ in range(3) for d in DILATIONS],
        scratch_shapes=[
            pltpu.VMEM((tm, d_model), BF16),
            pltpu.VMEM((tm, d_ff), BF16),
            pltpu.VMEM((2, tm, d_model), F32),
            pltpu.VMEM((tm, d_model), BF16),
            pltpu.VMEM((STAGING_SLOTS * SLABS_PER_LANE_BLOCK, tm, SLAB), F32),
            pltpu.VMEM((STAGING_SLOTS * SLABS_PER_LANE_BLOCK, tm, SLAB), F32),
            pltpu.VMEM(wgu.shape, BF16),
            pltpu.VMEM(wd.shape, BF16),
            pltpu.VMEM(w_in.shape, BF16),
            pltpu.SemaphoreType.DMA((2,)),
        ],
        compiler_params=pltpu.CompilerParams(dimension_semantics=("arbitrary",),
                                             vmem_limit_bytes=VMEM_LIMIT_TOKEN_STAGES),
        name="ffn1_inproj",
    )(x2, x2, g_pre, wgu, wd, g_post, g_attn, w_in, b_in)
    h1, qa, kva = outs[:3]
    qb, kb, vb = outs[3:3 + n_dil], outs[3 + n_dil:3 + 2 * n_dil], outs[3 + 2 * n_dil:]
    return h1, qa, kva, qb, kb, vb


def _lane_is_low(shape):
    return lax.broadcasted_iota(jnp.int32, shape, 1) < HEAD_DIM


def _masked_kv(k_grp, v_grp):
    low = _lane_is_low(k_grp.shape)
    zero = jnp.zeros_like(k_grp)
    lane = lax.broadcasted_iota(jnp.int32, k_grp.shape, 1)
    ones_low = jnp.clip(HEAD_DIM - lane, 0, 1).astype(F32).astype(BF16)
    ones_high = jnp.clip(lane - (HEAD_DIM - 1), 0, 1).astype(F32).astype(BF16)
    k_par = (jnp.where(low, k_grp, zero), jnp.where(low, zero, k_grp))
    v_par = (jnp.concatenate([jnp.where(low, v_grp, zero), ones_low], axis=1),
             jnp.concatenate([jnp.where(low, zero, v_grp), ones_high], axis=1))
    return k_par, v_par


def _pair_attention(q, k_par, v_par, biases):
    pair = 2 * HEAD_DIM
    acc = None
    maxes = []
    for par in range(2):
        s = lax.dot_general(q, k_par[par], (((1,), (1,)), ((), ())),
                            preferred_element_type=F32) + biases[par]
        m = jnp.max(s, axis=-1, keepdims=True)
        p = jnp.exp(s - m).astype(BF16)
        pv = jnp.dot(p, v_par[par], preferred_element_type=F32)
        acc = pv if acc is None else acc + pv
        maxes.append(m)
    m_pair = jnp.where(_lane_is_low((q.shape[0], pair)), maxes[0], maxes[1])
    return acc[:, :pair], acc[:, pair:], m_pair


def _attn_b_kernel(q_ref, kp_ref, kc_ref, vp_ref, vc_ref, bias_ref, o_ref, lse_ref):
    first_chunk = pl.program_id(2) == 0
    n_blk, n_cls, rows, _ = q_ref.shape
    pair = 2 * HEAD_DIM
    for cls in range(n_cls):
        for jb in range(n_blk):
            k_all = jnp.concatenate([kp_ref[jb, cls], kc_ref[jb, cls]], axis=0)
            v_all = jnp.concatenate([vp_ref[jb, cls], vc_ref[jb, cls]], axis=0)
            for grp in range(LANE_BLOCK // pair):
                lanes = slice(grp * pair, (grp + 1) * pair)
                k_par, v_par = _masked_kv(k_all[:, lanes], v_all[:, lanes])
                for t in range(rows // ATTN_BLOCK):
                    q = q_ref[jb, cls, t * ATTN_BLOCK:(t + 1) * ATTN_BLOCK, lanes]
                    keys = slice(t * ATTN_BLOCK, (t + 2) * ATTN_BLOCK)
                    variant = jnp.where(first_chunk, 0, 1) if t == 0 else 1
                    head = jb * HEADS_PER_LANE_BLOCK + grp * 2
                    biases = [bias_ref[(head + par) * 2 + variant] for par in range(2)]
                    o, den, m = _pair_attention(q, [k[keys] for k in k_par],
                                                [v[keys] for v in v_par], biases)
                    out_lanes = slice(jb * LANE_BLOCK + grp * pair,
                                      jb * LANE_BLOCK + (grp + 1) * pair)
                    out_rows = slice(t * ATTN_BLOCK, (t + 1) * ATTN_BLOCK)
                    o_ref[cls, out_rows, out_lanes] = (o / den).astype(BF16)
                    lse_ref[cls, out_rows, out_lanes] = m + jnp.log(den)


def _attn_b(qb, kb, vb, bias):
    batch, n_blk, n_cls, class_len, _ = qb.shape
    rows = min(ATTN_ROWS, class_len)
    cls_per_step = min(n_cls, ATTN_ROWS // rows)
    prev_per_chunk = rows // ATTN_BLOCK

    cur_spec = pl.BlockSpec((None, n_blk, cls_per_step, rows, LANE_BLOCK),
                            lambda b, c, i: (b, 0, c, i, 0))
    prev_spec = pl.BlockSpec(
        (None, n_blk, cls_per_step, ATTN_BLOCK, LANE_BLOCK),
        lambda b, c, i: (b, 0, c, jnp.maximum(i * prev_per_chunk - 1, 0), 0))
    width = n_blk * LANE_BLOCK
    out_spec = pl.BlockSpec((None, cls_per_step, rows, width), lambda b, c, i: (b, c, i, 0))
    return pl.pallas_call(
        _attn_b_kernel,
        grid=(batch, n_cls // cls_per_step, class_len // rows),
        in_specs=[cur_spec, prev_spec, cur_spec, prev_spec, cur_spec,
                  _const_spec(bias.shape)],
        out_specs=[out_spec, out_spec],
        out_shape=[jax.ShapeDtypeStruct((batch, n_cls, class_len, width), BF16),
                   jax.ShapeDtypeStruct((batch, n_cls, class_len, width), F32)],
        compiler_params=pltpu.CompilerParams(
            dimension_semantics=("arbitrary", "arbitrary", "arbitrary"),
            vmem_limit_bytes=VMEM_LIMIT_ATTENTION),
        name=f"mixer_b_dil{n_cls}",
    )(qb, kb, kb, vb, vb, bias)


def _attn_a_kernel(sink_ref, q_ref, kvp_ref, kvc_ref, bias_ref, o_ref):
    first_chunk = pl.program_id(1) == 0
    rows = q_ref.shape[1]
    pair = 2 * HEAD_DIM
    low_out = _lane_is_low((ATTN_BLOCK, pair))
    kv_all = jnp.concatenate([kvp_ref[...], kvc_ref[...]], axis=0)
    k_nat, v_nat = kv_all[:, :pair], kv_all[:, pair:]

    def both_halves(x, g):
        half = x[:, g * HEAD_DIM:(g + 1) * HEAD_DIM]
        return jnp.concatenate([half, half], axis=1)

    for g in range(A_KV_HEADS):
        k_par, v_par = _masked_kv(both_halves(k_nat, g), both_halves(v_nat, g))
        for grp in range(LANE_BLOCK // pair):
            lanes = slice(grp * pair, (grp + 1) * pair)
            head = g * HEADS_PER_LANE_BLOCK + grp * 2
            sink = jnp.where(low_out, sink_ref[head], sink_ref[head + 1])
            for t in range(rows // ATTN_BLOCK):
                q = q_ref[g, t * ATTN_BLOCK:(t + 1) * ATTN_BLOCK, lanes]
                keys = slice(t * ATTN_BLOCK, (t + 2) * ATTN_BLOCK)
                variant = jnp.where(first_chunk, 0, 1) if t == 0 else 1
                biases = [bias_ref[(head + par) * 2 + variant] for par in range(2)]
                o, den, m = _pair_attention(q, [k[keys] for k in k_par],
                                            [v[keys] for v in v_par], biases)
                total = den + jnp.exp(sink - m)
                out_lanes = slice(g * LANE_BLOCK + grp * pair, g * LANE_BLOCK + (grp + 1) * pair)
                o_ref[t * ATTN_BLOCK:(t + 1) * ATTN_BLOCK, out_lanes] = (o / total).astype(BF16)


def _attn_a(qa, kva, bias, sinks):
    batch, n_blk, seq, _ = qa.shape
    rows = min(ATTN_ROWS, seq)
    prev_per_chunk = rows // ATTN_BLOCK
    width = n_blk * LANE_BLOCK
    out = pl.pallas_call(
        _attn_a_kernel,
        grid=(batch, seq // rows),
        in_specs=[
            pl.BlockSpec(memory_space=pltpu.SMEM),
            pl.BlockSpec((None, n_blk, rows, LANE_BLOCK), lambda b, i: (b, 0, i, 0)),
            pl.BlockSpec((None, ATTN_BLOCK, LANE_BLOCK),
                         lambda b, i: (b, jnp.maximum(i * prev_per_chunk - 1, 0), 0)),
            pl.BlockSpec((None, rows, LANE_BLOCK), lambda b, i: (b, i, 0)),
            _const_spec(bias.shape),
        ],
        out_specs=pl.BlockSpec((None, rows, width), lambda b, i: (b, i, 0)),
        out_shape=jax.ShapeDtypeStruct((batch, seq, width), BF16),
        compiler_params=pltpu.CompilerParams(
            dimension_semantics=("arbitrary", "arbitrary"),
            vmem_limit_bytes=VMEM_LIMIT_ATTENTION),
        name="mixer_a",
    )(sinks, qa, kva, kva, bias)
    return out.reshape(batch * seq, width)


def _stage3_kernel(*refs):
    n_dil = len(DILATIONS)
    h_ref, mixa_ref = refs[:2]
    o_refs = refs[2:2 + n_dil]
    l_refs = refs[2 + n_dil:2 + 2 * n_dil]
    (p_ref, wout_hbm, bout_ref, g_attn_post_ref, g_pre_ref, wgu_hbm, wd_hbm, g_post_ref,
     g_ple_pre_ref, wgate_hbm, wproj_hbm, g_ple_post_ref,
     out_ref, xn_ref, act_ref, mixb_ref, slab_ref, mid_ref, att_ref, f_ref,
     wout_ref, wgu_ref, wd_ref, wgate_ref, wproj_ref, stage_ref, sem_ref) = refs[2 + 2 * n_dil:]
    tm = h_ref.shape[0]
    n_slabs = mixa_ref.shape[1] // SLAB

    @pl.when(pl.program_id(0) == 0)
    def _():
        for src, dst in ((wout_hbm, wout_ref), (wgu_hbm, wgu_ref), (wd_hbm, wd_ref),
                         (wgate_hbm, wgate_ref), (wproj_hbm, wproj_ref)):
            _load_as_bf16(src, dst, stage_ref, sem_ref)

    d_a = mixa_ref.shape[1]
    half = tm // 2
    quarter = half // 4

    def sub_chunks(r0, n):
        return [slice(r, r + ROW_CHUNK) for r in range(r0, r0 + n, ROW_CHUNK)]

    def interleave_merge(s, r0):
        lanes = slice(s * SLAB, (s + 1) * SLAB)
        for k, dil in enumerate(d for d in DILATIONS if d != 1):
            idx = DILATIONS.index(dil)
            n = half // dil
            for src_ref, which in ((o_refs[idx], 0), (l_refs[idx], 1)):
                dst = (2 * k + which) * n_slabs + s
                if dil == 4:
                    sub = 2 * ROW_CHUNK
                    for c in range(dil):
                        for u0 in range(r0 // dil, r0 // dil + n, sub):
                            blk = src_ref[c, u0:u0 + sub, lanes].astype(F32)
                            slab_ref[dst, pl.ds(c + u0 * dil, sub, stride=dil), :] = blk
                else:
                    mid = which * n_slabs + s
                    n4 = half // 4
                    for c4 in range(4):
                        for j in range(4):
                            blk = src_ref[c4 + 4 * j, r0 // dil:r0 // dil + n, lanes].astype(F32)
                            mid_ref[mid, pl.ds(c4 * n4 + j, n, stride=4), :] = blk
                        for w0 in range(0, n4, 2 * ROW_CHUNK):
                            piece = mid_ref[mid, c4 * n4 + w0:c4 * n4 + w0 + 2 * ROW_CHUNK, :]
                            slab_ref[dst, pl.ds(r0 + c4 + 4 * w0, 2 * ROW_CHUNK, stride=4), :] = piece
        for q0 in range(r0, r0 + half, 2 * ROW_CHUNK):
            rows = slice(q0, q0 + 2 * ROW_CHUNK)
            outs, lses = [], []
            k = 0
            for idx, dil in enumerate(DILATIONS):
                if dil == 1:
                    outs.append(o_refs[idx][0, rows, lanes].astype(F32))
                    lses.append(l_refs[idx][0, rows, lanes])
                else:
                    outs.append(slab_ref[2 * k * n_slabs + s, rows, :])
                    lses.append(slab_ref[(2 * k + 1) * n_slabs + s, rows, :])
                    k += 1
            l_max = lses[0]
            for l in lses[1:]:
                l_max = jnp.maximum(l_max, l)
            weights = [jnp.exp(l - l_max) for l in lses]
            num = weights[0] * outs[0]
            den = weights[0]
            for w, o in zip(weights[1:], outs[1:]):
                num = num + w * o
                den = den + w
            mixb_ref[rows, lanes] = (num / den).astype(BF16)

    def out_projection(part, r0):
        rows = slice(r0, r0 + half)
        if part == 0:
            att_ref[rows] = (jnp.dot(mixa_ref[rows], wout_ref[:d_a, :], preferred_element_type=F32)
                             + bout_ref[...])
        else:
            lo = (part - 1) * 2 * SLAB
            hi = lo + 2 * SLAB
            att_ref[rows] = att_ref[rows] + jnp.dot(
                mixb_ref[rows, lo:hi], wout_ref[d_a + lo:d_a + hi, :], preferred_element_type=F32)

    def attention_residual(r0, n):
        for rows in sub_chunks(r0, n):
            h2 = h_ref[rows] + _rms(att_ref[rows], g_attn_post_ref[...])
            out_ref[rows] = h2
            xn_ref[rows] = _rms(h2, g_pre_ref[...]).astype(BF16)

    def head_pieces(r0):
        return [functools.partial(interleave_merge, 0, r0),
                functools.partial(interleave_merge, 1, r0),
                functools.partial(out_projection, 0, r0),
                functools.partial(out_projection, 1, r0),
                functools.partial(interleave_merge, 2, r0),
                functools.partial(interleave_merge, 3, r0),
                functools.partial(out_projection, 2, r0)] + [
                    functools.partial(attention_residual, r0 + j * quarter, quarter)
                    for j in range(4)]

    def ffn_residual(r0, n):
        for rows in sub_chunks(r0, n):
            h3 = out_ref[rows] + 0.5 * _rms(f_ref[rows], g_post_ref[...])
            out_ref[rows] = h3
            xn_ref[rows] = _rms(h3, g_ple_pre_ref[...]).astype(BF16)

    def gate(r0):
        rows = slice(r0, r0 + half)
        att_ref[rows] = _sigmoid(
            jnp.dot(xn_ref[rows], wgate_ref[...], preferred_element_type=F32))

    def embedding(r0):
        rows = slice(r0, r0 + half)
        att_ref[rows] = att_ref[rows] * jnp.dot(
            p_ref[rows].astype(BF16), wproj_ref[...], preferred_element_type=F32)

    def embedding_residual(r0, n):
        for rows in sub_chunks(r0, n):
            out_ref[rows] = out_ref[rows] + _rms(att_ref[rows], g_ple_post_ref[...])

    def tail_pieces(r0):
        return ([functools.partial(ffn_residual, r0 + j * quarter, quarter) for j in range(4)]
                + [functools.partial(gate, r0), functools.partial(embedding, r0)]
                + [functools.partial(embedding_residual, r0 + j * quarter, quarter)
                   for j in range(4)])

    def ffn(r0, pieces):
        rows = slice(r0, r0 + half)
        f_ref[rows] = _swiglu(xn_ref.at[rows], wgu_ref, wd_ref, act_ref, dict(enumerate(pieces)))

    for piece in head_pieces(0):
        piece()
    ffn(0, head_pieces(half))
    ffn(half, tail_pieces(0))
    for piece in tail_pieces(half):
        piece()


def _stage3(h1, mix_a, outs_b, lses_b, p2, w_out, b_out, g_attn_post, g_pre, wgu, wd, g_post,
            g_ple_pre, w_gate, w_proj, g_ple_post, seq):
    tokens, d_model = h1.shape
    tm = TOKEN_TILE
    tiles_per_seq = seq // tm
    d_ff = wd.shape[0]
    half = mix_a.shape[1]
    n_dil = len(DILATIONS)

    def rows(width):
        return pl.BlockSpec((tm, width), lambda i: (i, 0))

    def class_spec(dil):
        return pl.BlockSpec((None, dil, tm // dil, half),
                            lambda i: (i // tiles_per_seq, 0, i % tiles_per_seq, 0))

    vec = _const_spec((1, d_model))
    hbm = pl.BlockSpec(memory_space=pl.ANY)
    return pl.pallas_call(
        _stage3_kernel,
        grid=(tokens // tm,),
        in_specs=[rows(d_model), rows(half)]
        + [class_spec(d) for d in DILATIONS] * 2
        + [
            rows(p2.shape[1]),
            hbm, vec, vec, vec,
            hbm, hbm, vec, vec,
            hbm, hbm, vec,
        ],
        out_specs=rows(d_model),
        out_shape=jax.ShapeDtypeStruct((tokens, d_model), F32),
        scratch_shapes=[
            pltpu.VMEM((tm, d_model), BF16),
            pltpu.VMEM((tm // 2, d_ff), BF16),
            pltpu.VMEM((tm, half), BF16),
            pltpu.VMEM((2 * (n_dil - 1) * (half // SLAB), tm, SLAB), F32),
            pltpu.VMEM((2 * (half // SLAB), tm // 2, SLAB), F32),
            pltpu.VMEM((tm, d_model), F32),
            pltpu.VMEM((tm, d_model), F32),
            pltpu.VMEM(w_out.shape, BF16),
            pltpu.VMEM(wgu.shape, BF16),
            pltpu.VMEM(wd.shape, BF16),
            pltpu.VMEM(w_gate.shape, BF16),
            pltpu.VMEM(w_proj.shape, BF16),
            pltpu.VMEM((2, WEIGHT_STAGE_ROWS, d_model), F32),
            pltpu.SemaphoreType.DMA((2,)),
        ],
        compiler_params=pltpu.CompilerParams(dimension_semantics=("arbitrary",),
                                             vmem_limit_bytes=VMEM_LIMIT_TOKEN_STAGES),
        name="outproj_ffn2_ple",
    )(h1, mix_a, *outs_b, *lses_b, p2, w_out, b_out, g_attn_post, g_pre, wgu, wd, g_post,
      g_ple_pre, w_gate, w_proj, g_ple_post)


def kernel(x, p, rel_bias, ffn1_pre_g, ffn1_w_gu, ffn1_w_down, ffn1_post_g, attn_pre_g, w_in, b_in,
           sinks, w_out, b_out, attn_post_g, ffn2_pre_g, ffn2_w_gu, ffn2_w_down, ffn2_post_g,
           ple_pre_g, w_ple_gate, w_ple_proj, ple_post_g):
    batch, seq, d_model = x.shape
    depth = p.shape[0]
    assert seq % (max(DILATIONS) * ATTN_BLOCK) == 0 and seq % TOKEN_TILE == 0
    assert x.dtype == F32

    bias = _bias_tables(rel_bias)
    h = x.reshape(batch * seq, d_model)
    for i in range(depth):
        row = lambda v: v[i].reshape(1, -1)
        h1, qa, kva, qb, kb, vb = _stage1(
            h, row(ffn1_pre_g), ffn1_w_gu[i], ffn1_w_down[i],
            row(ffn1_post_g), row(attn_pre_g), w_in[i], row(b_in), batch, seq)
        mix_a = _attn_a(qa, kva, bias[0], sinks[i])
        outs_b, lses_b = [], []
        for pat in range(len(DILATIONS)):
            o, lse = _attn_b(qb[pat], kb[pat], vb[pat], bias[1 + pat])
            outs_b.append(o)
            lses_b.append(lse)
        h = _stage3(
            h1, mix_a, outs_b, lses_b, p[i].reshape(batch * seq, -1),
            w_out[i], row(b_out), row(attn_post_g), row(ffn2_pre_g),
            ffn2_w_gu[i], ffn2_w_down[i], row(ffn2_post_g),
            row(ple_pre_g), w_ple_gate[i], w_ple_proj[i],
            row(ple_post_g), seq)
    return h.reshape(batch, seq, d_model)
```

```python
import functools
import math

import numpy as np
import jax
import jax.numpy as jnp
from jax import lax
from jax.experimental import pallas as pl
from jax.experimental.pallas import tpu as pltpu

HEAD_DIM = 64
A_Q_HEADS = 8
A_KV_HEADS = 2
A_WINDOW = 128
B_PATTERNS = ((128, 1), (512, 4), (2048, 16))
DILATIONS = tuple(d for _, d in B_PATTERNS)
NUM_BUCKETS = 32
MAX_DISTANCE = 2048
EPS = 1e-6
NEG_INF = -1e30

ATTN_BLOCK = 128
LANE_BLOCK = 256
HEADS_PER_LANE_BLOCK = LANE_BLOCK // HEAD_DIM
SLAB = 128
SLABS_PER_LANE_BLOCK = LANE_BLOCK // SLAB
STAGING_SLOTS = 2
FF_CHUNK = 256
TOKEN_TILE = 512
ROW_CHUNK = 16
WEIGHT_STAGE_ROWS = 512
ATTN_ROWS = 2048
V7X_VMEM_BYTES = 64 * 1024 * 1024
VMEM_LIMIT_TOKEN_STAGES = V7X_VMEM_BYTES - 6 * 1024 * 1024
VMEM_LIMIT_ATTENTION = V7X_VMEM_BYTES - 24 * 1024 * 1024

F32 = jnp.float32
BF16 = jnp.bfloat16


def _rms(x, g):
    return x * lax.rsqrt(jnp.mean(x * x, axis=-1, keepdims=True) + EPS) * g


def _row_chunks(n_rows):
    return [slice(r, r + ROW_CHUNK) for r in range(0, n_rows, ROW_CHUNK)]


def _sigmoid(x):
    return 1.0 / (1.0 + jnp.exp(-x))


def _swiglu(xn_ref, wgu_ref, wd_ref, act_ref, side_work=None):
    d_ff = wd_ref.shape[0]
    side_work = dict(side_work or {})
    for c in range(d_ff // FF_CHUNK):
        lo = c * FF_CHUNK
        g = jnp.dot(xn_ref[...], wgu_ref[:, lo:lo + FF_CHUNK], preferred_element_type=F32)
        u = jnp.dot(xn_ref[...], wgu_ref[:, d_ff + lo:d_ff + lo + FF_CHUNK],
                    preferred_element_type=F32)
        act_ref[:, lo:lo + FF_CHUNK] = ((g * _sigmoid(g)) * u).astype(BF16)
        if c in side_work:
            side_work.pop(c)()
    assert not side_work
    return jnp.dot(act_ref[...], wd_ref[...], preferred_element_type=F32)


def _load_as_bf16(src_hbm, dst_ref, stage_ref, sem_ref):
    s_rows, s_cols = stage_ref.shape[1:]
    rows, cols = src_hbm.shape
    pieces = [(r, min(s_rows, rows - r), c, min(s_cols, cols - c))
              for r in range(0, rows, s_rows) for c in range(0, cols, s_cols)]

    def copy(i):
        r, nr, c, nc = pieces[i]
        return pltpu.make_async_copy(src_hbm.at[pl.ds(r, nr), pl.ds(c, nc)],
                                     stage_ref.at[i % 2, pl.ds(0, nr), pl.ds(0, nc)],
                                     sem_ref.at[i % 2])

    copy(0).start()
    for i, (r, nr, c, nc) in enumerate(pieces):
        if i + 1 < len(pieces):
            copy(i + 1).start()
        copy(i).wait()
        dst_ref[r:r + nr, c:c + nc] = stage_ref[i % 2, :nr, :nc].astype(BF16)


def _const_spec(shape):
    return pl.BlockSpec(shape, lambda *_: (0,) * len(shape), pipeline_mode=pl.Buffered(1))


def _t5_bucket_np(dist):
    max_exact = NUM_BUCKETS // 2
    n = np.maximum(dist, 0)
    nf = np.maximum(n, 1).astype(np.float32)
    large = max_exact + (np.log(nf / np.float32(max_exact))
                         / np.float32(math.log(MAX_DISTANCE / max_exact))
                         * np.float32(NUM_BUCKETS - max_exact)).astype(np.int32)
    large = np.minimum(large, NUM_BUCKETS - 1)
    return np.where(n < max_exact, n, large).astype(np.int32)


def _bucket_tables():
    q = np.arange(ATTN_BLOCK)[:, None]
    k = np.arange(2 * ATTN_BLOCK)[None, :]
    rel = ATTN_BLOCK + q - k
    tables = []
    for max_dist, stride in ((A_WINDOW - 1, 1),) + tuple((w // d, d) for w, d in B_PATTERNS):
        valid = (rel >= 0) & (rel <= max_dist)
        bkt = _t5_bucket_np(rel * stride)
        later = np.where(valid, bkt, -1)
        first = np.where(valid & (k >= ATTN_BLOCK), bkt, -1)
        tables.append(np.stack([first, later]))
    return np.stack(tables).astype(np.int32)


def _bias_kernel(buckets_present, rb_ref, bkt_ref, out_ref):
    head = pl.program_id(0)
    first_block_keys = lax.broadcasted_iota(jnp.int32, bkt_ref.shape[2:], 1) >= ATTN_BLOCK
    for pattern, present in enumerate(buckets_present):
        col = head if pattern == 0 else head + A_Q_HEADS
        bkt = bkt_ref[pattern, 1]
        later = jnp.full(bkt.shape, NEG_INF, F32)
        for b in present:
            later = jnp.where(bkt == b, rb_ref[b, col], later)
        out_ref[pattern, 1] = later
        out_ref[pattern, 0] = jnp.where(first_block_keys, later, NEG_INF)


def _bias_tables(rel_bias):
    bkt_np = _bucket_tables()
    assert (bkt_np[:, 0] == np.where(np.arange(2 * ATTN_BLOCK) >= ATTN_BLOCK,
                                     bkt_np[:, 1], -1)).all()
    buckets_present = tuple(tuple(int(b) for b in np.unique(t[1]) if b >= 0) for t in bkt_np)
    n_pat = bkt_np.shape[0]
    table = (2, ATTN_BLOCK, 2 * ATTN_BLOCK)
    out = pl.pallas_call(
        functools.partial(_bias_kernel, buckets_present),
        grid=(A_Q_HEADS,),
        in_specs=[
            pl.BlockSpec(memory_space=pltpu.SMEM),
            pl.BlockSpec((n_pat,) + table, lambda h: (0, 0, 0, 0)),
        ],
        out_specs=pl.BlockSpec((n_pat, None) + table, lambda h: (0, h, 0, 0, 0)),
        out_shape=jax.ShapeDtypeStruct((n_pat, A_Q_HEADS) + table, F32),
        compiler_params=pltpu.CompilerParams(dimension_semantics=("arbitrary",)),
        name="bias_tables",
    )(rel_bias, jnp.asarray(bkt_np))
    return out.reshape(n_pat, A_Q_HEADS * 2, ATTN_BLOCK, 2 * ATTN_BLOCK)


def _stage1_kernel(x_ref, xprev_ref, g_pre_ref, wgu_hbm, wd_hbm, g_post_ref, g_attn_ref, win_hbm,
                   bin_ref, h_ref, qa_ref, kva_ref, *rest):
    n_dil = len(DILATIONS)
    qb_refs, kb_refs, vb_refs = rest[:n_dil], rest[n_dil:2 * n_dil], rest[2 * n_dil:3 * n_dil]
    (xn_ref, act_ref, f_ref, xn2_ref, slab_ref, slab2_ref,
     wgu_ref, wd_ref, win_ref, sem_ref) = rest[3 * n_dil:]
    tm = x_ref.shape[0]
    step = pl.program_id(0)
    rd_slot = (step + 1) % 2
    wr_slot = step % 2

    @pl.when(step == 0)
    def _():
        _load_as_bf16(wgu_hbm, wgu_ref, f_ref, sem_ref)
        _load_as_bf16(wd_hbm, wd_ref, f_ref, sem_ref)
        _load_as_bf16(win_hbm, win_ref, f_ref, sem_ref)
        f_ref[1] = jnp.zeros(f_ref.shape[1:], F32)

    for rows in _row_chunks(tm):
        xn_ref[rows] = _rms(x_ref[rows], g_pre_ref[...]).astype(BF16)
    for rows in _row_chunks(tm):
        h = xprev_ref[rows] + 0.5 * _rms(f_ref[rd_slot, rows], g_post_ref[...])
        h_ref[rows] = h
        xn2_ref[rows] = _rms(h, g_attn_ref[...]).astype(BF16)

    def proj(col_block):
        lo = col_block * LANE_BLOCK
        return (jnp.dot(xn2_ref[...], win_ref[:, lo:lo + LANE_BLOCK], preferred_element_type=F32)
                + bin_ref[:, lo:lo + LANE_BLOCK])

    def gather_rows(src_ref, base, start, n, stride):
        return [src_ref[base + s, pl.ds(start, n, stride=stride), :]
                for s in range(SLABS_PER_LANE_BLOCK)]

    def write_classes(z, refs, jb, slot):
        ref1, ref4, ref16 = refs
        base = slot * SLABS_PER_LANE_BLOCK
        ref1[jb, 0] = z.astype(BF16)
        for s in range(SLABS_PER_LANE_BLOCK):
            slab_ref[base + s] = z[:, s * SLAB:(s + 1) * SLAB]
        n4 = tm // 4
        for c4 in range(4):
            pieces = gather_rows(slab_ref, base, c4, n4, 4)
            ref4[jb, c4] = jnp.concatenate(pieces, axis=1).astype(BF16)
            for s in range(SLABS_PER_LANE_BLOCK):
                slab2_ref[base + s, c4 * n4:(c4 + 1) * n4, :] = pieces[s]
        for c4 in range(4):
            for j in range(4):
                pieces = gather_rows(slab2_ref, base, c4 * n4 + j, n4 // 4, 4)
                ref16[jb, c4 + 4 * j] = jnp.concatenate(pieces, axis=1).astype(BF16)

    q_scale = HEAD_DIM ** -0.5
    n_slots = slab_ref.shape[0] // SLABS_PER_LANE_BLOCK

    def store_kva():
        z = proj(2)
        heads = [z[:, h * HEAD_DIM:(h + 1) * HEAD_DIM] for h in range(LANE_BLOCK // HEAD_DIM)]
        kva_ref[...] = jnp.concatenate([h for h in heads for _ in range(2)], axis=1).astype(BF16)

    def store_qa(j):
        qa_ref[j] = (proj(j) * q_scale).astype(BF16)

    def store_classes(col_block, scale, refs, jb, slot):
        z = proj(col_block)
        write_classes(z if scale == 1.0 else z * scale, refs, jb, slot % n_slots)

    projections = [store_kva]
    for j in range(2):
        projections += [
            functools.partial(store_qa, j),
            functools.partial(store_classes, 3 + j, q_scale, qb_refs, j, 3 * j),
            functools.partial(store_classes, 5 + j, 1.0, kb_refs, j, 3 * j + 1),
            functools.partial(store_classes, 7 + j, 1.0, vb_refs, j, 3 * j + 2),
        ]
    side_work = {1 + c: work for c, work in enumerate(projections)}
    f_ref[wr_slot] = _swiglu(xn_ref, wgu_ref, wd_ref, act_ref, side_work)


def _stage1(x2, g_pre, wgu, wd, g_post, g_attn, w_in, b_in, batch, seq):
    tokens, d_model = x2.shape
    tm = TOKEN_TILE
    tiles_per_seq = seq // tm
    d_ff = wd.shape[0]
    n_blk = 2
    n_dil = len(DILATIONS)
    n_tiles = tokens // tm
    assert DILATIONS == (1, 4, 16) and tm % (16 * 16) == 0

    def lagged(i):
        j = jnp.maximum(i - 1, 0)
        return j // tiles_per_seq, j % tiles_per_seq

    def class_spec(dil):
        return pl.BlockSpec((None, n_blk, dil, tm // dil, LANE_BLOCK),
                            lambda i: (lagged(i)[0], 0, 0, lagged(i)[1], 0))

    def class_shape(dil):
        return jax.ShapeDtypeStruct((batch, n_blk, dil, seq // dil, LANE_BLOCK), BF16)

    row_spec = pl.BlockSpec((tm, d_model), lambda i: (jnp.minimum(i, n_tiles - 1), 0))
    lagged_row_spec = pl.BlockSpec((tm, d_model), lambda i: (jnp.maximum(i - 1, 0), 0))
    hbm_spec = pl.BlockSpec(memory_space=pl.ANY)
    outs = pl.pallas_call(
        _stage1_kernel,
        grid=(n_tiles + 1,),
        in_specs=[
            row_spec,
            lagged_row_spec,
            _const_spec((1, d_model)),
            hbm_spec,
            hbm_spec,
            _const_spec((1, d_model)),
            _const_spec((1, d_model)),
            hbm_spec,
            _const_spec(b_in.shape),
        ],
        out_specs=[
            lagged_row_spec,
            pl.BlockSpec((None, n_blk, tm, LANE_BLOCK),
                         lambda i: (lagged(i)[0], 0, lagged(i)[1], 0)),
            pl.BlockSpec((None, tm, 2 * LANE_BLOCK),
                         lambda i: (lagged(i)[0], lagged(i)[1], 0)),
        ] + [class_spec(d) for _ in range(3) for d in DILATIONS],
        out_shape=[
            jax.ShapeDtypeStruct((tokens, d_model), F32),
            jax.ShapeDtypeStruct((batch, n_blk, seq, LANE_BLOCK), BF16),
            jax.ShapeDtypeStruct((batch, seq, 2 * LANE_BLOCK), BF16),
        ] + [class_shape(d) for _ in range(3) for d in DILATIONS],
        scratch_shapes=[
            pltpu.VMEM((tm, d_model), BF16),
            pltpu.VMEM((tm, d_ff), BF16),
            pltpu.VMEM((2, tm, d_model), F32),
            pltpu.VMEM((tm, d_model), BF16),
            pltpu.VMEM((STAGING_SLOTS * SLABS_PER_LANE_BLOCK, tm, SLAB), F32),
            pltpu.VMEM((STAGING_SLOTS * SLABS_PER_LANE_BLOCK, tm, SLAB), F32),
            pltpu.VMEM(wgu.shape, BF16),
            pltpu.VMEM(wd.shape, BF16),
            pltpu.VMEM(w_in.shape, BF16),
            pltpu.SemaphoreType.DMA((2,)),
        ],
        compiler_params=pltpu.CompilerParams(dimension_semantics=("arbitrary",),
                                             vmem_limit_bytes=VMEM_LIMIT_TOKEN_STAGES),
        name="ffn1_inproj",
    )(x2, x2, g_pre, wgu, wd, g_post, g_attn, w_in, b_in)
    h1, qa, kva = outs[:3]
    qb, kb, vb = outs[3:3 + n_dil], outs[3 + n_dil:3 + 2 * n_dil], outs[3 + 2 * n_dil:]
    return h1, qa, kva, qb, kb, vb


def _lane_is_low(shape):
    return lax.broadcasted_iota(jnp.int32, shape, 1) < HEAD_DIM


def _masked_kv(k_grp, v_grp):
    low = _lane_is_low(k_grp.shape)
    zero = jnp.zeros_like(k_grp)
    lane = lax.broadcasted_iota(jnp.int32, k_grp.shape, 1)
    ones_low = jnp.clip(HEAD_DIM - lane, 0, 1).astype(F32).astype(BF16)
    ones_high = jnp.clip(lane - (HEAD_DIM - 1), 0, 1).astype(F32).astype(BF16)
    k_par = (jnp.where(low, k_grp, zero), jnp.where(low, zero, k_grp))
    v_par = (jnp.concatenate([jnp.where(low, v_grp, zero), ones_low], axis=1),
             jnp.concatenate([jnp.where(low, zero, v_grp), ones_high], axis=1))
    return k_par, v_par


def _pair_attention(q, k_par, v_par, biases):
    pair = 2 * HEAD_DIM
    acc = None
    maxes = []
    for par in range(2):
        s = lax.dot_general(q, k_par[par], (((1,), (1,)), ((), ())),
                            preferred_element_type=F32) + biases[par]
        m = jnp.max(s, axis=-1, keepdims=True)
        p = jnp.exp(s - m).astype(BF16)
        pv = jnp.dot(p, v_par[par], preferred_element_type=F32)
        acc = pv if acc is None else acc + pv
        maxes.append(m)
    m_pair = jnp.where(_lane_is_low((q.shape[0], pair)), maxes[0], maxes[1])
    return acc[:, :pair], acc[:, pair:], m_pair


def _attn_b_kernel(q_ref, kp_ref, kc_ref, vp_ref, vc_ref, bias_ref, o_ref, lse_ref):
    first_chunk = pl.program_id(2) == 0
    n_blk, n_cls, rows, _ = q_ref.shape
    pair = 2 * HEAD_DIM
    for cls in range(n_cls):
        for jb in range(n_blk):
            k_all = jnp.concatenate([kp_ref[jb, cls], kc_ref[jb, cls]], axis=0)
            v_all = jnp.concatenate([vp_ref[jb, cls], vc_ref[jb, cls]], axis=0)
            for grp in range(LANE_BLOCK // pair):
                lanes = slice(grp * pair, (grp + 1) * pair)
                k_par, v_par = _masked_kv(k_all[:, lanes], v_all[:, lanes])
                for t in range(rows // ATTN_BLOCK):
                    q = q_ref[jb, cls, t * ATTN_BLOCK:(t + 1) * ATTN_BLOCK, lanes]
                    keys = slice(t * ATTN_BLOCK, (t + 2) * ATTN_BLOCK)
                    variant = jnp.where(first_chunk, 0, 1) if t == 0 else 1
                    head = jb * HEADS_PER_LANE_BLOCK + grp * 2
                    biases = [bias_ref[(head + par) * 2 + variant] for par in range(2)]
                    o, den, m = _pair_attention(q, [k[keys] for k in k_par],
                                                [v[keys] for v in v_par], biases)
                    out_lanes = slice(jb * LANE_BLOCK + grp * pair,
                                      jb * LANE_BLOCK + (grp + 1) * pair)
                    out_rows = slice(t * ATTN_BLOCK, (t + 1) * ATTN_BLOCK)
                    o_ref[cls, out_rows, out_lanes] = (o / den).astype(BF16)
                    lse_ref[cls, out_rows, out_lanes] = m + jnp.log(den)


def _attn_b(qb, kb, vb, bias):
    batch, n_blk, n_cls, class_len, _ = qb.shape
    rows = min(ATTN_ROWS, class_len)
    cls_per_step = min(n_cls, ATTN_ROWS // rows)
    prev_per_chunk = rows // ATTN_BLOCK

    cur_spec = pl.BlockSpec((None, n_blk, cls_per_step, rows, LANE_BLOCK),
                            lambda b, c, i: (b, 0, c, i, 0))
    prev_spec = pl.BlockSpec(
        (None, n_blk, cls_per_step, ATTN_BLOCK, LANE_BLOCK),
        lambda b, c, i: (b, 0, c, jnp.maximum(i * prev_per_chunk - 1, 0), 0))
    width = n_blk * LANE_BLOCK
    out_spec = pl.BlockSpec((None, cls_per_step, rows, width), lambda b, c, i: (b, c, i, 0))
    return pl.pallas_call(
        _attn_b_kernel,
        grid=(batch, n_cls // cls_per_step, class_len // rows),
        in_specs=[cur_spec, prev_spec, cur_spec, prev_spec, cur_spec,
                  _const_spec(bias.shape)],
        out_specs=[out_spec, out_spec],
        out_shape=[jax.ShapeDtypeStruct((batch, n_cls, class_len, width), BF16),
                   jax.ShapeDtypeStruct((batch, n_cls, class_len, width), F32)],
        compiler_params=pltpu.CompilerParams(
            dimension_semantics=("arbitrary", "arbitrary", "arbitrary"),
            vmem_limit_bytes=VMEM_LIMIT_ATTENTION),
        name=f"mixer_b_dil{n_cls}",
    )(qb, kb, kb, vb, vb, bias)


def _attn_a_kernel(sink_ref, q_ref, kvp_ref, kvc_ref, bias_ref, o_ref):
    first_chunk = pl.program_id(1) == 0
    rows = q_ref.shape[1]
    pair = 2 * HEAD_DIM
    low_out = _lane_is_low((ATTN_BLOCK, pair))
    kv_all = jnp.concatenate([kvp_ref[...], kvc_ref[...]], axis=0)
    for g in range(A_KV_HEADS):
        k_par, v_par = _masked_kv(kv_all[:, g * pair:(g + 1) * pair],
                                  kv_all[:, (A_KV_HEADS + g) * pair:(A_KV_HEADS + g + 1) * pair])
        for grp in range(LANE_BLOCK // pair):
            lanes = slice(grp * pair, (grp + 1) * pair)
            head = g * HEADS_PER_LANE_BLOCK + grp * 2
            sink = jnp.where(low_out, sink_ref[head], sink_ref[head + 1])
            for t in range(rows // ATTN_BLOCK):
                q = q_ref[g, t * ATTN_BLOCK:(t + 1) * ATTN_BLOCK, lanes]
                keys = slice(t * ATTN_BLOCK, (t + 2) * ATTN_BLOCK)
                variant = jnp.where(first_chunk, 0, 1) if t == 0 else 1
                biases = [bias_ref[(head + par) * 2 + variant] for par in range(2)]
                o, den, m = _pair_attention(q, [k[keys] for k in k_par],
                                            [v[keys] for v in v_par], biases)
                total = den + jnp.exp(sink - m)
                out_lanes = slice(g * LANE_BLOCK + grp * pair, g * LANE_BLOCK + (grp + 1) * pair)
                o_ref[t * ATTN_BLOCK:(t + 1) * ATTN_BLOCK, out_lanes] = (o / total).astype(BF16)


def _attn_a(qa, kva, bias, sinks):
    batch, n_blk, seq, _ = qa.shape
    rows = min(ATTN_ROWS, seq)
    prev_per_chunk = rows // ATTN_BLOCK
    width = n_blk * LANE_BLOCK
    out = pl.pallas_call(
        _attn_a_kernel,
        grid=(batch, seq // rows),
        in_specs=[
            pl.BlockSpec(memory_space=pltpu.SMEM),
            pl.BlockSpec((None, n_blk, rows, LANE_BLOCK), lambda b, i: (b, 0, i, 0)),
            pl.BlockSpec((None, ATTN_BLOCK, kva.shape[2]),
                         lambda b, i: (b, jnp.maximum(i * prev_per_chunk - 1, 0), 0)),
            pl.BlockSpec((None, rows, kva.shape[2]), lambda b, i: (b, i, 0)),
            _const_spec(bias.shape),
        ],
        out_specs=pl.BlockSpec((None, rows, width), lambda b, i: (b, i, 0)),
        out_shape=jax.ShapeDtypeStruct((batch, seq, width), BF16),
        compiler_params=pltpu.CompilerParams(
            dimension_semantics=("arbitrary", "arbitrary"),
            vmem_limit_bytes=VMEM_LIMIT_ATTENTION),
        name="mixer_a",
    )(sinks, qa, kva, kva, bias)
    return out.reshape(batch * seq, width)


def _stage3_kernel(*refs):
    n_dil = len(DILATIONS)
    h_ref, mixa_ref = refs[:2]
    o_refs = refs[2:2 + n_dil]
    l_refs = refs[2 + n_dil:2 + 2 * n_dil]
    (p_ref, wout_hbm, bout_ref, g_attn_post_ref, g_pre_ref, wgu_hbm, wd_hbm, g_post_ref,
     g_ple_pre_ref, wgate_hbm, wproj_hbm, g_ple_post_ref,
     out_ref, xn_ref, act_ref, mixb_ref, slab_ref, mid_ref, att_ref, f_ref,
     wout_ref, wgu_ref, wd_ref, wgate_ref, wproj_ref, stage_ref, sem_ref) = refs[2 + 2 * n_dil:]
    tm = h_ref.shape[0]
    n_slabs = mixa_ref.shape[1] // SLAB

    @pl.when(pl.program_id(0) == 0)
    def _():
        for src, dst in ((wout_hbm, wout_ref), (wgu_hbm, wgu_ref), (wd_hbm, wd_ref),
                         (wgate_hbm, wgate_ref), (wproj_hbm, wproj_ref)):
            _load_as_bf16(src, dst, stage_ref, sem_ref)

    d_a = mixa_ref.shape[1]
    half = tm // 2
    quarter = half // 4

    def sub_chunks(r0, n):
        return [slice(r, r + ROW_CHUNK) for r in range(r0, r0 + n, ROW_CHUNK)]

    def interleave_merge(s, r0):
        lanes = slice(s * SLAB, (s + 1) * SLAB)
        for k, dil in enumerate(d for d in DILATIONS if d != 1):
            idx = DILATIONS.index(dil)
            n = half // dil
            for src_ref, which in ((o_refs[idx], 0), (l_refs[idx], 1)):
                dst = (2 * k + which) * n_slabs + s
                if dil == 4:
                    sub = 2 * ROW_CHUNK
                    for c in range(dil):
                        for u0 in range(r0 // dil, r0 // dil + n, sub):
                            blk = src_ref[c, u0:u0 + sub, lanes].astype(F32)
                            slab_ref[dst, pl.ds(c + u0 * dil, sub, stride=dil), :] = blk
                else:
                    mid = which * n_slabs + s
                    n4 = half // 4
                    for c4 in range(4):
                        for j in range(4):
                            blk = src_ref[c4 + 4 * j, r0 // dil:r0 // dil + n, lanes].astype(F32)
                            mid_ref[mid, pl.ds(c4 * n4 + j, n, stride=4), :] = blk
                        for w0 in range(0, n4, 2 * ROW_CHUNK):
                            piece = mid_ref[mid, c4 * n4 + w0:c4 * n4 + w0 + 2 * ROW_CHUNK, :]
                            slab_ref[dst, pl.ds(r0 + c4 + 4 * w0, 2 * ROW_CHUNK, stride=4), :] = piece
        for q0 in range(r0, r0 + half, 2 * ROW_CHUNK):
            rows = slice(q0, q0 + 2 * ROW_CHUNK)
            outs, lses = [], []
            k = 0
            for idx, dil in enumerate(DILATIONS):
                if dil == 1:
                    outs.append(o_refs[idx][0, rows, lanes].astype(F32))
                    lses.append(l_refs[idx][0, rows, lanes])
                else:
                    outs.append(slab_ref[2 * k * n_slabs + s, rows, :])
                    lses.append(slab_ref[(2 * k + 1) * n_slabs + s, rows, :])
                    k += 1
            l_max = lses[0]
            for l in lses[1:]:
                l_max = jnp.maximum(l_max, l)
            weights = [jnp.exp(l - l_max) for l in lses]
            num = weights[0] * outs[0]
            den = weights[0]
            for w, o in zip(weights[1:], outs[1:]):
                num = num + w * o
                den = den + w
            mixb_ref[rows, lanes] = (num / den).astype(BF16)

    def out_projection(part, r0):
        rows = slice(r0, r0 + half)
        if part == 0:
            att_ref[rows] = (jnp.dot(mixa_ref[rows], wout_ref[:d_a, :], preferred_element_type=F32)
                             + bout_ref[...])
        else:
            lo = (part - 1) * 2 * SLAB
            hi = lo + 2 * SLAB
            att_ref[rows] = att_ref[rows] + jnp.dot(
                mixb_ref[rows, lo:hi], wout_ref[d_a + lo:d_a + hi, :], preferred_element_type=F32)

    def attention_residual(r0, n):
        for rows in sub_chunks(r0, n):
            h2 = h_ref[rows] + _rms(att_ref[rows], g_attn_post_ref[...])
            out_ref[rows] = h2
            xn_ref[rows] = _rms(h2, g_pre_ref[...]).astype(BF16)

    def head_pieces(r0):
        return [functools.partial(interleave_merge, 0, r0),
                functools.partial(interleave_merge, 1, r0),
                functools.partial(out_projection, 0, r0),
                functools.partial(out_projection, 1, r0),
                functools.partial(interleave_merge, 2, r0),
                functools.partial(interleave_merge, 3, r0),
                functools.partial(out_projection, 2, r0)] + [
                    functools.partial(attention_residual, r0 + j * quarter, quarter)
                    for j in range(4)]

    def ffn_residual(r0, n):
        for rows in sub_chunks(r0, n):
            h3 = out_ref[rows] + 0.5 * _rms(f_ref[rows], g_post_ref[...])
            out_ref[rows] = h3
            xn_ref[rows] = _rms(h3, g_ple_pre_ref[...]).astype(BF16)

    def gate(r0):
        rows = slice(r0, r0 + half)
        att_ref[rows] = _sigmoid(
            jnp.dot(xn_ref[rows], wgate_ref[...], preferred_element_type=F32))

    def embedding(r0):
        rows = slice(r0, r0 + half)
        att_ref[rows] = att_ref[rows] * jnp.dot(
            p_ref[rows].astype(BF16), wproj_ref[...], preferred_element_type=F32)

    def embedding_residual(r0, n):
        for rows in sub_chunks(r0, n):
            out_ref[rows] = out_ref[rows] + _rms(att_ref[rows], g_ple_post_ref[...])

    def tail_pieces(r0):
        return ([functools.partial(ffn_residual, r0 + j * quarter, quarter) for j in range(4)]
                + [functools.partial(gate, r0), functools.partial(embedding, r0)]
                + [functools.partial(embedding_residual, r0 + j * quarter, quarter)
                   for j in range(4)])

    def ffn(r0, pieces):
        rows = slice(r0, r0 + half)
        f_ref[rows] = _swiglu(xn_ref.at[rows], wgu_ref, wd_ref, act_ref, dict(enumerate(pieces)))

    for piece in head_pieces(0):
        piece()
    ffn(0, head_pieces(half))
    ffn(half, tail_pieces(0))
    for piece in tail_pieces(half):
        piece()


def _stage3(h1, mix_a, outs_b, lses_b, p2, w_out, b_out, g_attn_post, g_pre, wgu, wd, g_post,
            g_ple_pre, w_gate, w_proj, g_ple_post, seq):
    tokens, d_model = h1.shape
    tm = TOKEN_TILE
    tiles_per_seq = seq // tm
    d_ff = wd.shape[0]
    half = mix_a.shape[1]
    n_dil = len(DILATIONS)

    def rows(width):
        return pl.BlockSpec((tm, width), lambda i: (i, 0))

    def class_spec(dil):
        return pl.BlockSpec((None, dil, tm // dil, half),
                            lambda i: (i // tiles_per_seq, 0, i % tiles_per_seq, 0))

    vec = _const_spec((1, d_model))
    hbm = pl.BlockSpec(memory_space=pl.ANY)
    return pl.pallas_call(
        _stage3_kernel,
        grid=(tokens // tm,),
        in_specs=[rows(d_model), rows(half)]
        + [class_spec(d) for d in DILATIONS] * 2
        + [
            rows(p2.shape[1]),
            hbm, vec, vec, vec,
            hbm, hbm, vec, vec,
            hbm, hbm, vec,
        ],
        out_specs=rows(d_model),
        out_shape=jax.ShapeDtypeStruct((tokens, d_model), F32),
        scratch_shapes=[
            pltpu.VMEM((tm, d_model), BF16),
            pltpu.VMEM((tm // 2, d_ff), BF16),
            pltpu.VMEM((tm, half), BF16),
            pltpu.VMEM((2 * (n_dil - 1) * (half // SLAB), tm, SLAB), F32),
            pltpu.VMEM((2 * (half // SLAB), tm // 2, SLAB), F32),
            pltpu.VMEM((tm, d_model), F32),
            pltpu.VMEM((tm, d_model), F32),
            pltpu.VMEM(w_out.shape, BF16),
            pltpu.VMEM(wgu.shape, BF16),
            pltpu.VMEM(wd.shape, BF16),
            pltpu.VMEM(w_gate.shape, BF16),
            pltpu.VMEM(w_proj.shape, BF16),
            pltpu.VMEM((2, WEIGHT_STAGE_ROWS, d_model), F32),
            pltpu.SemaphoreType.DMA((2,)),
        ],
        compiler_params=pltpu.CompilerParams(dimension_semantics=("arbitrary",),
                                             vmem_limit_bytes=VMEM_LIMIT_TOKEN_STAGES),
        name="outproj_ffn2_ple",
    )(h1, mix_a, *outs_b, *lses_b, p2, w_out, b_out, g_attn_post, g_pre, wgu, wd, g_post,
      g_ple_pre, w_gate, w_proj, g_ple_post)


def kernel(x, p, rel_bias, ffn1_pre_g, ffn1_w_gu, ffn1_w_down, ffn1_post_g, attn_pre_g, w_in, b_in,
           sinks, w_out, b_out, attn_post_g, ffn2_pre_g, ffn2_w_gu, ffn2_w_down, ffn2_post_g,
           ple_pre_g, w_ple_gate, w_ple_proj, ple_post_g):
    batch, seq, d_model = x.shape
    depth = p.shape[0]
    assert seq % (max(DILATIONS) * ATTN_BLOCK) == 0 and seq % TOKEN_TILE == 0
    assert x.dtype == F32

    bias = _bias_tables(rel_bias)
    h = x.reshape(batch * seq, d_model)
    for i in range(depth):
        row = lambda v: v[i].reshape(1, -1)
        h1, qa, kva, qb, kb, vb = _stage1(
            h, row(ffn1_pre_g), ffn1_w_gu[i], ffn1_w_down[i],
            row(ffn1_post_g), row(attn_pre_g), w_in[i], row(b_in), batch, seq)
        mix_a = _attn_a(qa, kva, bias[0], sinks[i])
        outs_b, lses_b = [], []
        for pat in range(len(DILATIONS)):
            o, lse = _attn_b(qb[pat], kb[pat], vb[pat], bias[1 + pat])
            outs_b.append(o)
            lses_b.append(lse)
        h = _stage3(
            h1, mix_a, outs_b, lses_b, p[i].reshape(batch * seq, -1),
            w_out[i], row(b_out), row(attn_post_g), row(ffn2_pre_g),
            ffn2_w_gu[i], ffn2_w_down[i], row(ffn2_post_g),
            row(ple_pre_g), w_ple_gate[i], w_ple_proj[i],
            row(ple_post_g), seq)
    return h.reshape(batch, seq, d_model)
```

```python
import functools
import math

import numpy as np
import jax
import jax.numpy as jnp
from jax import lax
from jax.experimental import pallas as pl
from jax.experimental.pallas import tpu as pltpu

HEAD_DIM = 64
A_Q_HEADS = 8
A_KV_HEADS = 2
A_WINDOW = 128
B_PATTERNS = ((128, 1), (512, 4), (2048, 16))
DILATIONS = tuple(d for _, d in B_PATTERNS)
NUM_BUCKETS = 32
MAX_DISTANCE = 2048
EPS = 1e-6
NEG_INF = -1e30

ATTN_BLOCK = 128
LANE_BLOCK = 256
HEADS_PER_LANE_BLOCK = LANE_BLOCK // HEAD_DIM
SLAB = 128
SLABS_PER_LANE_BLOCK = LANE_BLOCK // SLAB
STAGING_SLOTS = 2
FF_CHUNK = 256
TOKEN_TILE = 512
ROW_CHUNK = 16
WEIGHT_STAGE_ROWS = 512
ATTN_ROWS = 2048
V7X_VMEM_BYTES = 64 * 1024 * 1024
VMEM_LIMIT_TOKEN_STAGES = V7X_VMEM_BYTES - 6 * 1024 * 1024
VMEM_LIMIT_ATTENTION = V7X_VMEM_BYTES - 24 * 1024 * 1024

F32 = jnp.float32
BF16 = jnp.bfloat16


def _rms(x, g):
    return x * lax.rsqrt(jnp.mean(x * x, axis=-1, keepdims=True) + EPS) * g


def _row_chunks(n_rows):
    return [slice(r, r + ROW_CHUNK) for r in range(0, n_rows, ROW_CHUNK)]


def _sigmoid(x):
    return 1.0 / (1.0 + jnp.exp(-x))


def _swiglu(xn_ref, wgu_ref, wd_ref, act_ref, side_work=None):
    d_ff = wd_ref.shape[0]
    side_work = dict(side_work or {})
    for c in range(d_ff // FF_CHUNK):
        lo = c * FF_CHUNK
        g = jnp.dot(xn_ref[...], wgu_ref[:, lo:lo + FF_CHUNK], preferred_element_type=F32)
        u = jnp.dot(xn_ref[...], wgu_ref[:, d_ff + lo:d_ff + lo + FF_CHUNK],
                    preferred_element_type=F32)
        act_ref[:, lo:lo + FF_CHUNK] = ((g * _sigmoid(g)) * u).astype(BF16)
        if c in side_work:
            side_work.pop(c)()
    assert not side_work
    return jnp.dot(act_ref[...], wd_ref[...], preferred_element_type=F32)


def _load_as_bf16(src_hbm, dst_ref, stage_ref, sem_ref):
    s_rows, s_cols = stage_ref.shape[1:]
    rows, cols = src_hbm.shape
    pieces = [(r, min(s_rows, rows - r), c, min(s_cols, cols - c))
              for r in range(0, rows, s_rows) for c in range(0, cols, s_cols)]

    def copy(i):
        r, nr, c, nc = pieces[i]
        return pltpu.make_async_copy(src_hbm.at[pl.ds(r, nr), pl.ds(c, nc)],
                                     stage_ref.at[i % 2, pl.ds(0, nr), pl.ds(0, nc)],
                                     sem_ref.at[i % 2])

    copy(0).start()
    for i, (r, nr, c, nc) in enumerate(pieces):
        if i + 1 < len(pieces):
            copy(i + 1).start()
        copy(i).wait()
        dst_ref[r:r + nr, c:c + nc] = stage_ref[i % 2, :nr, :nc].astype(BF16)


def _const_spec(shape):
    return pl.BlockSpec(shape, lambda *_: (0,) * len(shape), pipeline_mode=pl.Buffered(1))


def _t5_bucket_np(dist):
    max_exact = NUM_BUCKETS // 2
    n = np.maximum(dist, 0)
    nf = np.maximum(n, 1).astype(np.float32)
    large = max_exact + (np.log(nf / np.float32(max_exact))
                         / np.float32(math.log(MAX_DISTANCE / max_exact))
                         * np.float32(NUM_BUCKETS - max_exact)).astype(np.int32)
    large = np.minimum(large, NUM_BUCKETS - 1)
    return np.where(n < max_exact, n, large).astype(np.int32)


def _bucket_tables():
    q = np.arange(ATTN_BLOCK)[:, None]
    k = np.arange(2 * ATTN_BLOCK)[None, :]
    rel = ATTN_BLOCK + q - k
    tables = []
    for max_dist, stride in ((A_WINDOW - 1, 1),) + tuple((w // d, d) for w, d in B_PATTERNS):
        valid = (rel >= 0) & (rel <= max_dist)
        bkt = _t5_bucket_np(rel * stride)
        later = np.where(valid, bkt, -1)
        first = np.where(valid & (k >= ATTN_BLOCK), bkt, -1)
        tables.append(np.stack([first, later]))
    return np.stack(tables).astype(np.int32)


def _bias_kernel(buckets_present, rb_ref, bkt_ref, out_ref):
    head = pl.program_id(0)
    first_block_keys = lax.broadcasted_iota(jnp.int32, bkt_ref.shape[2:], 1) >= ATTN_BLOCK
    for pattern, present in enumerate(buckets_present):
        col = head if pattern == 0 else head + A_Q_HEADS
        bkt = bkt_ref[pattern, 1]
        later = jnp.full(bkt.shape, NEG_INF, F32)
        for b in present:
            later = jnp.where(bkt == b, rb_ref[b, col], later)
        out_ref[pattern, 1] = later
        out_ref[pattern, 0] = jnp.where(first_block_keys, later, NEG_INF)


def _bias_tables(rel_bias):
    bkt_np = _bucket_tables()
    assert (bkt_np[:, 0] == np.where(np.arange(2 * ATTN_BLOCK) >= ATTN_BLOCK,
                                     bkt_np[:, 1], -1)).all()
    buckets_present = tuple(tuple(int(b) for b in np.unique(t[1]) if b >= 0) for t in bkt_np)
    n_pat = bkt_np.shape[0]
    table = (2, ATTN_BLOCK, 2 * ATTN_BLOCK)
    out = pl.pallas_call(
        functools.partial(_bias_kernel, buckets_present),
        grid=(A_Q_HEADS,),
        in_specs=[
            pl.BlockSpec(memory_space=pltpu.SMEM),
            pl.BlockSpec((n_pat,) + table, lambda h: (0, 0, 0, 0)),
        ],
        out_specs=pl.BlockSpec((n_pat, None) + table, lambda h: (0, h, 0, 0, 0)),
        out_shape=jax.ShapeDtypeStruct((n_pat, A_Q_HEADS) + table, F32),
        compiler_params=pltpu.CompilerParams(dimension_semantics=("arbitrary",)),
        name="bias_tables",
    )(rel_bias, jnp.asarray(bkt_np))
    return out.reshape(n_pat, A_Q_HEADS * 2, ATTN_BLOCK, 2 * ATTN_BLOCK)


def _stage1_kernel(x_ref, xprev_ref, g_pre_ref, wgu_hbm, wd_hbm, g_post_ref, g_attn_ref, win_hbm,
                   bin_ref, h_ref, qa_ref, kva_ref, *rest):
    n_dil = len(DILATIONS)
    qb_refs, kb_refs, vb_refs = rest[:n_dil], rest[n_dil:2 * n_dil], rest[2 * n_dil:3 * n_dil]
    (xn_ref, act_ref, f_ref, xn2_ref, slab_ref, slab2_ref,
     wgu_ref, wd_ref, win_ref, sem_ref) = rest[3 * n_dil:]
    tm = x_ref.shape[0]
    step = pl.program_id(0)
    rd_slot = (step + 1) % 2
    wr_slot = step % 2

    @pl.when(step == 0)
    def _():
        _load_as_bf16(wgu_hbm, wgu_ref, f_ref, sem_ref)
        _load_as_bf16(wd_hbm, wd_ref, f_ref, sem_ref)
        _load_as_bf16(win_hbm, win_ref, f_ref, sem_ref)
        f_ref[1] = jnp.zeros(f_ref.shape[1:], F32)

    for rows in _row_chunks(tm):
        xn_ref[rows] = _rms(x_ref[rows], g_pre_ref[...]).astype(BF16)
    for rows in _row_chunks(tm):
        h = xprev_ref[rows] + 0.5 * _rms(f_ref[rd_slot, rows], g_post_ref[...])
        h_ref[rows] = h
        xn2_ref[rows] = _rms(h, g_attn_ref[...]).astype(BF16)

    def proj(col_block):
        lo = col_block * LANE_BLOCK
        return (jnp.dot(xn2_ref[...], win_ref[:, lo:lo + LANE_BLOCK], preferred_element_type=F32)
                + bin_ref[:, lo:lo + LANE_BLOCK])

    def gather_rows(src_ref, base, start, n, stride):
        return [src_ref[base + s, pl.ds(start, n, stride=stride), :]
                for s in range(SLABS_PER_LANE_BLOCK)]

    def write_classes(z, refs, jb, slot):
        ref1, ref4, ref16 = refs
        base = slot * SLABS_PER_LANE_BLOCK
        ref1[jb, 0] = z.astype(BF16)
        for s in range(SLABS_PER_LANE_BLOCK):
            slab_ref[base + s] = z[:, s * SLAB:(s + 1) * SLAB]
        n4 = tm // 4
        for c4 in range(4):
            pieces = gather_rows(slab_ref, base, c4, n4, 4)
            ref4[jb, c4] = jnp.concatenate(pieces, axis=1).astype(BF16)
            for s in range(SLABS_PER_LANE_BLOCK):
                slab2_ref[base + s, c4 * n4:(c4 + 1) * n4, :] = pieces[s]
        for c4 in range(4):
            for j in range(4):
                pieces = gather_rows(slab2_ref, base, c4 * n4 + j, n4 // 4, 4)
                ref16[jb, c4 + 4 * j] = jnp.concatenate(pieces, axis=1).astype(BF16)

    q_scale = HEAD_DIM ** -0.5
    n_slots = slab_ref.shape[0] // SLABS_PER_LANE_BLOCK

    def store_kva():
        z = proj(2)
        heads = [z[:, h * HEAD_DIM:(h + 1) * HEAD_DIM] for h in range(LANE_BLOCK // HEAD_DIM)]
        kva_ref[...] = jnp.concatenate([h for h in heads for _ in range(2)], axis=1).astype(BF16)

    def store_qa(j):
        qa_ref[j] = (proj(j) * q_scale).astype(BF16)

    def store_classes(col_block, scale, refs, jb, slot):
        z = proj(col_block)
        write_classes(z if scale == 1.0 else z * scale, refs, jb, slot % n_slots)

    projections = [store_kva]
    for j in range(2):
        projections += [
            functools.partial(store_qa, j),
            functools.partial(store_classes, 3 + j, q_scale, qb_refs, j, 3 * j),
            functools.partial(store_classes, 5 + j, 1.0, kb_refs, j, 3 * j + 1),
            functools.partial(store_classes, 7 + j, 1.0, vb_refs, j, 3 * j + 2),
        ]
    side_work = {1 + c: work for c, work in enumerate(projections)}
    f_ref[wr_slot] = _swiglu(xn_ref, wgu_ref, wd_ref, act_ref, side_work)


def _stage1(x2, g_pre, wgu, wd, g_post, g_attn, w_in, b_in, batch, seq):
    tokens, d_model = x2.shape
    tm = TOKEN_TILE
    tiles_per_seq = seq // tm
    d_ff = wd.shape[0]
    n_blk = 2
    n_dil = len(DILATIONS)
    n_tiles = tokens // tm
    assert DILATIONS == (1, 4, 16) and tm % (16 * 16) == 0

    def lagged(i):
        j = jnp.maximum(i - 1, 0)
        return j // tiles_per_seq, j % tiles_per_seq

    def class_spec(dil):
        return pl.BlockSpec((None, n_blk, dil, tm // dil, LANE_BLOCK),
                            lambda i: (lagged(i)[0], 0, 0, lagged(i)[1], 0))

    def class_shape(dil):
        return jax.ShapeDtypeStruct((batch, n_blk, dil, seq // dil, LANE_BLOCK), BF16)

    row_spec = pl.BlockSpec((tm, d_model), lambda i: (jnp.minimum(i, n_tiles - 1), 0))
    lagged_row_spec = pl.BlockSpec((tm, d_model), lambda i: (jnp.maximum(i - 1, 0), 0))
    hbm_spec = pl.BlockSpec(memory_space=pl.ANY)
    outs = pl.pallas_call(
        _stage1_kernel,
        grid=(n_tiles + 1,),
        in_specs=[
            row_spec,
            lagged_row_spec,
            _const_spec((1, d_model)),
            hbm_spec,
            hbm_spec,
            _const_spec((1, d_model)),
            _const_spec((1, d_model)),
            hbm_spec,
            _const_spec(b_in.shape),
        ],
        out_specs=[
            lagged_row_spec,
            pl.BlockSpec((None, n_blk, tm, LANE_BLOCK),
                         lambda i: (lagged(i)[0], 0, lagged(i)[1], 0)),
            pl.BlockSpec((None, tm, 2 * LANE_BLOCK),
                         lambda i: (lagged(i)[0], lagged(i)[1], 0)),
        ] + [class_spec(d) for _ in range(3) for d in DILATIONS],
        out_shape=[
            jax.ShapeDtypeStruct((tokens, d_model), F32),
            jax.ShapeDtypeStruct((batch, n_blk, seq, LANE_BLOCK), BF16),
            jax.ShapeDtypeStruct((batch, seq, 2 * LANE_BLOCK), BF16),
        ] + [class_shape(d) for _ in range(3) for d in DILATIONS],
        scratch_shapes=[
            pltpu.VMEM((tm, d_model), BF16),
            pltpu.VMEM((tm, d_ff), BF16),
            pltpu.VMEM((2, tm, d_model), F32),
            pltpu.VMEM((tm, d_model), BF16),
            pltpu.VMEM((STAGING_SLOTS * SLABS_PER_LANE_BLOCK, tm, SLAB), F32),
            pltpu.VMEM((STAGING_SLOTS * SLABS_PER_LANE_BLOCK, tm, SLAB), F32),
            pltpu.VMEM(wgu.shape, BF16),
            pltpu.VMEM(wd.shape, BF16),
            pltpu.VMEM(w_in.shape, BF16),
            pltpu.SemaphoreType.DMA((2,)),
        ],
        compiler_params=pltpu.CompilerParams(dimension_semantics=("arbitrary",),
                                             vmem_limit_bytes=VMEM_LIMIT_TOKEN_STAGES),
        name="ffn1_inproj",
    )(x2, x2, g_pre, wgu, wd, g_post, g_attn, w_in, b_in)
    h1, qa, kva = outs[:3]
    qb, kb, vb = outs[3:3 + n_dil], outs[3 + n_dil:3 + 2 * n_dil], outs[3 + 2 * n_dil:]
    return h1, qa, kva, qb, kb, vb


def _lane_is_low(shape):
    return lax.broadcasted_iota(jnp.int32, shape, 1) < HEAD_DIM


def _masked_kv(k_grp, v_grp):
    low = _lane_is_low(k_grp.shape)
    zero = jnp.zeros_like(k_grp)
    lane = lax.broadcasted_iota(jnp.int32, k_grp.shape, 1)
    ones_low = jnp.clip(HEAD_DIM - lane, 0, 1).astype(F32).astype(BF16)
    ones_high = jnp.clip(lane - (HEAD_DIM - 1), 0, 1).astype(F32).astype(BF16)
    k_par = (jnp.where(low, k_grp, zero), jnp.where(low, zero, k_grp))
    v_par = (jnp.concatenate([jnp.where(low, v_grp, zero), ones_low], axis=1),
             jnp.concatenate([jnp.where(low, zero, v_grp), ones_high], axis=1))
    return k_par, v_par


def _pair_attention(q, k_par, v_par, biases):
    pair = 2 * HEAD_DIM
    acc = None
    maxes = []
    for par in range(2):
        s = lax.dot_general(q, k_par[par], (((1,), (1,)), ((), ())),
                            preferred_element_type=F32) + biases[par]
        m = jnp.max(s, axis=-1, keepdims=True)
        p = jnp.exp(s - m).astype(BF16)
        pv = jnp.dot(p, v_par[par], preferred_element_type=F32)
        acc = pv if acc is None else acc + pv
        maxes.append(m)
    m_pair = jnp.where(_lane_is_low((q.shape[0], pair)), maxes[0], maxes[1])
    return acc[:, :pair], acc[:, pair:], m_pair


def _attn_b_kernel(q_ref, kp_ref, kc_ref, vp_ref, vc_ref, bias_ref, o_ref, lse_ref):
    first_chunk = pl.program_id(2) == 0
    n_blk, n_cls, rows, _ = q_ref.shape
    pair = 2 * HEAD_DIM
    for cls in range(n_cls):
        for jb in range(n_blk):
            k_all = jnp.concatenate([kp_ref[jb, cls], kc_ref[jb, cls]], axis=0)
            v_all = jnp.concatenate([vp_ref[jb, cls], vc_ref[jb, cls]], axis=0)
            for grp in range(LANE_BLOCK // pair):
                lanes = slice(grp * pair, (grp + 1) * pair)
                k_par, v_par = _masked_kv(k_all[:, lanes], v_all[:, lanes])
                for t in range(rows // ATTN_BLOCK):
                    q = q_ref[jb, cls, t * ATTN_BLOCK:(t + 1) * ATTN_BLOCK, lanes]
                    keys = slice(t * ATTN_BLOCK, (t + 2) * ATTN_BLOCK)
                    variant = jnp.where(first_chunk, 0, 1) if t == 0 else 1
                    head = jb * HEADS_PER_LANE_BLOCK + grp * 2
                    biases = [bias_ref[(head + par) * 2 + variant] for par in range(2)]
                    o, den, m = _pair_attention(q, [k[keys] for k in k_par],
                                                [v[keys] for v in v_par], biases)
                    out_lanes = slice(jb * LANE_BLOCK + grp * pair,
                                      jb * LANE_BLOCK + (grp + 1) * pair)
                    out_rows = slice(t * ATTN_BLOCK, (t + 1) * ATTN_BLOCK)
                    o_ref[cls, out_rows, out_lanes] = (o / den).astype(BF16)
                    lse_ref[cls, out_rows, out_lanes] = m + jnp.log(den)


def _attn_b(qb, kb, vb, bias):
    batch, n_blk, n_cls, class_len, _ = qb.shape
    rows = min(ATTN_ROWS, class_len)
    cls_per_step = min(n_cls, ATTN_ROWS // rows)
    prev_per_chunk = rows // ATTN_BLOCK

    cur_spec = pl.BlockSpec((None, n_blk, cls_per_step, rows, LANE_BLOCK),
                            lambda b, c, i: (b, 0, c, i, 0))
    prev_spec = pl.BlockSpec(
        (None, n_blk, cls_per_step, ATTN_BLOCK, LANE_BLOCK),
        lambda b, c, i: (b, 0, c, jnp.maximum(i * prev_per_chunk - 1, 0), 0))
    width = n_blk * LANE_BLOCK
    out_spec = pl.BlockSpec((None, cls_per_step, rows, width), lambda b, c, i: (b, c, i, 0))
    return pl.pallas_call(
        _attn_b_kernel,
        grid=(batch, n_cls // cls_per_step, class_len // rows),
        in_specs=[cur_spec, prev_spec, cur_spec, prev_spec, cur_spec,
                  _const_spec(bias.shape)],
        out_specs=[out_spec, out_spec],
        out_shape=[jax.ShapeDtypeStruct((batch, n_cls, class_len, width), BF16),
                   jax.ShapeDtypeStruct((batch, n_cls, class_len, width), F32)],
        compiler_params=pltpu.CompilerParams(
            dimension_semantics=("arbitrary", "arbitrary", "arbitrary"),
            vmem_limit_bytes=VMEM_LIMIT_ATTENTION),
        name=f"mixer_b_dil{n_cls}",
    )(qb, kb, kb, vb, vb, bias)


def _attn_a_kernel(sink_ref, q_ref, kvp_ref, kvc_ref, bias_ref, o_ref, chunk_axis=2):
    first_chunk = pl.program_id(chunk_axis) == 0
    rows = q_ref.shape[1]
    pair = 2 * HEAD_DIM
    low_out = _lane_is_low((ATTN_BLOCK, pair))
    kv_all = jnp.concatenate([kvp_ref[...], kvc_ref[...]], axis=0)
    for g in range(A_KV_HEADS):
        k_par, v_par = _masked_kv(kv_all[:, g * pair:(g + 1) * pair],
                                  kv_all[:, (A_KV_HEADS + g) * pair:(A_KV_HEADS + g + 1) * pair])
        for grp in range(LANE_BLOCK // pair):
            lanes = slice(grp * pair, (grp + 1) * pair)
            head = g * HEADS_PER_LANE_BLOCK + grp * 2
            sink = jnp.where(low_out, sink_ref[head], sink_ref[head + 1])
            for t in range(rows // ATTN_BLOCK):
                q = q_ref[g, t * ATTN_BLOCK:(t + 1) * ATTN_BLOCK, lanes]
                keys = slice(t * ATTN_BLOCK, (t + 2) * ATTN_BLOCK)
                variant = jnp.where(first_chunk, 0, 1) if t == 0 else 1
                biases = [bias_ref[(head + par) * 2 + variant] for par in range(2)]
                o, den, m = _pair_attention(q, [k[keys] for k in k_par],
                                            [v[keys] for v in v_par], biases)
                total = den + jnp.exp(sink - m)
                out_lanes = slice(g * LANE_BLOCK + grp * pair, g * LANE_BLOCK + (grp + 1) * pair)
                o_ref[t * ATTN_BLOCK:(t + 1) * ATTN_BLOCK, out_lanes] = (o / total).astype(BF16)


def _attn_local_kernel(sink_ref, qa_ref, kvap_ref, kvac_ref, biasa_ref,
                       q_ref, kp_ref, kc_ref, vp_ref, vc_ref, biasb_ref,
                       oa_ref, o_ref, lse_ref):
    _attn_a_kernel(sink_ref, qa_ref, kvap_ref, kvac_ref, biasa_ref, oa_ref)
    _attn_b_kernel(q_ref, kp_ref, kc_ref, vp_ref, vc_ref, biasb_ref, o_ref, lse_ref)


def _attn_local(qa, kva, bias_a, sinks, qb, kb, vb, bias_b):
    batch, n_blk, seq, _ = qa.shape
    rows = min(ATTN_ROWS, seq)
    prev_per_chunk = rows // ATTN_BLOCK
    width = n_blk * LANE_BLOCK

    def prev_block(i):
        return jnp.maximum(i * prev_per_chunk - 1, 0)

    cur_spec = pl.BlockSpec((None, n_blk, 1, rows, LANE_BLOCK), lambda b, c, i: (b, 0, 0, i, 0))
    prev_spec = pl.BlockSpec((None, n_blk, 1, ATTN_BLOCK, LANE_BLOCK),
                             lambda b, c, i: (b, 0, 0, prev_block(i), 0))
    out_spec = pl.BlockSpec((None, 1, rows, width), lambda b, c, i: (b, 0, i, 0))
    mix_a, o, lse = pl.pallas_call(
        _attn_local_kernel,
        grid=(batch, 1, seq // rows),
        in_specs=[
            pl.BlockSpec(memory_space=pltpu.SMEM),
            pl.BlockSpec((None, n_blk, rows, LANE_BLOCK), lambda b, c, i: (b, 0, i, 0)),
            pl.BlockSpec((None, ATTN_BLOCK, kva.shape[2]), lambda b, c, i: (b, prev_block(i), 0)),
            pl.BlockSpec((None, rows, kva.shape[2]), lambda b, c, i: (b, i, 0)),
            _const_spec(bias_a.shape),
            cur_spec, prev_spec, cur_spec, prev_spec, cur_spec,
            _const_spec(bias_b.shape),
        ],
        out_specs=[pl.BlockSpec((None, rows, width), lambda b, c, i: (b, i, 0)),
                   out_spec, out_spec],
        out_shape=[jax.ShapeDtypeStruct((batch, seq, width), BF16),
                   jax.ShapeDtypeStruct((batch, 1, seq, width), BF16),
                   jax.ShapeDtypeStruct((batch, 1, seq, width), F32)],
        compiler_params=pltpu.CompilerParams(
            dimension_semantics=("arbitrary", "arbitrary", "arbitrary"),
            vmem_limit_bytes=VMEM_LIMIT_TOKEN_STAGES),
        name="mixer_a_and_b_dil1",
    )(sinks, qa, kva, kva, bias_a, qb, kb, kb, vb, vb, bias_b)
    return mix_a.reshape(batch * seq, width), o, lse


def _stage3_kernel(*refs):
    n_dil = len(DILATIONS)
    h_ref, mixa_ref = refs[:2]
    o_refs = refs[2:2 + n_dil]
    l_refs = refs[2 + n_dil:2 + 2 * n_dil]
    (p_ref, wout_hbm, bout_ref, g_attn_post_ref, g_pre_ref, wgu_hbm, wd_hbm, g_post_ref,
     g_ple_pre_ref, wgate_hbm, wproj_hbm, g_ple_post_ref,
     out_ref, xn_ref, act_ref, mixb_ref, slab_ref, mid_ref, att_ref, f_ref,
     wout_ref, wgu_ref, wd_ref, wgate_ref, wproj_ref, stage_ref, sem_ref) = refs[2 + 2 * n_dil:]
    tm = h_ref.shape[0]
    n_slabs = mixa_ref.shape[1] // SLAB

    @pl.when(pl.program_id(0) == 0)
    def _():
        for src, dst in ((wout_hbm, wout_ref), (wgu_hbm, wgu_ref), (wd_hbm, wd_ref),
                         (wgate_hbm, wgate_ref), (wproj_hbm, wproj_ref)):
            _load_as_bf16(src, dst, stage_ref, sem_ref)

    d_a = mixa_ref.shape[1]
    half = tm // 2
    quarter = half // 4

    def sub_chunks(r0, n):
        return [slice(r, r + ROW_CHUNK) for r in range(r0, r0 + n, ROW_CHUNK)]

    def interleave_merge(s, r0):
        lanes = slice(s * SLAB, (s + 1) * SLAB)
        for k, dil in enumerate(d for d in DILATIONS if d != 1):
            idx = DILATIONS.index(dil)
            n = half // dil
            for src_ref, which in ((o_refs[idx], 0), (l_refs[idx], 1)):
                dst = (2 * k + which) * n_slabs + s
                if dil == 4:
                    sub = 2 * ROW_CHUNK
                    for c in range(dil):
                        for u0 in range(r0 // dil, r0 // dil + n, sub):
                            blk = src_ref[c, u0:u0 + sub, lanes].astype(F32)
                            slab_ref[dst, pl.ds(c + u0 * dil, sub, stride=dil), :] = blk
                else:
                    mid = which * n_slabs + s
                    n4 = half // 4
                    for c4 in range(4):
                        for j in range(4):
                            blk = src_ref[c4 + 4 * j, r0 // dil:r0 // dil + n, lanes].astype(F32)
                            mid_ref[mid, pl.ds(c4 * n4 + j, n, stride=4), :] = blk
                        for w0 in range(0, n4, 2 * ROW_CHUNK):
                            piece = mid_ref[mid, c4 * n4 + w0:c4 * n4 + w0 + 2 * ROW_CHUNK, :]
                            slab_ref[dst, pl.ds(r0 + c4 + 4 * w0, 2 * ROW_CHUNK, stride=4), :] = piece
        for q0 in range(r0, r0 + half, 2 * ROW_CHUNK):
            rows = slice(q0, q0 + 2 * ROW_CHUNK)
            outs, lses = [], []
            k = 0
            for idx, dil in enumerate(DILATIONS):
                if dil == 1:
                    outs.append(o_refs[idx][0, rows, lanes].astype(F32))
                    lses.append(l_refs[idx][0, rows, lanes])
                else:
                    outs.append(slab_ref[2 * k * n_slabs + s, rows, :])
                    lses.append(slab_ref[(2 * k + 1) * n_slabs + s, rows, :])
                    k += 1
            l_max = lses[0]
            for l in lses[1:]:
                l_max = jnp.maximum(l_max, l)
            weights = [jnp.exp(l - l_max) for l in lses]
            num = weights[0] * outs[0]
            den = weights[0]
            for w, o in zip(weights[1:], outs[1:]):
                num = num + w * o
                den = den + w
            mixb_ref[rows, lanes] = (num / den).astype(BF16)

    def out_projection(part, r0):
        rows = slice(r0, r0 + half)
        if part == 0:
            att_ref[rows] = (jnp.dot(mixa_ref[rows], wout_ref[:d_a, :], preferred_element_type=F32)
                             + bout_ref[...])
        else:
            lo = (part - 1) * 2 * SLAB
            hi = lo + 2 * SLAB
            att_ref[rows] = att_ref[rows] + jnp.dot(
                mixb_ref[rows, lo:hi], wout_ref[d_a + lo:d_a + hi, :], preferred_element_type=F32)

    def attention_residual(r0, n):
        for rows in sub_chunks(r0, n):
            h2 = h_ref[rows] + _rms(att_ref[rows], g_attn_post_ref[...])
            out_ref[rows] = h2
            xn_ref[rows] = _rms(h2, g_pre_ref[...]).astype(BF16)

    def head_pieces(r0):
        return [functools.partial(interleave_merge, 0, r0),
                functools.partial(interleave_merge, 1, r0),
                functools.partial(out_projection, 0, r0),
                functools.partial(out_projection, 1, r0),
                functools.partial(interleave_merge, 2, r0),
                functools.partial(interleave_merge, 3, r0),
                functools.partial(out_projection, 2, r0)] + [
                    functools.partial(attention_residual, r0 + j * quarter, quarter)
                    for j in range(4)]

    def ffn_residual(r0, n):
        for rows in sub_chunks(r0, n):
            h3 = out_ref[rows] + 0.5 * _rms(f_ref[rows], g_post_ref[...])
            out_ref[rows] = h3
            xn_ref[rows] = _rms(h3, g_ple_pre_ref[...]).astype(BF16)

    def gate(r0):
        rows = slice(r0, r0 + half)
        att_ref[rows] = _sigmoid(
            jnp.dot(xn_ref[rows], wgate_ref[...], preferred_element_type=F32))

    def embedding(r0):
        rows = slice(r0, r0 + half)
        att_ref[rows] = att_ref[rows] * jnp.dot(
            p_ref[rows].astype(BF16), wproj_ref[...], preferred_element_type=F32)

    def embedding_residual(r0, n):
        for rows in sub_chunks(r0, n):
            out_ref[rows] = out_ref[rows] + _rms(att_ref[rows], g_ple_post_ref[...])

    def tail_pieces(r0):
        return ([functools.partial(ffn_residual, r0 + j * quarter, quarter) for j in range(4)]
                + [functools.partial(gate, r0), functools.partial(embedding, r0)]
                + [functools.partial(embedding_residual, r0 + j * quarter, quarter)
                   for j in range(4)])

    def ffn(r0, pieces):
        rows = slice(r0, r0 + half)
        f_ref[rows] = _swiglu(xn_ref.at[rows], wgu_ref, wd_ref, act_ref, dict(enumerate(pieces)))

    for piece in head_pieces(0):
        piece()
    ffn(0, head_pieces(half))
    ffn(half, tail_pieces(0))
    for piece in tail_pieces(half):
        piece()


def _stage3(h1, mix_a, outs_b, lses_b, p2, w_out, b_out, g_attn_post, g_pre, wgu, wd, g_post,
            g_ple_pre, w_gate, w_proj, g_ple_post, seq):
    tokens, d_model = h1.shape
    tm = TOKEN_TILE
    tiles_per_seq = seq // tm
    d_ff = wd.shape[0]
    half = mix_a.shape[1]
    n_dil = len(DILATIONS)

    def rows(width):
        return pl.BlockSpec((tm, width), lambda i: (i, 0))

    def class_spec(dil):
        return pl.BlockSpec((None, dil, tm // dil, half),
                            lambda i: (i // tiles_per_seq, 0, i % tiles_per_seq, 0))

    vec = _const_spec((1, d_model))
    hbm = pl.BlockSpec(memory_space=pl.ANY)
    return pl.pallas_call(
        _stage3_kernel,
        grid=(tokens // tm,),
        in_specs=[rows(d_model), rows(half)]
        + [class_spec(d) for d in DILATIONS] * 2
        + [
            rows(p2.shape[1]),
            hbm, vec, vec, vec,
            hbm, hbm, vec, vec,
            hbm, hbm, vec,
        ],
        out_specs=rows(d_model),
        out_shape=jax.ShapeDtypeStruct((tokens, d_model), F32),
        scratch_shapes=[
            pltpu.VMEM((tm, d_model), BF16),
            pltpu.VMEM((tm // 2, d_ff), BF16),
            pltpu.VMEM((tm, half), BF16),
            pltpu.VMEM((2 * (n_dil - 1) * (half // SLAB), tm, SLAB), F32),
            pltpu.VMEM((2 * (half // SLAB), tm // 2, SLAB), F32),
            pltpu.VMEM((tm, d_model), F32),
            pltpu.VMEM((tm, d_model), F32),
            pltpu.VMEM(w_out.shape, BF16),
            pltpu.VMEM(wgu.shape, BF16),
            pltpu.VMEM(wd.shape, BF16),
            pltpu.VMEM(w_gate.shape, BF16),
            pltpu.VMEM(w_proj.shape, BF16),
            pltpu.VMEM((2, WEIGHT_STAGE_ROWS, d_model), F32),
            pltpu.SemaphoreType.DMA((2,)),
        ],
        compiler_params=pltpu.CompilerParams(dimension_semantics=("arbitrary",),
                                             vmem_limit_bytes=VMEM_LIMIT_TOKEN_STAGES),
        name="outproj_ffn2_ple",
    )(h1, mix_a, *outs_b, *lses_b, p2, w_out, b_out, g_attn_post, g_pre, wgu, wd, g_post,
      g_ple_pre, w_gate, w_proj, g_ple_post)


def kernel(x, p, rel_bias, ffn1_pre_g, ffn1_w_gu, ffn1_w_down, ffn1_post_g, attn_pre_g, w_in, b_in,
           sinks, w_out, b_out, attn_post_g, ffn2_pre_g, ffn2_w_gu, ffn2_w_down, ffn2_post_g,
           ple_pre_g, w_ple_gate, w_ple_proj, ple_post_g):
    batch, seq, d_model = x.shape
    depth = p.shape[0]
    assert seq % (max(DILATIONS) * ATTN_BLOCK) == 0 and seq % TOKEN_TILE == 0
    assert x.dtype == F32

    bias = _bias_tables(rel_bias)
    h = x.reshape(batch * seq, d_model)
    for i in range(depth):
        row = lambda v: v[i].reshape(1, -1)
        h1, qa, kva, qb, kb, vb = _stage1(
            h, row(ffn1_pre_g), ffn1_w_gu[i], ffn1_w_down[i],
            row(ffn1_post_g), row(attn_pre_g), w_in[i], row(b_in), batch, seq)
        assert DILATIONS[0] == 1
        mix_a, o, lse = _attn_local(qa, kva, bias[0], sinks[i], qb[0], kb[0], vb[0], bias[1])
        outs_b, lses_b = [o], [lse]
        for pat in range(1, len(DILATIONS)):
            o, lse = _attn_b(qb[pat], kb[pat], vb[pat], bias[1 + pat])
            outs_b.append(o)
            lses_b.append(lse)
        h = _stage3(
            h1, mix_a, outs_b, lses_b, p[i].reshape(batch * seq, -1),
            w_out[i], row(b_out), row(attn_post_g), row(ffn2_pre_g),
            ffn2_w_gu[i], ffn2_w_down[i], row(ffn2_post_g),
            row(ple_pre_g), w_ple_gate[i], w_ple_proj[i],
            row(ple_post_g), seq)
    return h.reshape(batch, seq, d_model)
```
